```python
import jax, jax.numpy as jnp
from jax import lax
import numpy as np

D_MODEL = 1024
BATCH = 2
SEQ = 8192
DEPTH = 2

CTX_LEN = 256
GRID_W = 64
EPS = 1e-6

GLA_HEADS = 4
GLA_DK = 32
GLA_DV = 64
GLA_QK = GLA_HEADS * GLA_DK
W_GLA = GLA_HEADS * GLA_DV
GLA_RANK = 16
GLA_TAU = 16.0
GLA_CHUNK = 64
GLA_GATE_BIAS = 2.0
FFT_GROUPS = 4
FFT_DG = 64
W_FFT = FFT_GROUPS * FFT_DG
W_CONV = 256
CONV_WIDTH = 3
POOL_WINDOWS = (2, 4, 8, 16)
POOL_GROUPS = 4
POOL_DG = 64
W_POOL = POOL_GROUPS * POOL_DG
D_MIX = W_GLA + W_FFT + W_CONV + W_POOL
D_FF = 2816

COL_SIZES = (GLA_QK, W_GLA, GLA_RANK, GLA_RANK, GLA_QK, W_GLA, W_FFT, W_CONV, W_CONV, W_CONV, W_POOL)
D_IN = sum(COL_SIZES)
SPLITS = tuple(int(s) for s in np.cumsum(COL_SIZES)[:-1])
KVA_COLS = GLA_QK + W_GLA + 2 * GLA_RANK

kernel_name = "hybrid_parallel_gla_fnet_conv_pool_dit"


def rmsnorm(x, g):
    xf = x.astype(jnp.float32)
    y = xf * lax.rsqrt(jnp.mean(xf * xf, axis=-1, keepdims=True) + EPS)
    return (y * g.astype(jnp.float32)).astype(x.dtype)


def on_grid(fn, u, grid):
    if not grid:
        return fn(u)
    b, n, ch = u.shape
    rows = n // GRID_W
    return fn(u.reshape(b, rows, GRID_W, ch)).reshape(b, n, ch)


def dwconv3(u, w, bias):
    pad = [(0, 0)] * (u.ndim - 2) + [(1, 1), (0, 0)]
    up = jnp.pad(u, pad)
    return up[..., :-2, :] * w[0] + up[..., 1:-1, :] * w[1] + up[..., 2:, :] * w[2] + bias


def pool_minus_self(u, window):
    n = u.shape[-2]
    t = np.arange(n)
    lo = np.clip(t - window // 2, 0, n - 1)
    hi = np.clip(t + window // 2 - 1, 0, n - 1)
    count = jnp.asarray((hi - lo + 1).astype(np.float32)[:, None])
    uf = u.astype(jnp.float32)
    cs = jnp.cumsum(uf, axis=-2)
    cs = jnp.concatenate([jnp.zeros_like(cs[..., :1, :]), cs], axis=-2)
    total = jnp.take(cs, hi + 1, axis=-2) - jnp.take(cs, lo, axis=-2)
    return (total / count - uf).astype(u.dtype)


def pool_mixer(u, w_pool, scale, grid):
    def f(z):
        return jnp.concatenate(
            [pool_minus_self(z[..., i * POOL_DG:(i + 1) * POOL_DG], w) for i, w in enumerate(POOL_WINDOWS)],
            axis=-1)
    p = on_grid(f, u, grid)
    b, n, _ = u.shape
    y = jnp.einsum('bngc,gcd->bngd', p.reshape(b, n, POOL_GROUPS, POOL_DG), w_pool)
    return y.reshape(b, n, W_POOL) * scale


def fourier_mixer(u, w_f):
    b, n, _ = u.shape
    uf = u.astype(jnp.float32).reshape(b, n, FFT_GROUPS, FFT_DG)
    f = jnp.fft.fft2(uf, axes=(1, 3), norm='ortho').real.astype(u.dtype)
    y = jnp.einsum('bngc,gcd->bngd', f, w_f)
    return y.reshape(b, n, W_FFT)


def gla_kv_decay(p_k, p_v, p_af, p_ab, w_a2, b_a2):
    b, n, _ = p_k.shape
    k = p_k.astype(jnp.float32).reshape(b, n, GLA_HEADS, GLA_DK)
    v = p_v.astype(jnp.float32).reshape(b, n, GLA_HEADS, GLA_DV)
    la_f = jax.nn.log_sigmoid((p_af @ w_a2[0] + b_a2[0]).astype(jnp.float32)) / GLA_TAU
    la_b = jax.nn.log_sigmoid((p_ab @ w_a2[1] + b_a2[1]).astype(jnp.float32)) / GLA_TAU
    return (k, v, la_f.reshape(b, n, GLA_HEADS, GLA_DK), la_b.reshape(b, n, GLA_HEADS, GLA_DK))


def gla_chunked(q, k, v, log_a, h0):
    b, n, h, dk = q.shape
    dv = v.shape[-1]
    nc = n // GLA_CHUNK
    causal = np.tril(np.ones((GLA_CHUNK, GLA_CHUNK), dtype=bool))[None, :, :, None, None]

    def to_chunks(t):
        return t.reshape(b, nc, GLA_CHUNK, h, t.shape[-1]).swapaxes(0, 1)

    def step(state, inp):
        qc, kc, vc, ac = inp
        cum = jnp.cumsum(ac, axis=1)
        o_inter = jnp.einsum('bihk,bhkv->bihv', qc * jnp.exp(cum), state)
        diff = cum[:, :, None] - cum[:, None, :]
        decay = jnp.where(causal, jnp.exp(jnp.minimum(diff, 0.0)), 0.0)
        attn = jnp.einsum('bihk,bjhk,bijhk->bhij', qc, kc, decay)
        o_intra = jnp.einsum('bhij,bjhv->bihv', attn, vc)
        last = cum[:, -1]
        k_dec = kc * jnp.exp(last[:, None] - cum)
        new_state = state * jnp.exp(last)[..., None] + jnp.einsum('bjhk,bjhv->bhkv', k_dec, vc)
        return new_state, o_intra + o_inter

    state, o = lax.scan(step, h0, (to_chunks(q), to_chunks(k), to_chunks(v), to_chunks(log_a)))
    return o.swapaxes(0, 1).reshape(b, n, h, dv), state


def gla_final_state(k, v, log_a):
    cum = jnp.cumsum(log_a, axis=1)
    return jnp.einsum('blhk,blhv->bhkv', k * jnp.exp(cum[:, -1:] - cum), v)


def mixers(p, grid, h0_f, h0_b, w_a2, b_a2, gla_g, fft_w, conv_w, conv_b, pool_w, pool_scale):
    p_k, p_v, p_af, p_ab, p_q, p_g, p_fft, p_h, p_bg, p_cg, p_pool = jnp.split(p, SPLITS, axis=-1)
    b, n, _ = p.shape
    k, v, la_f, la_b = gla_kv_decay(p_k, p_v, p_af, p_ab, w_a2, b_a2)
    q = p_q.astype(jnp.float32).reshape(b, n, GLA_HEADS, GLA_DK) * (GLA_DK ** -0.5)
    o_f, s_f = gla_chunked(q, k, v, la_f, h0_f)
    o_b, s_b = gla_chunked(jnp.flip(q, 1), jnp.flip(k, 1), jnp.flip(v, 1), jnp.flip(la_b, 1), h0_b)
    o = rmsnorm(o_f + jnp.flip(o_b, 1), gla_g)
    o = o * jax.nn.silu(p_g.astype(jnp.float32).reshape(b, n, GLA_HEADS, GLA_DV))
    y_gla = o.reshape(b, n, W_GLA).astype(p.dtype)
    y_fft = fourier_mixer(p_fft, fft_w)
    y_conv = p_bg * on_grid(lambda z: dwconv3(z, conv_w, conv_b), p_cg * p_h, grid)
    y_pool = pool_mixer(p_pool, pool_w, pool_scale, grid)
    return jnp.concatenate([y_gla, y_fft, y_conv, y_pool], axis=-1), (s_f, s_b)


def conv_ffn(h, grid, w_up, cw, cb, w_down):
    a, u = jnp.split(h @ w_up, 2, axis=-1)
    a = on_grid(lambda z: dwconv3(z, cw, cb), a, grid)
    return (jax.nn.silu(a) * u) @ w_down


def setup_inputs(seed: int = 0) -> dict:
    key = jax.random.key(seed)
    ks = jax.random.split(key, 23)
    D = D_MODEL

    def nrm(k, shape, scale):
        return jax.random.normal(k, shape, jnp.float32) * scale

    return {
        'x': nrm(ks[0], (BATCH, SEQ, D), 1.0),
        'c': nrm(ks[1], (BATCH, D), 1.0),
        'ctx': nrm(ks[2], (BATCH, CTX_LEN, D), 1.0),
        'c_ctx': nrm(ks[3], (D,), 1.0),
        'norm1_g': 1.0 + nrm(ks[4], (DEPTH, D), 0.02),
        'norm2_g': 1.0 + nrm(ks[5], (DEPTH, D), 0.02),
        'w_mod': nrm(ks[6], (DEPTH, D, 6 * D), 0.5 * D ** -0.5),
        'b_mod': nrm(ks[7], (DEPTH, 6 * D), 0.02),
        'w_in': nrm(ks[8], (DEPTH, D, D_IN), D ** -0.5),
        'gla_w_a2': nrm(ks[9], (DEPTH, 2, GLA_RANK, GLA_QK), GLA_RANK ** -0.5),
        'gla_b_a2': GLA_GATE_BIAS + nrm(ks[10], (DEPTH, 2, GLA_QK), 0.1),
        'gla_norm_g': 1.0 + nrm(ks[11], (DEPTH, GLA_DV), 0.02),
        'fft_w': nrm(ks[12], (DEPTH, FFT_GROUPS, FFT_DG, FFT_DG), FFT_DG ** -0.5),
        'conv_w': nrm(ks[13], (DEPTH, CONV_WIDTH, W_CONV), CONV_WIDTH ** -0.5),
        'conv_b': nrm(ks[14], (DEPTH, W_CONV), 0.02),
        'pool_w': nrm(ks[15], (DEPTH, POOL_GROUPS, POOL_DG, POOL_DG), POOL_DG ** -0.5),
        'pool_scale': 1.0 + nrm(ks[16], (DEPTH, W_POOL), 0.1),
        'w_out': nrm(ks[17], (DEPTH, D_MIX, D), D_MIX ** -0.5),
        'ffn_w_up': nrm(ks[18], (DEPTH, D, 2 * D_FF), D ** -0.5),
        'ffn_conv_w': nrm(ks[19], (DEPTH, CONV_WIDTH, D_FF), CONV_WIDTH ** -0.5),
        'ffn_conv_b': nrm(ks[20], (DEPTH, D_FF), 0.02),
        'ffn_w_down': nrm(ks[21], (DEPTH, D_FF, D), D_FF ** -0.5),
        'final_norm_g': 1.0 + nrm(ks[22], (D,), 0.02),
    }


def reference(x, c, ctx, c_ctx, norm1_g, norm2_g, w_mod, b_mod, w_in, gla_w_a2, gla_b_a2, gla_norm_g,
              fft_w, conv_w, conv_b, pool_w, pool_scale, w_out, ffn_w_up, ffn_conv_w, ffn_conv_b,
              ffn_w_down, final_norm_g):
    D = D_MODEL
    s_lat = jax.nn.silu(c)
    s_ctx = jax.nn.silu(c_ctx)
    xc = ctx
    for i in range(DEPTH):
        last = i == DEPTH - 1
        if last:
            mod_c = s_ctx @ w_mod[i][:, :2 * D] + b_mod[i][:2 * D]
            csh1, csc1 = jnp.split(mod_c, 2)
            hc = rmsnorm(xc, norm1_g[i]) * (1 + csc1) + csh1
            pc = hc @ w_in[i][:, :KVA_COLS]
            pk, pv, paf, pab = jnp.split(pc, SPLITS[:3], axis=-1)
            k_c, v_c, la_f_c, la_b_c = gla_kv_decay(pk, pv, paf, pab, gla_w_a2[i], gla_b_a2[i])
            h_f = gla_final_state(k_c, v_c, la_f_c)
            h_b = gla_final_state(jnp.flip(k_c, 1), jnp.flip(v_c, 1), jnp.flip(la_b_c, 1))
        else:
            mod_c = s_ctx @ w_mod[i] + b_mod[i]
            csh1, csc1, cg1, csh2, csc2, cg2 = jnp.split(mod_c, 6)
            hc = rmsnorm(xc, norm1_g[i]) * (1 + csc1) + csh1
            zeros = jnp.zeros((xc.shape[0], GLA_HEADS, GLA_DK, GLA_DV), jnp.float32)
            yc, (h_f, h_b) = mixers(hc @ w_in[i], False, zeros, zeros, gla_w_a2[i], gla_b_a2[i],
                                    gla_norm_g[i], fft_w[i], conv_w[i], conv_b[i], pool_w[i], pool_scale[i])
            xc = xc + cg1 * (yc @ w_out[i])
            hc2 = rmsnorm(xc, norm2_g[i]) * (1 + csc2) + csh2
            xc = xc + cg2 * conv_ffn(hc2, False, ffn_w_up[i], ffn_conv_w[i], ffn_conv_b[i], ffn_w_down[i])
        mod = s_lat @ w_mod[i] + b_mod[i]
        sh1, sc1, g1, sh2, sc2, g2 = [m[:, None, :] for m in jnp.split(mod, 6, axis=-1)]
        hx = rmsnorm(x, norm1_g[i]) * (1 + sc1) + sh1
        yx, _ = mixers(hx @ w_in[i], True, h_f, h_b, gla_w_a2[i], gla_b_a2[i], gla_norm_g[i],
                       fft_w[i], conv_w[i], conv_b[i], pool_w[i], pool_scale[i])
        x = x + g1 * (yx @ w_out[i])
        hx2 = rmsnorm(x, norm2_g[i]) * (1 + sc2) + sh2
        x = x + g2 * conv_ffn(hx2, True, ffn_w_up[i], ffn_conv_w[i], ffn_conv_b[i], ffn_w_down[i])
    return rmsnorm(x, final_norm_g)
```

```python
import functools

import numpy as np
import jax
import jax.numpy as jnp
from jax import lax
from jax.experimental import pallas as pl
from jax.experimental.pallas import tpu as pltpu

F32 = jnp.float32
BF16 = jnp.bfloat16

GRID_W = 64
EPS = 1e-6
GLA_HEADS = 4
GLA_DK = 32
GLA_DV = 64
GLA_QK = GLA_HEADS * GLA_DK
W_GLA = GLA_HEADS * GLA_DV
GLA_RANK = 16
GLA_TAU = 16.0
FFT_GROUPS = 4
FFT_DG = 64
W_FFT = FFT_GROUPS * FFT_DG
W_CONV = 256
POOL_WINDOWS = (2, 4, 8, 16)
POOL_DG = 64
W_POOL = len(POOL_WINDOWS) * POOL_DG
COL_SIZES = (GLA_QK, W_GLA, GLA_RANK, GLA_RANK, GLA_QK, W_GLA, W_FFT, W_CONV, W_CONV, W_CONV, W_POOL)
COL_STARTS = tuple(int(s) for s in np.cumsum((0,) + COL_SIZES)[:-1])

LANES = 128
VMEM_LIMIT_BYTES = 56 * 1024 * 1024

TOKEN_TILE = 256
GLA_CHUNK = 64
GLA_SUB = 16
GLA_BLOCK = 256
FFN_CHUNK = 256
FFT_N2 = 128
NEG_BIG = -1e30

IN_K, IN_Q, IN_V, IN_G, IN_FFT, IN_H, IN_BG, IN_CG, IN_POOL, IN_A = (
    0, 128, 256, 512, 768, 1024, 1280, 1536, 1792, 2048)
IN_COLS = IN_A + LANES


def _const_spec(shape):
    nd = len(shape)
    return pl.BlockSpec(shape, lambda *_: (0,) * nd, pipeline_mode=pl.Buffered(1))


def _params(*sem):
    return pltpu.CompilerParams(dimension_semantics=sem, vmem_limit_bytes=VMEM_LIMIT_BYTES)


def _rms_mod(x, g, scale, shift):
    ms = jnp.mean(x * x, axis=-1, keepdims=True)
    return (x * lax.rsqrt(ms + EPS)) * g * (1.0 + scale) + shift


def _silu(a):
    return a * jax.nn.sigmoid(a)


def _mod_kernel(c_ref, w_ref, b_ref, o_ref):
    s = _silu(c_ref[...]).astype(BF16)
    o_ref[0] = jnp.dot(s, w_ref[0].astype(BF16), preferred_element_type=F32) + b_ref[0]


def _mod_call(cc, w_mod, b_mod):
    depth, d, n = w_mod.shape
    tn = 1536
    return pl.pallas_call(
        _mod_kernel,
        grid=(depth, n // tn),
        in_specs=[pl.BlockSpec(cc.shape, lambda i, j: (0, 0)),
                  pl.BlockSpec((1, d, tn), lambda i, j: (i, 0, j)),
                  pl.BlockSpec((1, 1, tn), lambda i, j: (i, 0, j))],
        out_specs=pl.BlockSpec((1, cc.shape[0], tn), lambda i, j: (i, 0, j)),
        out_shape=jax.ShapeDtypeStruct((depth, cc.shape[0], n), F32),
        compiler_params=_params("arbitrary", "arbitrary"),
        name="modulation",
    )(cc, w_mod, b_mod.reshape(depth, 1, n))


def _in_kernel(x_ref, mod_ref, g_ref, w_ref, wa2_ref, ba2_ref, cw_ref, cb_ref, cnt_ref, wpool_ref, pscale_ref,
               kq_ref, v_ref, la_ref, gate_ref, fft_ref, ycp_ref, *, row_len):
    t_rows = x_ref.shape[0]
    h = _rms_mod(x_ref[...], g_ref[...], mod_ref[0, 1:2, :], mod_ref[0, 0:1, :]).astype(BF16)
    p = jnp.dot(h, w_ref[...], preferred_element_type=F32)
    kq_ref[:, :GLA_QK] = p[:, IN_K:IN_K + GLA_QK]
    kq_ref[:, GLA_QK:] = p[:, IN_Q:IN_Q + GLA_QK] * (GLA_DK ** -0.5)
    v_ref[...] = p[:, IN_V:IN_V + W_GLA]
    gate_ref[...] = p[:, IN_G:IN_G + W_GLA]
    fft_ref[...] = p[:, IN_FFT:IN_FFT + W_FFT].astype(BF16)
    z = jnp.dot(p[:, IN_A:IN_A + LANES].astype(BF16), wa2_ref[...], preferred_element_type=F32) + ba2_ref[...]
    la_ref[...] = (jnp.minimum(z, 0.0) - jnp.log1p(jnp.exp(-jnp.abs(z)))) * (1.0 / GLA_TAU)

    pos = lax.broadcasted_iota(jnp.int32, (t_rows, W_CONV), 0) & (row_len - 1)

    def prev(a, s):
        return jnp.where(pos >= s, pltpu.roll(a, s, 0), 0.0)

    def nxt(a, s):
        return jnp.where(pos < row_len - s, pltpu.roll(a, t_rows - s, 0), 0.0)

    t = p[:, IN_CG:IN_CG + W_CONV] * p[:, IN_H:IN_H + W_CONV]
    cw = cw_ref[...]
    conv = prev(t, 1) * cw[0:1] + t * cw[1:2] + nxt(t, 1) * cw[2:3] + cb_ref[...]
    ycp_ref[:, :W_CONV] = (p[:, IN_BG:IN_BG + W_CONV] * conv).astype(BF16)

    u = p[:, IN_POOL:IN_POOL + W_POOL]
    f1 = u
    f2 = f1 + nxt(f1, 1)
    f4 = f2 + nxt(f2, 2)
    f8 = f4 + nxt(f4, 4)
    g1 = prev(u, 1)
    g2 = g1 + prev(g1, 1)
    g4 = g2 + prev(g2, 2)
    g8 = g4 + prev(g4, 4)
    lane = lax.broadcasted_iota(jnp.int32, (t_rows, W_POOL), 1)
    tot = jnp.where(lane < POOL_DG, g1 + f1,
                    jnp.where(lane < 2 * POOL_DG, g2 + f2,
                              jnp.where(lane < 3 * POOL_DG, g4 + f4, g8 + f8)))
    pooled = tot / cnt_ref[...] - u
    yp = jnp.dot(pooled.astype(BF16), wpool_ref[...], preferred_element_type=F32) * pscale_ref[...]
    ycp_ref[:, W_CONV:] = yp.astype(BF16)


def _pool_counts(t_rows, row_len):
    pos = np.arange(t_rows) % row_len
    cols = []
    for w in POOL_WINDOWS:
        lo = np.clip(pos - w // 2, 0, row_len - 1)
        hi = np.clip(pos + w // 2 - 1, 0, row_len - 1)
        cols.append(np.repeat((hi - lo + 1).astype(np.float32)[:, None], POOL_DG, axis=1))
    return np.concatenate(cols, axis=1)


def _in_call(xt, mod, g, w, wa2, ba2, cw, cb, wpool, pscale, *, row_len, tiles_per_batch):
    n, d = xt.shape
    t = TOKEN_TILE
    cnt = jnp.asarray(_pool_counts(t, row_len))
    row = lambda i: (i, 0)
    outs = [jax.ShapeDtypeStruct((n, 2 * GLA_QK), F32), jax.ShapeDtypeStruct((n, W_GLA), F32),
            jax.ShapeDtypeStruct((n, 2 * GLA_QK), F32), jax.ShapeDtypeStruct((n, W_GLA), F32),
            jax.ShapeDtypeStruct((n, W_FFT), BF16), jax.ShapeDtypeStruct((n, W_CONV + W_POOL), BF16)]
    return pl.pallas_call(
        functools.partial(_in_kernel, row_len=row_len),
        grid=(n // t,),
        in_specs=[pl.BlockSpec((t, d), row),
                  pl.BlockSpec((1, 6, d), lambda i: (i // tiles_per_batch, 0, 0)),
                  _const_spec(g.shape), _const_spec(w.shape), _const_spec(wa2.shape), _const_spec(ba2.shape),
                  _const_spec(cw.shape), _const_spec(cb.shape), _const_spec(cnt.shape),
                  _const_spec(wpool.shape), _const_spec(pscale.shape)],
        out_specs=[pl.BlockSpec((t, o.shape[1]), row) for o in outs],
        out_shape=outs,
        compiler_params=_params("arbitrary"),
        name="in_proj",
    )(xt, mod, g, w, wa2, ba2, cw, cb, cnt, wpool, pscale)


def _gla_chunk(q, k, v, la, s_ref, o_ref, r0, consts, fwd):
    tri, rexp, bdmask, hq, hv = consts
    c, sub = GLA_CHUNK, GLA_SUB
    nt = (((1,), (1,)), ((), ()))
    tn = (((0,), (0,)), ((), ()))
    la_hi = la.astype(BF16)
    la_lo = (la - la_hi.astype(F32)).astype(BF16)
    cum = jnp.dot(tri, la_hi, preferred_element_type=F32) + jnp.dot(tri, la_lo, preferred_element_type=F32)
    end = cum[c - 1:c] if fwd else cum[0:1]
    s_prev = s_ref[0]
    qe = (q * jnp.exp(cum)).astype(BF16)
    o_all = lax.dot_general(qe, s_prev.astype(BF16), nt, preferred_element_type=F32)
    kd = (k * jnp.exp(end - cum)).astype(BF16)
    upd = lax.dot_general(v.astype(BF16), kd, tn, preferred_element_type=F32)
    s_ref[0] = s_prev * jnp.exp(end) + upd * bdmask
    o_blk = [o_all[sub * r:sub * (r + 1)] for r in range(c // sub)]

    b = c // 2
    while b >= sub:
        npair = c // (2 * b)
        qs, ks, vk = [], [], []
        for p in range(npair):
            f0, s0 = 2 * p * b, 2 * p * b + b
            if fwd:
                ref, qsl, ksl = cum[s0 - 1:s0], slice(s0, s0 + b), slice(f0, f0 + b)
            else:
                ref, qsl, ksl = cum[s0:s0 + 1], slice(f0, f0 + b), slice(s0, s0 + b)
            qs.append(q[qsl] * jnp.exp(cum[qsl] - ref))
            ks.append(k[ksl] * jnp.exp(ref - cum[ksl]))
            vk.append(v[ksl])
        qs = jnp.concatenate(qs, axis=0) if npair > 1 else qs[0]
        ks = jnp.concatenate(ks, axis=0) if npair > 1 else ks[0]
        vk = jnp.concatenate(vk, axis=0) if npair > 1 else vk[0]
        half = c // 2
        qst = jnp.concatenate([qs * hq[h] for h in range(GLA_HEADS)], axis=0).astype(BF16)
        att = lax.dot_general(qst, ks.astype(BF16), nt, preferred_element_type=F32)
        if npair > 1:
            ri = lax.broadcasted_iota(jnp.int32, att.shape, 0)
            ci = lax.broadcasted_iota(jnp.int32, att.shape, 1)
            att = jnp.where(((ri & (half - 1)) ^ ci) < b, att, 0.0)
        res4 = jnp.dot(att.astype(BF16), vk.astype(BF16), preferred_element_type=F32)
        res = res4[0:half] * hv[0]
        for h in range(1, GLA_HEADS):
            res = res + res4[h * half:(h + 1) * half] * hv[h]
        for p in range(npair):
            q0 = (2 * p * b + b) if fwd else 2 * p * b
            for t in range(b // sub):
                o_blk[q0 // sub + t] = o_blk[q0 // sub + t] + res[p * b + sub * t:p * b + sub * (t + 1)]
        b //= 2

    ii = lax.broadcasted_iota(jnp.int32, (sub, GLA_QK), 0)
    for r in range(c // sub):
        sl = slice(sub * r, sub * (r + 1))
        cl, qb, kb, vb = cum[sl], q[sl], k[sl], v[sl]
        es = []
        for j in range(sub):
            keep = (ii >= j) if fwd else (ii <= j)
            w = jnp.exp(jnp.where(keep, cl - cl[j:j + 1], NEG_BIG))
            es.append((w * (qb * kb[j:j + 1])).astype(BF16))
        pr = jnp.dot(jnp.concatenate(es, axis=0), rexp, preferred_element_type=F32)
        od = pr[0:sub] * vb[0:1]
        for j in range(1, sub):
            od = od + pr[sub * j:sub * (j + 1)] * vb[j:j + 1]
        o_ref[pl.ds(r0 + sub * r, sub), :] = o_blk[r] + od


def _gla_kernel(kqf_ref, vf_ref, laf_ref, kqb_ref, vb_ref, lab_ref, h0f_ref, h0b_ref,
                trif_ref, trib_ref, rexp_ref, bd_ref,
                of_ref, ob_ref, sf_ref, sb_ref):
    @pl.when(pl.program_id(1) == 0)
    def _():
        sf_ref[...] = h0f_ref[...]
        sb_ref[...] = h0b_ref[...]

    lq = lax.broadcasted_iota(jnp.int32, (1, GLA_QK), 1)
    lv = lax.broadcasted_iota(jnp.int32, (1, W_GLA), 1)
    hq = [jnp.where((lq >= h * GLA_DK) & (lq < (h + 1) * GLA_DK), 1.0, 0.0) for h in range(GLA_HEADS)]
    hv = [jnp.where((lv >= h * GLA_DV) & (lv < (h + 1) * GLA_DV), 1.0, 0.0) for h in range(GLA_HEADS)]
    rexp, bdmask = rexp_ref[...], bd_ref[...]
    nchunk = kqf_ref.shape[0] // GLA_CHUNK

    def body(ci, carry):
        r0 = pl.multiple_of(ci * GLA_CHUNK, GLA_CHUNK)
        rows = pl.ds(r0, GLA_CHUNK)
        _gla_chunk(kqf_ref[rows, GLA_QK:], kqf_ref[rows, :GLA_QK], vf_ref[rows, :], laf_ref[rows, :],
                   sf_ref, of_ref, r0, (trif_ref[...], rexp, bdmask, hq, hv), True)
        r1 = pl.multiple_of((nchunk - 1 - ci) * GLA_CHUNK, GLA_CHUNK)
        rows = pl.ds(r1, GLA_CHUNK)
        _gla_chunk(kqb_ref[rows, GLA_QK:], kqb_ref[rows, :GLA_QK], vb_ref[rows, :], lab_ref[rows, :],
                   sb_ref, ob_ref, r1, (trib_ref[...], rexp, bdmask, hq, hv), False)
        return carry

    lax.fori_loop(0, nchunk, body, 0)


def _gla_consts():
    c = GLA_CHUNK
    i = np.arange(c)
    trif = (i[None, :] <= i[:, None]).astype(np.float32)
    trib = (i[None, :] >= i[:, None]).astype(np.float32)
    hk = np.arange(GLA_QK) // GLA_DK
    hd = np.arange(W_GLA) // GLA_DV
    rexp = (hk[:, None] == hd[None, :]).astype(np.float32)
    bd = (hd[:, None] == hk[None, :]).astype(np.float32)
    return (jnp.asarray(trif, BF16), jnp.asarray(trib, BF16), jnp.asarray(rexp, BF16), jnp.asarray(bd, F32))


def _gla_call(kq, v, la, h0f, h0b, *, batch):
    n = kq.shape[0]
    tg = GLA_BLOCK
    nb = n // batch // tg
    trif, trib, rexp, bd = _gla_consts()
    fw = lambda b, i: (b * nb + i, 0)
    bw = lambda b, i: (b * nb + nb - 1 - i, 0)
    bw1 = lambda b, i: (b * nb + nb - 1 - i, 1)
    st = lambda b, i: (b, 0, 0)
    sshape = jax.ShapeDtypeStruct((batch, W_GLA, GLA_QK), F32)
    return pl.pallas_call(
        _gla_kernel,
        grid=(batch, nb),
        in_specs=[pl.BlockSpec((tg, 2 * GLA_QK), fw), pl.BlockSpec((tg, W_GLA), fw), pl.BlockSpec((tg, GLA_QK), fw),
                  pl.BlockSpec((tg, 2 * GLA_QK), bw), pl.BlockSpec((tg, W_GLA), bw), pl.BlockSpec((tg, GLA_QK), bw1),
                  pl.BlockSpec((1, W_GLA, GLA_QK), st), pl.BlockSpec((1, W_GLA, GLA_QK), st),
                  _const_spec(trif.shape), _const_spec(trib.shape), _const_spec(rexp.shape), _const_spec(bd.shape)],
        out_specs=[pl.BlockSpec((tg, W_GLA), fw), pl.BlockSpec((tg, W_GLA), bw),
                   pl.BlockSpec((1, W_GLA, GLA_QK), st), pl.BlockSpec((1, W_GLA, GLA_QK), st)],
        out_shape=[jax.ShapeDtypeStruct((n, W_GLA), F32), jax.ShapeDtypeStruct((n, W_GLA), F32), sshape, sshape],
        compiler_params=_params("arbitrary", "arbitrary"),
        name="gla",
    )(kq, v, la, kq, v, la, h0f, h0b, trif, trib, rexp, bd)


def _fft_tail(z, half, cs_ref, wf_ref, norm):
    ab = jnp.concatenate([z[:half], z[half:]], axis=1).astype(BF16)
    f = jnp.dot(ab, cs_ref[...].astype(BF16), preferred_element_type=F32) * norm
    return jnp.dot(f.astype(BF16), wf_ref[...], preferred_element_type=F32).astype(BF16)


def _fft1_kernel(w_ref, u_ref, y_ref):
    y_ref[0] = jnp.dot(w_ref[...].astype(BF16), u_ref[0], preferred_element_type=F32).astype(BF16)


def _fft2_kernel(m_ref, y_ref, cs_ref, wf_ref, o_ref, *, group, norm):
    for g in range(group):
        ys = jnp.concatenate([y_ref[0, 0, g], y_ref[0, 1, g]], axis=0)
        z = jnp.dot(m_ref[g].astype(BF16), ys, preferred_element_type=F32)
        o_ref[0, :, g * W_FFT:(g + 1) * W_FFT] = _fft_tail(z, FFT_N2, cs_ref, wf_ref, norm)


def _fft_direct_kernel(m_ref, u_ref, cs_ref, wf_ref, o_ref, *, norm):
    z = jnp.dot(m_ref[...].astype(BF16), u_ref[...], preferred_element_type=F32)
    o_ref[...] = _fft_tail(z, u_ref.shape[0], cs_ref, wf_ref, norm)


def _dft_cos_sin(n):
    k = np.arange(n, dtype=np.int64)
    ang = 2.0 * np.pi * ((k[:, None] * k[None, :]) % n).astype(np.float64) / n
    return np.cos(ang), np.sin(ang)


def _channel_dft():
    c, s = _dft_cos_sin(FFT_DG)
    eye = np.eye(FFT_GROUPS)
    return jnp.asarray(np.concatenate([np.kron(eye, c), np.kron(eye, s)], axis=0), F32)


def _fft_latent_call(u, wf_bd, *, batch):
    n = u.shape[0] // batch
    n1, n2 = n // FFT_N2, FFT_N2
    c1, s1 = _dft_cos_sin(n1)
    w1 = jnp.asarray(np.concatenate([c1, -s1], axis=0), F32)
    k1 = np.arange(n1, dtype=np.int64)[:, None, None]
    k2 = np.arange(n2, dtype=np.int64)[None, :, None]
    m2 = np.arange(n2, dtype=np.int64)[None, None, :]
    ang = 2.0 * np.pi * ((m2 * (k1 + n1 * k2)) % n).astype(np.float64) / n
    cm, sm = np.cos(ang), np.sin(ang)
    mats = jnp.asarray(np.concatenate([np.concatenate([cm, sm], axis=2),
                                       np.concatenate([-sm, cm], axis=2)], axis=1), F32)
    cs = _channel_dft()
    norm = float(1.0 / np.sqrt(n * FFT_DG))
    width = n2 * W_FFT
    ch = min(width, 8192)
    y = pl.pallas_call(
        _fft1_kernel,
        grid=(batch, width // ch),
        in_specs=[_const_spec(w1.shape), pl.BlockSpec((1, n1, ch), lambda b, j: (b, 0, j))],
        out_specs=pl.BlockSpec((1, 2 * n1, ch), lambda b, j: (b, 0, j)),
        out_shape=jax.ShapeDtypeStruct((batch, 2 * n1, width), BF16),
        compiler_params=_params("arbitrary", "arbitrary"),
        name="fft_stage1",
    )(w1, u.reshape(batch, n1, width))
    group = min(n1, 8)
    out = pl.pallas_call(
        functools.partial(_fft2_kernel, group=group, norm=norm),
        grid=(n1 // group, batch),
        in_specs=[pl.BlockSpec((group, 2 * n2, 2 * n2), lambda j, b: (j, 0, 0)),
                  pl.BlockSpec((1, 2, group, n2, W_FFT), lambda j, b: (b, 0, j, 0, 0)),
                  _const_spec(cs.shape), _const_spec(wf_bd.shape)],
        out_specs=pl.BlockSpec((1, n2, group * W_FFT), lambda j, b: (b, 0, j)),
        out_shape=jax.ShapeDtypeStruct((batch, n2, n1 * W_FFT), BF16),
        compiler_params=_params("arbitrary", "arbitrary"),
        name="fft_stage2",
    )(mats, y.reshape(batch, 2, n1, n2, W_FFT), cs, wf_bd)
    return out.reshape(batch * n, W_FFT)


def _fft_direct_call(u, wf_bd, *, batch):
    n = u.shape[0] // batch
    c, s = _dft_cos_sin(n)
    m = jnp.asarray(np.concatenate([c, -s], axis=0), F32)
    cs = _channel_dft()
    norm = float(1.0 / np.sqrt(n * FFT_DG))
    return pl.pallas_call(
        functools.partial(_fft_direct_kernel, norm=norm),
        grid=(batch,),
        in_specs=[_const_spec(m.shape), pl.BlockSpec((n, W_FFT), lambda b: (b, 0)),
                  _const_spec(cs.shape), _const_spec(wf_bd.shape)],
        out_specs=pl.BlockSpec((n, W_FFT), lambda b: (b, 0)),
        out_shape=jax.ShapeDtypeStruct((batch * n, W_FFT), BF16),
        compiler_params=_params("arbitrary"),
        name="fft_direct",
    )(m, u, cs, wf_bd)


def _out_kernel(x_ref, mod_ref, of_ref, ob_ref, gate_ref, fft_ref, ycp_ref, glag_ref, ones_ref, wout_ref,
                n2g_ref, wup_ref, cw_ref, cb_ref, wdn_ref, fing_ref, o_ref, *, row_len, final):
    t_rows = x_ref.shape[0]
    d_ff = wdn_ref.shape[0]
    o = of_ref[...] + ob_ref[...]
    osq = o * o
    hi = osq.astype(BF16)
    lo = (osq - hi.astype(F32)).astype(BF16)
    ms = (jnp.dot(hi, ones_ref[...], preferred_element_type=F32)
          + jnp.dot(lo, ones_ref[...], preferred_element_type=F32)) * (1.0 / GLA_DV)
    gl = (o * lax.rsqrt(ms + EPS)) * glag_ref[...] * _silu(gate_ref[...])
    ymix = jnp.concatenate([gl.astype(BF16), fft_ref[...], ycp_ref[...]], axis=1)
    x1 = x_ref[...] + mod_ref[0, 2:3, :] * jnp.dot(ymix, wout_ref[...], preferred_element_type=F32)

    h2 = _rms_mod(x1, n2g_ref[...], mod_ref[0, 4:5, :], mod_ref[0, 3:4, :]).astype(BF16)
    pos = lax.broadcasted_iota(jnp.int32, (t_rows, FFN_CHUNK), 0) & (row_len - 1)
    first, last = pos == 0, pos == row_len - 1
    acc = jnp.zeros(x1.shape, F32)
    for c0 in range(0, d_ff, FFN_CHUNK):
        cols = slice(c0, c0 + FFN_CHUNK)
        a = jnp.dot(h2, wup_ref[:, cols], preferred_element_type=F32)
        u = jnp.dot(h2, wup_ref[:, d_ff + c0:d_ff + c0 + FFN_CHUNK], preferred_element_type=F32)
        cw = cw_ref[:, cols]
        a = (jnp.where(first, 0.0, pltpu.roll(a, 1, 0)) * cw[0:1] + a * cw[1:2]
             + jnp.where(last, 0.0, pltpu.roll(a, t_rows - 1, 0)) * cw[2:3] + cb_ref[:, cols])
        acc = acc + jnp.dot((_silu(a) * u).astype(BF16), wdn_ref[cols, :], preferred_element_type=F32)
    x2 = x1 + mod_ref[0, 5:6, :] * acc
    if final:
        ms2 = jnp.mean(x2 * x2, axis=-1, keepdims=True)
        x2 = (x2 * lax.rsqrt(ms2 + EPS)) * fing_ref[...]
    o_ref[...] = x2


def _out_call(xt, mod, o_f, o_b, gate, yfft, ycp, glag, wout, n2g, wup, cw, cb, wdn, fing,
              *, row_len, tiles_per_batch, final):
    n, d = xt.shape
    t = TOKEN_TILE
    hd = np.arange(W_GLA) // GLA_DV
    ones = jnp.asarray((hd[:, None] == hd[None, :]).astype(np.float32), BF16)
    row = lambda i: (i, 0)
    acts = [xt, o_f, o_b, gate, yfft, ycp]
    consts = [glag, ones, wout, n2g, wup, cw, cb, wdn, fing]
    return pl.pallas_call(
        functools.partial(_out_kernel, row_len=row_len, final=final),
        grid=(n // t,),
        in_specs=[pl.BlockSpec((t, d), row), pl.BlockSpec((1, 6, d), lambda i: (i // tiles_per_batch, 0, 0))]
                 + [pl.BlockSpec((t, a.shape[1]), row) for a in acts[1:]]
                 + [_const_spec(a.shape) for a in consts],
        out_specs=pl.BlockSpec((t, d), row),
        out_shape=jax.ShapeDtypeStruct((n, d), F32),
        compiler_params=_params("arbitrary"),
        name="out_ffn",
    )(xt, mod, *acts[1:], *consts)


def _block_diag(w):
    g, a, b = w.shape
    eye = jnp.eye(g, dtype=w.dtype)
    return (eye[:, None, :, None] * w[:, :, None, :]).reshape(g * a, g * b)


def _reorder_w_in(w):
    col = {name: w[:, s:s + z] for name, s, z in zip(
        ("k", "v", "af", "ab", "q", "g", "fft", "h", "bg", "cg", "pool"), COL_STARTS, COL_SIZES)}
    pad = jnp.zeros((w.shape[0], LANES - 2 * GLA_RANK), w.dtype)
    order = ("k", "q", "v", "g", "fft", "h", "bg", "cg", "pool", "af", "ab")
    return jnp.concatenate([col[o] for o in order] + [pad], axis=1).astype(BF16)


def _decay_weights(w_a2, b_a2):
    wa2 = jnp.zeros((LANES, 2 * GLA_QK), F32)
    wa2 = wa2.at[0:GLA_RANK, 0:GLA_QK].set(w_a2[0])
    wa2 = wa2.at[GLA_RANK:2 * GLA_RANK, GLA_QK:].set(w_a2[1])
    return wa2.astype(BF16), b_a2.reshape(1, 2 * GLA_QK)


def kernel(x, c, ctx, c_ctx, norm1_g, norm2_g, w_mod, b_mod, w_in, gla_w_a2, gla_b_a2, gla_norm_g,
           fft_w, conv_w, conv_b, pool_w, pool_scale, w_out, ffn_w_up, ffn_conv_w, ffn_conv_b,
           ffn_w_down, final_norm_g):
    batch, seq, d = x.shape
    ctx_len = ctx.shape[1]
    depth = w_mod.shape[0]
    assert seq % GLA_BLOCK == 0 and seq % FFT_N2 == 0 and ctx_len == TOKEN_TILE == GLA_BLOCK

    rows = 8
    cc = jnp.concatenate([c, c_ctx[None, :], jnp.zeros((rows - batch - 1, d), F32)], axis=0)
    mod = _mod_call(cc, w_mod, b_mod)

    xt = x.reshape(batch * seq, d)
    xc = ctx.reshape(batch * ctx_len, d)
    zero_state = jnp.zeros((batch, W_GLA, GLA_QK), F32)
    fing = final_norm_g.reshape(1, d)
    for i in range(depth):
        last = i == depth - 1
        mod_lat = mod[i, :batch].reshape(batch, 6, d)
        mod_ctx = jnp.broadcast_to(mod[i, batch].reshape(1, 6, d), (batch, 6, d))
        w_r = _reorder_w_in(w_in[i])
        wa2, ba2 = _decay_weights(gla_w_a2[i], gla_b_a2[i])
        wpool = _block_diag(pool_w[i]).astype(BF16)
        wf_bd = _block_diag(fft_w[i]).astype(BF16)
        in_args = (norm1_g[i].reshape(1, d), w_r, wa2, ba2, conv_w[i], conv_b[i].reshape(1, -1), wpool,
                   pool_scale[i].reshape(1, -1))
        out_args = (jnp.tile(gla_norm_g[i], GLA_HEADS).reshape(1, W_GLA), w_out[i].astype(BF16),
                    norm2_g[i].reshape(1, d), ffn_w_up[i].astype(BF16), ffn_conv_w[i],
                    ffn_conv_b[i].reshape(1, -1), ffn_w_down[i].astype(BF16), fing)

        kq, v, la, gate, ufft, ycp = _in_call(xc, mod_ctx, *in_args, row_len=ctx_len, tiles_per_batch=1)
        o_f, o_b, s_f, s_b = _gla_call(kq, v, la, zero_state, zero_state, batch=batch)
        if not last:
            yfft = _fft_direct_call(ufft, wf_bd, batch=batch)
            xc = _out_call(xc, mod_ctx, o_f, o_b, gate, yfft, ycp, *out_args,
                           row_len=ctx_len, tiles_per_batch=1, final=False)

        kq, v, la, gate, ufft, ycp = _in_call(xt, mod_lat, *in_args, row_len=GRID_W,
                                              tiles_per_batch=seq // TOKEN_TILE)
        o_f, o_b, _, _ = _gla_call(kq, v, la, s_f, s_b, batch=batch)
        yfft = _fft_latent_call(ufft, wf_bd, batch=batch)
        xt = _out_call(xt, mod_lat, o_f, o_b, gate, yfft, ycp, *out_args,
                       row_len=GRID_W, tiles_per_batch=seq // TOKEN_TILE, final=last)
    return xt.reshape(batch, seq, d)
```

```python
import functools

import numpy as np
import jax
import jax.numpy as jnp
from jax import lax
from jax.experimental import pallas as pl
from jax.experimental.pallas import tpu as pltpu

F32 = jnp.float32
BF16 = jnp.bfloat16

GRID_W = 64
EPS = 1e-6
GLA_HEADS = 4
GLA_DK = 32
GLA_DV = 64
GLA_QK = GLA_HEADS * GLA_DK
W_GLA = GLA_HEADS * GLA_DV
GLA_RANK = 16
GLA_TAU = 16.0
FFT_GROUPS = 4
FFT_DG = 64
W_FFT = FFT_GROUPS * FFT_DG
W_CONV = 256
POOL_WINDOWS = (2, 4, 8, 16)
POOL_DG = 64
W_POOL = len(POOL_WINDOWS) * POOL_DG
COL_SIZES = (GLA_QK, W_GLA, GLA_RANK, GLA_RANK, GLA_QK, W_GLA, W_FFT, W_CONV, W_CONV, W_CONV, W_POOL)
COL_STARTS = tuple(int(s) for s in np.cumsum((0,) + COL_SIZES)[:-1])

LANES = 128
VMEM_LIMIT_BYTES = 56 * 1024 * 1024

TOKEN_TILE = 256
OUT_TILE = 512
GLA_CHUNK = 64
GLA_SUB = 16
GLA_BLOCK = 256
FFN_CHUNK = 256
FFT_N2 = 128
NEG_BIG = -1e30

IN_K, IN_Q, IN_V, IN_G, IN_FFT, IN_H, IN_BG, IN_CG, IN_POOL, IN_A = (
    0, 128, 256, 512, 768, 1024, 1280, 1536, 1792, 2048)
IN_COLS = IN_A + LANES


def _const_spec(shape):
    nd = len(shape)
    return pl.BlockSpec(shape, lambda *_: (0,) * nd, pipeline_mode=pl.Buffered(1))


def _params(*sem):
    return pltpu.CompilerParams(dimension_semantics=sem, vmem_limit_bytes=VMEM_LIMIT_BYTES)


def _rms_mod(x, g, scale, shift):
    ms = jnp.mean(x * x, axis=-1, keepdims=True)
    return (x * lax.rsqrt(ms + EPS)) * g * (1.0 + scale) + shift


def _silu(a):
    return a * jax.nn.sigmoid(a)


def _mod_kernel(c_ref, w_ref, b_ref, o_ref):
    s = _silu(c_ref[...]).astype(BF16)
    o_ref[0] = jnp.dot(s, w_ref[0].astype(BF16), preferred_element_type=F32) + b_ref[0]


def _mod_call(cc, w_mod, b_mod):
    depth, d, n = w_mod.shape
    tn = 1536
    return pl.pallas_call(
        _mod_kernel,
        grid=(depth, n // tn),
        in_specs=[pl.BlockSpec(cc.shape, lambda i, j: (0, 0)),
                  pl.BlockSpec((1, d, tn), lambda i, j: (i, 0, j)),
                  pl.BlockSpec((1, 1, tn), lambda i, j: (i, 0, j))],
        out_specs=pl.BlockSpec((1, cc.shape[0], tn), lambda i, j: (i, 0, j)),
        out_shape=jax.ShapeDtypeStruct((depth, cc.shape[0], n), F32),
        compiler_params=_params("arbitrary", "arbitrary"),
        name="modulation",
    )(cc, w_mod, b_mod.reshape(depth, 1, n))


def _in_kernel(x_ref, mod_ref, g_ref, w_ref, wa2_ref, ba2_ref, cw_ref, cb_ref, cnt_ref, wpool_ref, pscale_ref,
               kq_ref, v_ref, la_ref, gate_ref, fft_ref, ycp_ref, *, row_len):
    t_rows = x_ref.shape[0]
    h = _rms_mod(x_ref[...], g_ref[...], mod_ref[0, 1:2, :], mod_ref[0, 0:1, :]).astype(BF16)
    p = jnp.dot(h, w_ref[...], preferred_element_type=F32)
    kq_ref[:, :GLA_QK] = p[:, IN_K:IN_K + GLA_QK]
    kq_ref[:, GLA_QK:] = p[:, IN_Q:IN_Q + GLA_QK] * (GLA_DK ** -0.5)
    v_ref[...] = p[:, IN_V:IN_V + W_GLA]
    gate_ref[...] = p[:, IN_G:IN_G + W_GLA]
    fft_ref[...] = p[:, IN_FFT:IN_FFT + W_FFT].astype(BF16)
    z = jnp.dot(p[:, IN_A:IN_A + LANES].astype(BF16), wa2_ref[...], preferred_element_type=F32) + ba2_ref[...]
    la_ref[...] = (jnp.minimum(z, 0.0) - jnp.log1p(jnp.exp(-jnp.abs(z)))) * (1.0 / GLA_TAU)

    pos = lax.broadcasted_iota(jnp.int32, (t_rows, W_CONV), 0) & (row_len - 1)

    def prev(a, s):
        return jnp.where(pos >= s, pltpu.roll(a, s, 0), 0.0)

    def nxt(a, s):
        return jnp.where(pos < row_len - s, pltpu.roll(a, t_rows - s, 0), 0.0)

    t = p[:, IN_CG:IN_CG + W_CONV] * p[:, IN_H:IN_H + W_CONV]
    cw = cw_ref[...]
    conv = prev(t, 1) * cw[0:1] + t * cw[1:2] + nxt(t, 1) * cw[2:3] + cb_ref[...]
    ycp_ref[:, :W_CONV] = (p[:, IN_BG:IN_BG + W_CONV] * conv).astype(BF16)

    u = p[:, IN_POOL:IN_POOL + W_POOL]
    f1 = u
    f2 = f1 + nxt(f1, 1)
    f4 = f2 + nxt(f2, 2)
    f8 = f4 + nxt(f4, 4)
    g1 = prev(u, 1)
    g2 = g1 + prev(g1, 1)
    g4 = g2 + prev(g2, 2)
    g8 = g4 + prev(g4, 4)
    lane = lax.broadcasted_iota(jnp.int32, (t_rows, W_POOL), 1)
    tot = jnp.where(lane < POOL_DG, g1 + f1,
                    jnp.where(lane < 2 * POOL_DG, g2 + f2,
                              jnp.where(lane < 3 * POOL_DG, g4 + f4, g8 + f8)))
    pooled = tot / cnt_ref[...] - u
    yp = jnp.dot(pooled.astype(BF16), wpool_ref[...], preferred_element_type=F32) * pscale_ref[...]
    ycp_ref[:, W_CONV:] = yp.astype(BF16)


def _pool_counts(t_rows, row_len):
    pos = np.arange(t_rows) % row_len
    cols = []
    for w in POOL_WINDOWS:
        lo = np.clip(pos - w // 2, 0, row_len - 1)
        hi = np.clip(pos + w // 2 - 1, 0, row_len - 1)
        cols.append(np.repeat((hi - lo + 1).astype(np.float32)[:, None], POOL_DG, axis=1))
    return np.concatenate(cols, axis=1)


def _in_call(xt, mod, g, w, wa2, ba2, cw, cb, wpool, pscale, *, row_len):
    n, d = xt.shape
    t = TOKEN_TILE
    tiles_per_group = n // mod.shape[0] // t
    assert tiles_per_group * t * mod.shape[0] == n
    cnt = jnp.asarray(_pool_counts(t, row_len))
    row = lambda i: (i, 0)
    outs = [jax.ShapeDtypeStruct((n, 2 * GLA_QK), F32), jax.ShapeDtypeStruct((n, W_GLA), F32),
            jax.ShapeDtypeStruct((n, 2 * GLA_QK), F32), jax.ShapeDtypeStruct((n, W_GLA), F32),
            jax.ShapeDtypeStruct((n, W_FFT), BF16), jax.ShapeDtypeStruct((n, W_CONV + W_POOL), BF16)]
    return pl.pallas_call(
        functools.partial(_in_kernel, row_len=row_len),
        grid=(n // t,),
        in_specs=[pl.BlockSpec((t, d), row),
                  pl.BlockSpec((1, 6, d), lambda i: (i // tiles_per_group, 0, 0)),
                  _const_spec(g.shape), _const_spec(w.shape), _const_spec(wa2.shape), _const_spec(ba2.shape),
                  _const_spec(cw.shape), _const_spec(cb.shape), _const_spec(cnt.shape),
                  _const_spec(wpool.shape), _const_spec(pscale.shape)],
        out_specs=[pl.BlockSpec((t, o.shape[1]), row) for o in outs],
        out_shape=outs,
        compiler_params=_params("arbitrary"),
        name="in_proj",
    )(xt, mod, g, w, wa2, ba2, cw, cb, cnt, wpool, pscale)


def _gla_chunk(q, k, v, la, s_ref, o_ref, r0, consts, fwd):
    tri, rexp, bdmask, hq, hv = consts
    c, sub = GLA_CHUNK, GLA_SUB
    nt = (((1,), (1,)), ((), ()))
    tn = (((0,), (0,)), ((), ()))
    la_hi = la.astype(BF16)
    la_lo = (la - la_hi.astype(F32)).astype(BF16)
    cum = jnp.dot(tri, la_hi, preferred_element_type=F32) + jnp.dot(tri, la_lo, preferred_element_type=F32)
    end = cum[c - 1:c] if fwd else cum[0:1]
    s_prev = s_ref[0]
    qe = (q * jnp.exp(cum)).astype(BF16)
    o_all = lax.dot_general(qe, s_prev.astype(BF16), nt, preferred_element_type=F32)
    kd = (k * jnp.exp(end - cum)).astype(BF16)
    upd = lax.dot_general(v.astype(BF16), kd, tn, preferred_element_type=F32)
    s_ref[0] = s_prev * jnp.exp(end) + upd * bdmask
    o_blk = [o_all[sub * r:sub * (r + 1)] for r in range(c // sub)]

    b = c // 2
    while b >= sub:
        npair = c // (2 * b)
        qs, ks, vk = [], [], []
        for p in range(npair):
            f0, s0 = 2 * p * b, 2 * p * b + b
            if fwd:
                ref, qsl, ksl = cum[s0 - 1:s0], slice(s0, s0 + b), slice(f0, f0 + b)
            else:
                ref, qsl, ksl = cum[s0:s0 + 1], slice(f0, f0 + b), slice(s0, s0 + b)
            qs.append(q[qsl] * jnp.exp(cum[qsl] - ref))
            ks.append(k[ksl] * jnp.exp(ref - cum[ksl]))
            vk.append(v[ksl])
        qs = jnp.concatenate(qs, axis=0) if npair > 1 else qs[0]
        ks = jnp.concatenate(ks, axis=0) if npair > 1 else ks[0]
        vk = jnp.concatenate(vk, axis=0) if npair > 1 else vk[0]
        half = c // 2
        qst = jnp.concatenate([qs * hq[h] for h in range(GLA_HEADS)], axis=0).astype(BF16)
        att = lax.dot_general(qst, ks.astype(BF16), nt, preferred_element_type=F32)
        if npair > 1:
            ri = lax.broadcasted_iota(jnp.int32, att.shape, 0)
            ci = lax.broadcasted_iota(jnp.int32, att.shape, 1)
            att = jnp.where(((ri & (half - 1)) ^ ci) < b, att, 0.0)
        res4 = jnp.dot(att.astype(BF16), vk.astype(BF16), preferred_element_type=F32)
        res = res4[0:half] * hv[0]
        for h in range(1, GLA_HEADS):
            res = res + res4[h * half:(h + 1) * half] * hv[h]
        for p in range(npair):
            q0 = (2 * p * b + b) if fwd else 2 * p * b
            for t in range(b // sub):
                o_blk[q0 // sub + t] = o_blk[q0 // sub + t] + res[p * b + sub * t:p * b + sub * (t + 1)]
        b //= 2

    ii = lax.broadcasted_iota(jnp.int32, (sub, GLA_QK), 0)
    for r in range(c // sub):
        sl = slice(sub * r, sub * (r + 1))
        cl, qb, kb, vb = cum[sl], q[sl], k[sl], v[sl]
        es = []
        for j in range(sub):
            keep = (ii >= j) if fwd else (ii <= j)
            w = jnp.exp(jnp.where(keep, cl - cl[j:j + 1], NEG_BIG))
            es.append((w * (qb * kb[j:j + 1])).astype(BF16))
        pr = jnp.dot(jnp.concatenate(es, axis=0), rexp, preferred_element_type=F32)
        od = pr[0:sub] * vb[0:1]
        for j in range(1, sub):
            od = od + pr[sub * j:sub * (j + 1)] * vb[j:j + 1]
        o_ref[pl.ds(r0 + sub * r, sub), :] = o_blk[r] + od


def _gla_kernel(kqf_ref, vf_ref, laf_ref, kqb_ref, vb_ref, lab_ref, h0f_ref, h0b_ref,
                trif_ref, trib_ref, rexp_ref, bd_ref,
                of_ref, ob_ref, sf_ref, sb_ref):
    @pl.when(pl.program_id(1) == 0)
    def _():
        sf_ref[...] = h0f_ref[...]
        sb_ref[...] = h0b_ref[...]

    lq = lax.broadcasted_iota(jnp.int32, (1, GLA_QK), 1)
    lv = lax.broadcasted_iota(jnp.int32, (1, W_GLA), 1)
    hq = [jnp.where((lq >= h * GLA_DK) & (lq < (h + 1) * GLA_DK), 1.0, 0.0) for h in range(GLA_HEADS)]
    hv = [jnp.where((lv >= h * GLA_DV) & (lv < (h + 1) * GLA_DV), 1.0, 0.0) for h in range(GLA_HEADS)]
    rexp, bdmask = rexp_ref[...], bd_ref[...]
    nchunk = kqf_ref.shape[0] // GLA_CHUNK

    def body(ci, carry):
        r0 = pl.multiple_of(ci * GLA_CHUNK, GLA_CHUNK)
        rows = pl.ds(r0, GLA_CHUNK)
        _gla_chunk(kqf_ref[rows, GLA_QK:], kqf_ref[rows, :GLA_QK], vf_ref[rows, :], laf_ref[rows, :],
                   sf_ref, of_ref, r0, (trif_ref[...], rexp, bdmask, hq, hv), True)
        r1 = pl.multiple_of((nchunk - 1 - ci) * GLA_CHUNK, GLA_CHUNK)
        rows = pl.ds(r1, GLA_CHUNK)
        _gla_chunk(kqb_ref[rows, GLA_QK:], kqb_ref[rows, :GLA_QK], vb_ref[rows, :], lab_ref[rows, :],
                   sb_ref, ob_ref, r1, (trib_ref[...], rexp, bdmask, hq, hv), False)
        return carry

    lax.fori_loop(0, nchunk, body, 0)


def _gla_consts():
    c = GLA_CHUNK
    i = np.arange(c)
    trif = (i[None, :] <= i[:, None]).astype(np.float32)
    trib = (i[None, :] >= i[:, None]).astype(np.float32)
    hk = np.arange(GLA_QK) // GLA_DK
    hd = np.arange(W_GLA) // GLA_DV
    rexp = (hk[:, None] == hd[None, :]).astype(np.float32)
    bd = (hd[:, None] == hk[None, :]).astype(np.float32)
    return (jnp.asarray(trif, BF16), jnp.asarray(trib, BF16), jnp.asarray(rexp, BF16), jnp.asarray(bd, F32))


def _gla_call(kq, v, la, h0f, h0b, *, batch):
    n = kq.shape[0]
    tg = GLA_BLOCK
    nb = n // batch // tg
    trif, trib, rexp, bd = _gla_consts()
    fw = lambda b, i: (b * nb + i, 0)
    bw = lambda b, i: (b * nb + nb - 1 - i, 0)
    bw1 = lambda b, i: (b * nb + nb - 1 - i, 1)
    st = lambda b, i: (b, 0, 0)
    sshape = jax.ShapeDtypeStruct((batch, W_GLA, GLA_QK), F32)
    return pl.pallas_call(
        _gla_kernel,
        grid=(batch, nb),
        in_specs=[pl.BlockSpec((tg, 2 * GLA_QK), fw), pl.BlockSpec((tg, W_GLA), fw), pl.BlockSpec((tg, GLA_QK), fw),
                  pl.BlockSpec((tg, 2 * GLA_QK), bw), pl.BlockSpec((tg, W_GLA), bw), pl.BlockSpec((tg, GLA_QK), bw1),
                  pl.BlockSpec((1, W_GLA, GLA_QK), st), pl.BlockSpec((1, W_GLA, GLA_QK), st),
                  _const_spec(trif.shape), _const_spec(trib.shape), _const_spec(rexp.shape), _const_spec(bd.shape)],
        out_specs=[pl.BlockSpec((tg, W_GLA), fw), pl.BlockSpec((tg, W_GLA), bw),
                   pl.BlockSpec((1, W_GLA, GLA_QK), st), pl.BlockSpec((1, W_GLA, GLA_QK), st)],
        out_shape=[jax.ShapeDtypeStruct((n, W_GLA), F32), jax.ShapeDtypeStruct((n, W_GLA), F32), sshape, sshape],
        compiler_params=_params("arbitrary", "arbitrary"),
        name="gla",
    )(kq, v, la, kq, v, la, h0f, h0b, trif, trib, rexp, bd)


def _fft_tail(z, half, cs_ref, wf_ref, norm):
    ab = jnp.concatenate([z[:half], z[half:]], axis=1).astype(BF16)
    f = jnp.dot(ab, cs_ref[...].astype(BF16), preferred_element_type=F32) * norm
    return jnp.dot(f.astype(BF16), wf_ref[...], preferred_element_type=F32).astype(BF16)


def _fft1_kernel(w_ref, u_ref, y_ref):
    y_ref[0] = jnp.dot(w_ref[...].astype(BF16), u_ref[0], preferred_element_type=F32).astype(BF16)


def _fft2_kernel(m_ref, y_ref, cs_ref, wf_ref, o_ref, *, group, norm):
    for g in range(group):
        ys = jnp.concatenate([y_ref[0, 0, g], y_ref[0, 1, g]], axis=0)
        z = jnp.dot(m_ref[g].astype(BF16), ys, preferred_element_type=F32)
        o_ref[0, :, g * W_FFT:(g + 1) * W_FFT] = _fft_tail(z, FFT_N2, cs_ref, wf_ref, norm)


def _fft_direct_kernel(m_ref, u_ref, cs_ref, wf_ref, o_ref, *, norm):
    z = jnp.dot(m_ref[...].astype(BF16), u_ref[...], preferred_element_type=F32)
    o_ref[...] = _fft_tail(z, u_ref.shape[0], cs_ref, wf_ref, norm)


def _dft_cos_sin(n):
    k = np.arange(n, dtype=np.int64)
    ang = 2.0 * np.pi * ((k[:, None] * k[None, :]) % n).astype(np.float64) / n
    return np.cos(ang), np.sin(ang)


def _channel_dft():
    c, s = _dft_cos_sin(FFT_DG)
    eye = np.eye(FFT_GROUPS)
    return jnp.asarray(np.concatenate([np.kron(eye, c), np.kron(eye, s)], axis=0), F32)


def _fft_latent_call(u, wf_bd, *, batch):
    n = u.shape[0] // batch
    n1, n2 = n // FFT_N2, FFT_N2
    c1, s1 = _dft_cos_sin(n1)
    w1 = jnp.asarray(np.concatenate([c1, -s1], axis=0), F32)
    k1 = np.arange(n1, dtype=np.int64)[:, None, None]
    k2 = np.arange(n2, dtype=np.int64)[None, :, None]
    m2 = np.arange(n2, dtype=np.int64)[None, None, :]
    ang = 2.0 * np.pi * ((m2 * (k1 + n1 * k2)) % n).astype(np.float64) / n
    cm, sm = np.cos(ang), np.sin(ang)
    mats = jnp.asarray(np.concatenate([np.concatenate([cm, sm], axis=2),
                                       np.concatenate([-sm, cm], axis=2)], axis=1), F32)
    cs = _channel_dft()
    norm = float(1.0 / np.sqrt(n * FFT_DG))
    width = n2 * W_FFT
    ch = min(width, 8192)
    y = pl.pallas_call(
        _fft1_kernel,
        grid=(batch, width // ch),
        in_specs=[_const_spec(w1.shape), pl.BlockSpec((1, n1, ch), lambda b, j: (b, 0, j))],
        out_specs=pl.BlockSpec((1, 2 * n1, ch), lambda b, j: (b, 0, j)),
        out_shape=jax.ShapeDtypeStruct((batch, 2 * n1, width), BF16),
        compiler_params=_params("arbitrary", "arbitrary"),
        name="fft_stage1",
    )(w1, u.reshape(batch, n1, width))
    group = min(n1, 8)
    out = pl.pallas_call(
        functools.partial(_fft2_kernel, group=group, norm=norm),
        grid=(n1 // group, batch),
        in_specs=[pl.BlockSpec((group, 2 * n2, 2 * n2), lambda j, b: (j, 0, 0)),
                  pl.BlockSpec((1, 2, group, n2, W_FFT), lambda j, b: (b, 0, j, 0, 0)),
                  _const_spec(cs.shape), _const_spec(wf_bd.shape)],
        out_specs=pl.BlockSpec((1, n2, group * W_FFT), lambda j, b: (b, 0, j)),
        out_shape=jax.ShapeDtypeStruct((batch, n2, n1 * W_FFT), BF16),
        compiler_params=_params("arbitrary", "arbitrary"),
        name="fft_stage2",
    )(mats, y.reshape(batch, 2, n1, n2, W_FFT), cs, wf_bd)
    return out.reshape(batch * n, W_FFT)


def _fft_direct_call(u, wf_bd, *, batch):
    n = u.shape[0] // batch
    c, s = _dft_cos_sin(n)
    m = jnp.asarray(np.concatenate([c, -s], axis=0), F32)
    cs = _channel_dft()
    norm = float(1.0 / np.sqrt(n * FFT_DG))
    return pl.pallas_call(
        functools.partial(_fft_direct_kernel, norm=norm),
        grid=(batch,),
        in_specs=[_const_spec(m.shape), pl.BlockSpec((n, W_FFT), lambda b: (b, 0)),
                  _const_spec(cs.shape), _const_spec(wf_bd.shape)],
        out_specs=pl.BlockSpec((n, W_FFT), lambda b: (b, 0)),
        out_shape=jax.ShapeDtypeStruct((batch * n, W_FFT), BF16),
        compiler_params=_params("arbitrary"),
        name="fft_direct",
    )(m, u, cs, wf_bd)


def _out_kernel(x_ref, mod_ref, of_ref, ob_ref, gate_ref, fft_ref, ycp_ref, glag_ref, ones_ref, wout_ref,
                n2g_ref, wup_ref, cw_ref, cb_ref, wdn_ref, fing_ref, o_ref,
                x1_ref, h2_ref, au_ref, h_ref, acc_ref, *, row_len, final):
    t_rows = x_ref.shape[0]
    d_ff = wdn_ref.shape[0]
    nchunk = d_ff // FFN_CHUNK
    o = of_ref[...] + ob_ref[...]
    osq = o * o
    hi = osq.astype(BF16)
    lo = (osq - hi.astype(F32)).astype(BF16)
    ms = (jnp.dot(hi, ones_ref[...], preferred_element_type=F32)
          + jnp.dot(lo, ones_ref[...], preferred_element_type=F32)) * (1.0 / GLA_DV)
    gl = (o * lax.rsqrt(ms + EPS)) * glag_ref[...] * _silu(gate_ref[...])
    ymix = jnp.concatenate([gl.astype(BF16), fft_ref[...], ycp_ref[...]], axis=1)
    x1 = x_ref[...] + mod_ref[0, 2:3, :] * jnp.dot(ymix, wout_ref[...], preferred_element_type=F32)
    x1_ref[...] = x1
    h2_ref[...] = _rms_mod(x1, n2g_ref[...], mod_ref[0, 4:5, :], mod_ref[0, 3:4, :]).astype(BF16)
    acc_ref[...] = jnp.zeros(acc_ref.shape, F32)

    def up(c, slot):
        cols = pl.ds(pl.multiple_of(c * (2 * FFN_CHUNK), 2 * FFN_CHUNK), 2 * FFN_CHUNK)
        au_ref[slot] = jnp.dot(h2_ref[...], wup_ref[:, cols], preferred_element_type=F32)

    def elem(c, slot):
        cols = pl.ds(pl.multiple_of(c * FFN_CHUNK, FFN_CHUNK), FFN_CHUNK)
        pos = lax.broadcasted_iota(jnp.int32, (t_rows, FFN_CHUNK), 0) & (row_len - 1)
        a = au_ref[slot, :, :FFN_CHUNK]
        cw = cw_ref[:, cols]
        a = (jnp.where(pos == 0, 0.0, pltpu.roll(a, 1, 0)) * cw[0:1] + a * cw[1:2]
             + jnp.where(pos == row_len - 1, 0.0, pltpu.roll(a, t_rows - 1, 0)) * cw[2:3] + cb_ref[:, cols])
        h_ref[slot] = (_silu(a) * au_ref[slot, :, FFN_CHUNK:]).astype(BF16)

    def down(c, slot):
        rows = pl.ds(pl.multiple_of(c * FFN_CHUNK, FFN_CHUNK), FFN_CHUNK)
        acc_ref[...] += jnp.dot(h_ref[slot], wdn_ref[rows, :], preferred_element_type=F32)

    def stage(c):
        up(c + 1, 0)
        elem(c, 1)
        down(c - 1, 0)
        up(c + 2, 1)
        elem(c + 1, 0)
        down(c, 1)

    assert nchunk % 2 == 1 and nchunk >= 3
    up(0, 0)
    up(1, 1)
    elem(0, 0)

    for i in range((nchunk - 3) // 2):
        stage(2 * i + 1)
    c = nchunk - 2
    up(c + 1, 0)
    elem(c, 1)
    down(c - 1, 0)
    elem(c + 1, 0)
    down(c, 1)
    down(c + 1, 0)
    x2 = x1_ref[...] + mod_ref[0, 5:6, :] * acc_ref[...]
    if final:
        ms2 = jnp.mean(x2 * x2, axis=-1, keepdims=True)
        x2 = (x2 * lax.rsqrt(ms2 + EPS)) * fing_ref[...]
    o_ref[...] = x2


def _out_call(xt, mod, o_f, o_b, gate, yfft, ycp, glag, wout, n2g, wup, cw, cb, wdn, fing,
              *, row_len, final):
    n, d = xt.shape
    t = OUT_TILE
    tiles_per_group = n // mod.shape[0] // t
    assert tiles_per_group * t * mod.shape[0] == n
    hd = np.arange(W_GLA) // GLA_DV
    ones = jnp.asarray((hd[:, None] == hd[None, :]).astype(np.float32), BF16)
    row = lambda i: (i, 0)
    acts = [xt, o_f, o_b, gate, yfft, ycp]
    consts = [glag, ones, wout, n2g, wup, cw, cb, wdn, fing]
    return pl.pallas_call(
        functools.partial(_out_kernel, row_len=row_len, final=final),
        grid=(n // t,),
        in_specs=[pl.BlockSpec((t, d), row), pl.BlockSpec((1, 6, d), lambda i: (i // tiles_per_group, 0, 0))]
                 + [pl.BlockSpec((t, a.shape[1]), row) for a in acts[1:]]
                 + [_const_spec(a.shape) for a in consts],
        out_specs=pl.BlockSpec((t, d), row),
        out_shape=jax.ShapeDtypeStruct((n, d), F32),
        scratch_shapes=[pltpu.VMEM((t, d), F32), pltpu.VMEM((t, d), BF16),
                        pltpu.VMEM((2, t, 2 * FFN_CHUNK), F32), pltpu.VMEM((2, t, FFN_CHUNK), BF16),
                        pltpu.VMEM((t, d), F32)],
        compiler_params=_params("arbitrary"),
        name="out_ffn",
    )(xt, mod, *acts[1:], *consts)


def _block_diag(w):
    g, a, b = w.shape
    eye = jnp.eye(g, dtype=w.dtype)
    return (eye[:, None, :, None] * w[:, :, None, :]).reshape(g * a, g * b)


def _interleave_w_up(w):
    d, two_ff = w.shape
    nchunk = two_ff // (2 * FFN_CHUNK)
    return w.reshape(d, 2, nchunk, FFN_CHUNK).transpose(0, 2, 1, 3).reshape(d, two_ff).astype(BF16)


def _reorder_w_in(w):
    col = {name: w[:, s:s + z] for name, s, z in zip(
        ("k", "v", "af", "ab", "q", "g", "fft", "h", "bg", "cg", "pool"), COL_STARTS, COL_SIZES)}
    pad = jnp.zeros((w.shape[0], LANES - 2 * GLA_RANK), w.dtype)
    order = ("k", "q", "v", "g", "fft", "h", "bg", "cg", "pool", "af", "ab")
    return jnp.concatenate([col[o] for o in order] + [pad], axis=1).astype(BF16)


def _decay_weights(w_a2, b_a2):
    wa2 = jnp.zeros((LANES, 2 * GLA_QK), F32)
    wa2 = wa2.at[0:GLA_RANK, 0:GLA_QK].set(w_a2[0])
    wa2 = wa2.at[GLA_RANK:2 * GLA_RANK, GLA_QK:].set(w_a2[1])
    return wa2.astype(BF16), b_a2.reshape(1, 2 * GLA_QK)


def kernel(x, c, ctx, c_ctx, norm1_g, norm2_g, w_mod, b_mod, w_in, gla_w_a2, gla_b_a2, gla_norm_g,
           fft_w, conv_w, conv_b, pool_w, pool_scale, w_out, ffn_w_up, ffn_conv_w, ffn_conv_b,
           ffn_w_down, final_norm_g):
    batch, seq, d = x.shape
    ctx_len = ctx.shape[1]
    depth = w_mod.shape[0]
    assert seq % GLA_BLOCK == 0 and seq % FFT_N2 == 0 and ctx_len == TOKEN_TILE == GLA_BLOCK

    rows = 8
    cc = jnp.concatenate([c, c_ctx[None, :], jnp.zeros((rows - batch - 1, d), F32)], axis=0)
    mod = _mod_call(cc, w_mod, b_mod)

    xt = x.reshape(batch * seq, d)
    xc = ctx.reshape(batch * ctx_len, d)
    zero_state = jnp.zeros((batch, W_GLA, GLA_QK), F32)
    fing = final_norm_g.reshape(1, d)
    for i in range(depth):
        last = i == depth - 1
        mod_lat = mod[i, :batch].reshape(batch, 6, d)
        mod_ctx = mod[i, batch].reshape(1, 6, d)
        w_r = _reorder_w_in(w_in[i])
        wa2, ba2 = _decay_weights(gla_w_a2[i], gla_b_a2[i])
        wpool = _block_diag(pool_w[i]).astype(BF16)
        wf_bd = _block_diag(fft_w[i]).astype(BF16)
        in_args = (norm1_g[i].reshape(1, d), w_r, wa2, ba2, conv_w[i], conv_b[i].reshape(1, -1), wpool,
                   pool_scale[i].reshape(1, -1))
        out_args = (jnp.tile(gla_norm_g[i], GLA_HEADS).reshape(1, W_GLA), w_out[i].astype(BF16),
                    norm2_g[i].reshape(1, d), _interleave_w_up(ffn_w_up[i]), ffn_conv_w[i],
                    ffn_conv_b[i].reshape(1, -1), ffn_w_down[i].astype(BF16), fing)

        kq, v, la, gate, ufft, ycp = _in_call(xc, mod_ctx, *in_args, row_len=ctx_len)
        o_f, o_b, s_f, s_b = _gla_call(kq, v, la, zero_state, zero_state, batch=batch)
        if not last:
            yfft = _fft_direct_call(ufft, wf_bd, batch=batch)
            xc = _out_call(xc, mod_ctx, o_f, o_b, gate, yfft, ycp, *out_args,
                           row_len=ctx_len, final=False)

        kq, v, la, gate, ufft, ycp = _in_call(xt, mod_lat, *in_args, row_len=GRID_W)
        o_f, o_b, _, _ = _gla_call(kq, v, la, s_f, s_b, batch=batch)
        yfft = _fft_latent_call(ufft, wf_bd, batch=batch)
        xt = _out_call(xt, mod_lat, o_f, o_b, gate, yfft, ycp, *out_args,
                       row_len=GRID_W, final=last)
    return xt.reshape(batch, seq, d)
```

```python
import functools

import numpy as np
import jax
import jax.numpy as jnp
from jax import lax
from jax.experimental import pallas as pl
from jax.experimental.pallas import tpu as pltpu

F32 = jnp.float32
BF16 = jnp.bfloat16

GRID_W = 64
EPS = 1e-6
GLA_HEADS = 4
GLA_DK = 32
GLA_DV = 64
GLA_QK = GLA_HEADS * GLA_DK
W_GLA = GLA_HEADS * GLA_DV
GLA_RANK = 16
GLA_TAU = 16.0
FFT_GROUPS = 4
FFT_DG = 64
W_FFT = FFT_GROUPS * FFT_DG
W_CONV = 256
POOL_WINDOWS = (2, 4, 8, 16)
POOL_DG = 64
W_POOL = len(POOL_WINDOWS) * POOL_DG
COL_SIZES = (GLA_QK, W_GLA, GLA_RANK, GLA_RANK, GLA_QK, W_GLA, W_FFT, W_CONV, W_CONV, W_CONV, W_POOL)
COL_STARTS = tuple(int(s) for s in np.cumsum((0,) + COL_SIZES)[:-1])

LANES = 128
VMEM_LIMIT_BYTES = 56 * 1024 * 1024

TOKEN_TILE = 256
IN_TILE = 1024
OUT_TILE = 512
GLA_CHUNK = 64
GLA_SUB = 8
GLA_BLOCK = 256
FFN_CHUNK = 256
FFT_N2 = 128
NEG_BIG = -1e30
LOG2_E = 1.4426950408889634

IN_K, IN_Q, IN_V, IN_G, IN_FFT, IN_H, IN_BG, IN_CG, IN_POOL, IN_A = (
    0, 128, 256, 512, 768, 1024, 1280, 1536, 1792, 2048)
IN_COLS = IN_A + LANES


def _const_spec(shape):
    nd = len(shape)
    return pl.BlockSpec(shape, lambda *_: (0,) * nd, pipeline_mode=pl.Buffered(1))


def _params(*sem):
    return pltpu.CompilerParams(dimension_semantics=sem, vmem_limit_bytes=VMEM_LIMIT_BYTES)


def _rms_mod(x, g, scale, shift):
    ms = jnp.mean(x * x, axis=-1, keepdims=True)
    return (x * lax.rsqrt(ms + EPS)) * g * (1.0 + scale) + shift


def _silu(a):
    return a * jax.nn.sigmoid(a)


def _mod_kernel(c_ref, w_ref, b_ref, o_ref):
    s = _silu(c_ref[...]).astype(BF16)
    o_ref[0] = jnp.dot(s, w_ref[0].astype(BF16), preferred_element_type=F32) + b_ref[0]


def _mod_call(cc, w_mod, b_mod):
    depth, d, n = w_mod.shape
    tn = 1536
    return pl.pallas_call(
        _mod_kernel,
        grid=(depth, n // tn),
        in_specs=[pl.BlockSpec(cc.shape, lambda i, j: (0, 0)),
                  pl.BlockSpec((1, d, tn), lambda i, j: (i, 0, j)),
                  pl.BlockSpec((1, 1, tn), lambda i, j: (i, 0, j))],
        out_specs=pl.BlockSpec((1, cc.shape[0], tn), lambda i, j: (i, 0, j)),
        out_shape=jax.ShapeDtypeStruct((depth, cc.shape[0], n), F32),
        compiler_params=_params("arbitrary", "arbitrary"),
        name="modulation",
    )(cc, w_mod, b_mod.reshape(depth, 1, n))


def _in_kernel(x_ref, mod_ref, g_ref, w_ref, wa2_ref, ba2_ref, cw_ref, cb_ref, cnt_ref, wpool_ref, pscale_ref,
               kq_ref, v_ref, la_ref, gate_ref, fft_ref, ycp_ref, *, row_len):
    nsub = x_ref.shape[0] // TOKEN_TILE
    ps = []
    for s in range(nsub):
        rows = slice(s * TOKEN_TILE, (s + 1) * TOKEN_TILE)
        h = _rms_mod(x_ref[rows, :], g_ref[...], mod_ref[0, 1:2, :], mod_ref[0, 0:1, :]).astype(BF16)
        ps.append(jnp.dot(h, w_ref[...], preferred_element_type=F32))
    for s in range(nsub):
        rows = slice(s * TOKEN_TILE, (s + 1) * TOKEN_TILE)
        _in_mixers(ps[s], rows, wa2_ref, ba2_ref, cw_ref, cb_ref, cnt_ref, wpool_ref, pscale_ref,
                   kq_ref, v_ref, la_ref, gate_ref, fft_ref, ycp_ref, row_len)


def _in_mixers(p, rows, wa2_ref, ba2_ref, cw_ref, cb_ref, cnt_ref, wpool_ref, pscale_ref,
               kq_ref, v_ref, la_ref, gate_ref, fft_ref, ycp_ref, row_len):
    t_rows = p.shape[0]
    kq_ref[rows, :GLA_QK] = p[:, IN_K:IN_K + GLA_QK]
    kq_ref[rows, GLA_QK:] = p[:, IN_Q:IN_Q + GLA_QK] * (GLA_DK ** -0.5)
    v_ref[rows, :] = p[:, IN_V:IN_V + W_GLA]
    gate_ref[rows, :] = p[:, IN_G:IN_G + W_GLA]
    fft_ref[rows, :] = p[:, IN_FFT:IN_FFT + W_FFT].astype(BF16)
    z = jnp.dot(p[:, IN_A:IN_A + LANES].astype(BF16), wa2_ref[...], preferred_element_type=F32) + ba2_ref[...]
    la_ref[rows, :] = (jnp.minimum(z, 0.0) - jnp.log(1.0 + jnp.exp(-jnp.abs(z)))) * (1.0 / GLA_TAU)

    def pos(a):
        return lax.broadcasted_iota(jnp.int32, a.shape, 0) & (row_len - 1)

    def prev(a, s):
        return jnp.where(pos(a) >= s, pltpu.roll(a, s, 0), 0.0)

    def nxt(a, s):
        return jnp.where(pos(a) < row_len - s, pltpu.roll(a, t_rows - s, 0), 0.0)

    t = p[:, IN_CG:IN_CG + W_CONV] * p[:, IN_H:IN_H + W_CONV]
    cw = cw_ref[...]
    conv = prev(t, 1) * cw[0:1] + t * cw[1:2] + nxt(t, 1) * cw[2:3] + cb_ref[...]
    ycp_ref[rows, :W_CONV] = (p[:, IN_BG:IN_BG + W_CONV] * conv).astype(BF16)

    u = p[:, IN_POOL:IN_POOL + W_POOL]
    halves = []
    for side, steps in ((0, 1), (1, 3)):
        f = u[:, side * LANES:(side + 1) * LANES]
        g = prev(f, 1)
        sums = [g + f]
        for i in range(steps):
            f = f + nxt(f, 1 << i)
            g = g + prev(g, 1 << i)
            sums.append(g + f)
        lane = lax.broadcasted_iota(jnp.int32, f.shape, 1)
        halves.append(jnp.where(lane < POOL_DG, sums[-2], sums[-1]))
    tot = jnp.concatenate(halves, axis=1)
    pooled = tot / cnt_ref[...] - u
    yp = jnp.dot(pooled.astype(BF16), wpool_ref[...], preferred_element_type=F32) * pscale_ref[...]
    ycp_ref[rows, W_CONV:] = yp.astype(BF16)


def _pool_counts(t_rows, row_len):
    pos = np.arange(t_rows) % row_len
    cols = []
    for w in POOL_WINDOWS:
        lo = np.clip(pos - w // 2, 0, row_len - 1)
        hi = np.clip(pos + w // 2 - 1, 0, row_len - 1)
        cols.append(np.repeat((hi - lo + 1).astype(np.float32)[:, None], POOL_DG, axis=1))
    return np.concatenate(cols, axis=1)


def _in_call(xt, mod, g, w, wa2, ba2, cw, cb, wpool, pscale, *, row_len):
    n, d = xt.shape
    t = min(IN_TILE, n // mod.shape[0])
    tiles_per_group = n // mod.shape[0] // t
    assert tiles_per_group * t * mod.shape[0] == n
    cnt = jnp.asarray(_pool_counts(TOKEN_TILE, row_len))
    row = lambda i: (i, 0)
    outs = [jax.ShapeDtypeStruct((n, 2 * GLA_QK), F32), jax.ShapeDtypeStruct((n, W_GLA), F32),
            jax.ShapeDtypeStruct((n, 2 * GLA_QK), F32), jax.ShapeDtypeStruct((n, W_GLA), F32),
            jax.ShapeDtypeStruct((n, W_FFT), BF16), jax.ShapeDtypeStruct((n, W_CONV + W_POOL), BF16)]
    return pl.pallas_call(
        functools.partial(_in_kernel, row_len=row_len),
        grid=(n // t,),
        in_specs=[pl.BlockSpec((t, d), row),
                  pl.BlockSpec((1, 6, d), lambda i: (i // tiles_per_group, 0, 0)),
                  _const_spec(g.shape), _const_spec(w.shape), _const_spec(wa2.shape), _const_spec(ba2.shape),
                  _const_spec(cw.shape), _const_spec(cb.shape), _const_spec(cnt.shape),
                  _const_spec(wpool.shape), _const_spec(pscale.shape)],
        out_specs=[pl.BlockSpec((t, o.shape[1]), row) for o in outs],
        out_shape=outs,
        compiler_params=_params("arbitrary"),
        name="in_proj",
    )(xt, mod, g, w, wa2, ba2, cw, cb, cnt, wpool, pscale)


def _gla_block(q, k, v, la, s_ref, consts, fwd):
    tri2, rexp, bdmask, hq, hv = consts
    t_rows = q.shape[0]
    c, sub = GLA_CHUNK, GLA_SUB
    nchunk = t_rows // c
    nt = (((1,), (1,)), ((), ()))
    tn = (((0,), (0,)), ((), ()))
    la_hi = la.astype(BF16)
    la_lo = (la - la_hi.astype(F32)).astype(BF16)
    ce = (jnp.dot(tri2, la_hi, preferred_element_type=F32)
          + jnp.dot(tri2, la_lo, preferred_element_type=F32)) * LOG2_E
    cum, end = ce[:t_rows], ce[t_rows:]
    qe = (q * jnp.exp2(cum)).astype(BF16)
    kd = (k * jnp.exp2(end - cum)).astype(BF16)
    v16 = v.astype(BF16)

    s = s_ref[0]
    o_parts = [None] * nchunk
    for ci in (range(nchunk) if fwd else reversed(range(nchunk))):
        rows = slice(ci * c, (ci + 1) * c)
        o_parts[ci] = lax.dot_general(qe[rows], s.astype(BF16), nt, preferred_element_type=F32)
        upd = lax.dot_general(v16[rows], kd[rows], tn, preferred_element_type=F32)
        s = s * jnp.exp2(end[ci * c:ci * c + 1]) + upd * bdmask
    s_ref[0] = s
    o = jnp.concatenate(o_parts, axis=0)

    half = t_rows // 2
    b = c // 2
    while b >= sub:
        npair = t_rows // (2 * b)
        refs = []
        for p in range(npair):
            r = 2 * b * p + (b - 1 if fwd else b)
            refs.append(jnp.broadcast_to(cum[r:r + 1], (2 * b, GLA_QK)))
        e = jnp.exp2(-jnp.abs(cum - jnp.concatenate(refs, axis=0)))
        qfull, kfull = q * e, k * e
        first = [slice(2 * b * p, 2 * b * p + b) for p in range(npair)]
        second = [slice(2 * b * p + b, 2 * b * p + 2 * b) for p in range(npair)]
        qrows, krows = (second, first) if fwd else (first, second)
        qsel = jnp.concatenate([qfull[r] for r in qrows], axis=0)
        ksel = jnp.concatenate([kfull[r] for r in krows], axis=0).astype(BF16)
        vsel = jnp.concatenate([v16[r] for r in krows], axis=0)
        qst = jnp.concatenate([qsel * hq[h] for h in range(GLA_HEADS)], axis=0).astype(BF16)
        att = lax.dot_general(qst, ksel, nt, preferred_element_type=F32)
        ri = lax.broadcasted_iota(jnp.int32, att.shape, 0)
        ci = lax.broadcasted_iota(jnp.int32, att.shape, 1)
        att = jnp.where(((ri & (half - 1)) ^ ci) < b, att, 0.0).astype(BF16)
        att_k = jnp.concatenate([att[h * half:(h + 1) * half] for h in range(GLA_HEADS)], axis=1)
        v_k = jnp.concatenate([vsel * hv[h].astype(BF16) for h in range(GLA_HEADS)], axis=0)
        res = jnp.dot(att_k, v_k, preferred_element_type=F32)
        zero = jnp.zeros((b, W_GLA), F32)
        pieces = []
        for p in range(npair):
            piece = res[p * b:(p + 1) * b]
            pieces += [zero, piece] if fwd else [piece, zero]
        o = o + jnp.concatenate(pieces, axis=0)
        b //= 2

    nblk = t_rows // sub

    def row_of_block(x, j):
        w = x.shape[1]
        xb = x.reshape(nblk, sub, w)[:, j:j + 1, :]
        return jnp.broadcast_to(xb, (nblk, sub, w)).reshape(t_rows, w)

    ii = lax.broadcasted_iota(jnp.int32, (t_rows, GLA_QK), 0) & (sub - 1)
    es = []
    for j in range(sub):
        keep = (ii >= j) if fwd else (ii <= j)
        w = jnp.exp2(jnp.where(keep, cum - row_of_block(cum, j), NEG_BIG))
        es.append((w * (q * row_of_block(k, j))).astype(BF16))
    pr = jnp.dot(jnp.concatenate(es, axis=0), rexp, preferred_element_type=F32)
    for j in range(sub):
        o = o + pr[t_rows * j:t_rows * (j + 1)] * row_of_block(v, j)
    return o


def _gla_kernel(kqf_ref, vf_ref, laf_ref, kqb_ref, vb_ref, lab_ref, h0f_ref, h0b_ref,
                trif_ref, trib_ref, rexp_ref, bd_ref,
                of_ref, ob_ref, sf_ref, sb_ref):
    @pl.when(pl.program_id(1) == 0)
    def _():
        sf_ref[...] = h0f_ref[...]
        sb_ref[...] = h0b_ref[...]

    lq = lax.broadcasted_iota(jnp.int32, (1, GLA_QK), 1)
    lv = lax.broadcasted_iota(jnp.int32, (1, W_GLA), 1)
    hq = [jnp.where((lq >= h * GLA_DK) & (lq < (h + 1) * GLA_DK), 1.0, 0.0) for h in range(GLA_HEADS)]
    hv = [jnp.where((lv >= h * GLA_DV) & (lv < (h + 1) * GLA_DV), 1.0, 0.0) for h in range(GLA_HEADS)]
    rexp, bdmask = rexp_ref[...], bd_ref[...]
    of_ref[...] = _gla_block(kqf_ref[:, GLA_QK:], kqf_ref[:, :GLA_QK], vf_ref[...], laf_ref[...], sf_ref,
                             (trif_ref[...], rexp, bdmask, hq, hv), True)
    ob_ref[...] = _gla_block(kqb_ref[:, GLA_QK:], kqb_ref[:, :GLA_QK], vb_ref[...], lab_ref[...], sb_ref,
                             (trib_ref[...], rexp, bdmask, hq, hv), False)


def _gla_consts(t):
    c = GLA_CHUNK
    i = np.arange(t)
    same = (i[None, :] // c) == (i[:, None] // c)
    trif = np.concatenate([same & (i[None, :] <= i[:, None]), same], axis=0).astype(np.float32)
    trib = np.concatenate([same & (i[None, :] >= i[:, None]), same], axis=0).astype(np.float32)
    hk = np.arange(GLA_QK) // GLA_DK
    hd = np.arange(W_GLA) // GLA_DV
    rexp = (hk[:, None] == hd[None, :]).astype(np.float32)
    bd = (hd[:, None] == hk[None, :]).astype(np.float32)
    return (jnp.asarray(trif, BF16), jnp.asarray(trib, BF16), jnp.asarray(rexp, BF16), jnp.asarray(bd, F32))


def _gla_call(kq, v, la, h0f, h0b, *, batch):
    n = kq.shape[0]
    tg = min(GLA_BLOCK, n // batch)
    nb = n // batch // tg
    trif, trib, rexp, bd = _gla_consts(tg)
    fw = lambda b, i: (b * nb + i, 0)
    bw = lambda b, i: (b * nb + nb - 1 - i, 0)
    bw1 = lambda b, i: (b * nb + nb - 1 - i, 1)
    st = lambda b, i: (b, 0, 0)
    sshape = jax.ShapeDtypeStruct((batch, W_GLA, GLA_QK), F32)
    return pl.pallas_call(
        _gla_kernel,
        grid=(batch, nb),
        in_specs=[pl.BlockSpec((tg, 2 * GLA_QK), fw), pl.BlockSpec((tg, W_GLA), fw), pl.BlockSpec((tg, GLA_QK), fw),
                  pl.BlockSpec((tg, 2 * GLA_QK), bw), pl.BlockSpec((tg, W_GLA), bw), pl.BlockSpec((tg, GLA_QK), bw1),
                  pl.BlockSpec((1, W_GLA, GLA_QK), st), pl.BlockSpec((1, W_GLA, GLA_QK), st),
                  _const_spec(trif.shape), _const_spec(trib.shape), _const_spec(rexp.shape), _const_spec(bd.shape)],
        out_specs=[pl.BlockSpec((tg, W_GLA), fw), pl.BlockSpec((tg, W_GLA), bw),
                   pl.BlockSpec((1, W_GLA, GLA_QK), st), pl.BlockSpec((1, W_GLA, GLA_QK), st)],
        out_shape=[jax.ShapeDtypeStruct((n, W_GLA), F32), jax.ShapeDtypeStruct((n, W_GLA), F32), sshape, sshape],
        compiler_params=_params("arbitrary", "arbitrary"),
        name="gla",
    )(kq, v, la, kq, v, la, h0f, h0b, trif, trib, rexp, bd)


def _fft_tail(z, half, cs_ref, wf_ref, norm):
    ab = jnp.concatenate([z[:half], z[half:]], axis=1).astype(BF16)
    f = jnp.dot(ab, cs_ref[...].astype(BF16), preferred_element_type=F32) * norm
    return jnp.dot(f.astype(BF16), wf_ref[...], preferred_element_type=F32).astype(BF16)


def _fft1_kernel(w_ref, u_ref, y_ref):
    y_ref[0] = jnp.dot(w_ref[...].astype(BF16), u_ref[0], preferred_element_type=F32).astype(BF16)


def _fft2_kernel(m_ref, y_ref, cs_ref, wf_ref, o_ref, *, group, norm):
    for g in range(group):
        ys = jnp.concatenate([y_ref[0, 0, g], y_ref[0, 1, g]], axis=0)
        z = jnp.dot(m_ref[g].astype(BF16), ys, preferred_element_type=F32)
        o_ref[0, :, g * W_FFT:(g + 1) * W_FFT] = _fft_tail(z, FFT_N2, cs_ref, wf_ref, norm)


def _fft_direct_kernel(m_ref, u_ref, cs_ref, wf_ref, o_ref, *, norm):
    z = jnp.dot(m_ref[...].astype(BF16), u_ref[...], preferred_element_type=F32)
    o_ref[...] = _fft_tail(z, u_ref.shape[0], cs_ref, wf_ref, norm)


def _dft_cos_sin(n):
    k = np.arange(n, dtype=np.int64)
    ang = 2.0 * np.pi * ((k[:, None] * k[None, :]) % n).astype(np.float64) / n
    return np.cos(ang), np.sin(ang)


def _channel_dft():
    c, s = _dft_cos_sin(FFT_DG)
    eye = np.eye(FFT_GROUPS)
    return jnp.asarray(np.concatenate([np.kron(eye, c), np.kron(eye, s)], axis=0), F32)


def _fft_latent_call(u, wf_bd, *, batch):
    n = u.shape[0] // batch
    n1, n2 = n // FFT_N2, FFT_N2
    c1, s1 = _dft_cos_sin(n1)
    w1 = jnp.asarray(np.concatenate([c1, -s1], axis=0), F32)
    k1 = np.arange(n1, dtype=np.int64)[:, None, None]
    k2 = np.arange(n2, dtype=np.int64)[None, :, None]
    m2 = np.arange(n2, dtype=np.int64)[None, None, :]
    ang = 2.0 * np.pi * ((m2 * (k1 + n1 * k2)) % n).astype(np.float64) / n
    cm, sm = np.cos(ang), np.sin(ang)
    mats = jnp.asarray(np.concatenate([np.concatenate([cm, sm], axis=2),
                                       np.concatenate([-sm, cm], axis=2)], axis=1), F32)
    cs = _channel_dft()
    norm = float(1.0 / np.sqrt(n * FFT_DG))
    width = n2 * W_FFT
    ch = min(width, 8192)
    y = pl.pallas_call(
        _fft1_kernel,
        grid=(batch, width // ch),
        in_specs=[_const_spec(w1.shape), pl.BlockSpec((1, n1, ch), lambda b, j: (b, 0, j))],
        out_specs=pl.BlockSpec((1, 2 * n1, ch), lambda b, j: (b, 0, j)),
        out_shape=jax.ShapeDtypeStruct((batch, 2 * n1, width), BF16),
        compiler_params=_params("arbitrary", "arbitrary"),
        name="fft_stage1",
    )(w1, u.reshape(batch, n1, width))
    group = min(n1, 8)
    out = pl.pallas_call(
        functools.partial(_fft2_kernel, group=group, norm=norm),
        grid=(n1 // group, batch),
        in_specs=[pl.BlockSpec((group, 2 * n2, 2 * n2), lambda j, b: (j, 0, 0)),
                  pl.BlockSpec((1, 2, group, n2, W_FFT), lambda j, b: (b, 0, j, 0, 0)),
                  _const_spec(cs.shape), _const_spec(wf_bd.shape)],
        out_specs=pl.BlockSpec((1, n2, group * W_FFT), lambda j, b: (b, 0, j)),
        out_shape=jax.ShapeDtypeStruct((batch, n2, n1 * W_FFT), BF16),
        compiler_params=_params("arbitrary", "arbitrary"),
        name="fft_stage2",
    )(mats, y.reshape(batch, 2, n1, n2, W_FFT), cs, wf_bd)
    return out.reshape(batch * n, W_FFT)


def _fft_direct_call(u, wf_bd, *, batch):
    n = u.shape[0] // batch
    c, s = _dft_cos_sin(n)
    m = jnp.asarray(np.concatenate([c, -s], axis=0), F32)
    cs = _channel_dft()
    norm = float(1.0 / np.sqrt(n * FFT_DG))
    return pl.pallas_call(
        functools.partial(_fft_direct_kernel, norm=norm),
        grid=(batch,),
        in_specs=[_const_spec(m.shape), pl.BlockSpec((n, W_FFT), lambda b: (b, 0)),
                  _const_spec(cs.shape), _const_spec(wf_bd.shape)],
        out_specs=pl.BlockSpec((n, W_FFT), lambda b: (b, 0)),
        out_shape=jax.ShapeDtypeStruct((batch * n, W_FFT), BF16),
        compiler_params=_params("arbitrary"),
        name="fft_direct",
    )(m, u, cs, wf_bd)


def _out_kernel(x_ref, mod_ref, of_ref, ob_ref, gate_ref, fft_ref, ycp_ref, glag_ref, ones_ref, wout_ref,
                n2g_ref, wup_ref, cw_ref, cb_ref, wdn_ref, fing_ref, o_ref,
                x1_ref, h2_ref, au_ref, h_ref, acc_ref, *, row_len, final):
    t_rows = x_ref.shape[0]
    d_ff = wdn_ref.shape[0]
    nchunk = d_ff // FFN_CHUNK
    o = of_ref[...] + ob_ref[...]
    osq = o * o
    hi = osq.astype(BF16)
    lo = (osq - hi.astype(F32)).astype(BF16)
    ms = (jnp.dot(hi, ones_ref[...], preferred_element_type=F32)
          + jnp.dot(lo, ones_ref[...], preferred_element_type=F32)) * (1.0 / GLA_DV)
    gl = (o * lax.rsqrt(ms + EPS)) * glag_ref[...] * _silu(gate_ref[...])
    ymix = jnp.concatenate([gl.astype(BF16), fft_ref[...], ycp_ref[...]], axis=1)
    x1 = x_ref[...] + mod_ref[0, 2:3, :] * jnp.dot(ymix, wout_ref[...], preferred_element_type=F32)
    x1_ref[...] = x1
    h2_ref[...] = _rms_mod(x1, n2g_ref[...], mod_ref[0, 4:5, :], mod_ref[0, 3:4, :]).astype(BF16)
    acc_ref[...] = jnp.zeros(acc_ref.shape, F32)

    def up(c, slot):
        for half in range(2):
            cols = slice(half * d_ff + c * FFN_CHUNK, half * d_ff + (c + 1) * FFN_CHUNK)
            au_ref[slot, :, half * FFN_CHUNK:(half + 1) * FFN_CHUNK] = jnp.dot(
                h2_ref[...], wup_ref[:, cols], preferred_element_type=F32)

    def elem(c, slot):
        cols = slice(c * FFN_CHUNK, (c + 1) * FFN_CHUNK)
        pos = lax.broadcasted_iota(jnp.int32, (t_rows, FFN_CHUNK), 0) & (row_len - 1)
        a = au_ref[slot, :, :FFN_CHUNK]
        cw = cw_ref[:, cols]
        a = (jnp.where(pos == 0, 0.0, pltpu.roll(a, 1, 0)) * cw[0:1] + a * cw[1:2]
             + jnp.where(pos == row_len - 1, 0.0, pltpu.roll(a, t_rows - 1, 0)) * cw[2:3] + cb_ref[:, cols])
        h_ref[slot] = (_silu(a) * au_ref[slot, :, FFN_CHUNK:]).astype(BF16)

    def down(c, slot):
        rows = slice(c * FFN_CHUNK, (c + 1) * FFN_CHUNK)
        acc_ref[...] += jnp.dot(h_ref[slot], wdn_ref[rows, :], preferred_element_type=F32)

    def stage(c):
        up(c + 1, 0)
        elem(c, 1)
        down(c - 1, 0)
        up(c + 2, 1)
        elem(c + 1, 0)
        down(c, 1)

    assert nchunk % 2 == 1 and nchunk >= 3
    up(0, 0)
    up(1, 1)
    elem(0, 0)

    for i in range((nchunk - 3) // 2):
        stage(2 * i + 1)
    c = nchunk - 2
    up(c + 1, 0)
    elem(c, 1)
    down(c - 1, 0)
    elem(c + 1, 0)
    down(c, 1)
    down(c + 1, 0)
    x2 = x1_ref[...] + mod_ref[0, 5:6, :] * acc_ref[...]
    if final:
        ms2 = jnp.mean(x2 * x2, axis=-1, keepdims=True)
        x2 = (x2 * lax.rsqrt(ms2 + EPS)) * fing_ref[...]
    o_ref[...] = x2


def _out_call(xt, mod, o_f, o_b, gate, yfft, ycp, glag, wout, n2g, wup, cw, cb, wdn, fing,
              *, row_len, final):
    n, d = xt.shape
    t = OUT_TILE
    tiles_per_group = n // mod.shape[0] // t
    assert tiles_per_group * t * mod.shape[0] == n
    hd = np.arange(W_GLA) // GLA_DV
    ones = jnp.asarray((hd[:, None] == hd[None, :]).astype(np.float32), BF16)
    row = lambda i: (i, 0)
    acts = [xt, o_f, o_b, gate, yfft, ycp]
    consts = [glag, ones, wout, n2g, wup, cw, cb, wdn, fing]
    return pl.pallas_call(
        functools.partial(_out_kernel, row_len=row_len, final=final),
        grid=(n // t,),
        in_specs=[pl.BlockSpec((t, d), row), pl.BlockSpec((1, 6, d), lambda i: (i // tiles_per_group, 0, 0))]
                 + [pl.BlockSpec((t, a.shape[1]), row) for a in acts[1:]]
                 + [_const_spec(a.shape) for a in consts],
        out_specs=pl.BlockSpec((t, d), row),
        out_shape=jax.ShapeDtypeStruct((n, d), F32),
        scratch_shapes=[pltpu.VMEM((t, d), F32), pltpu.VMEM((t, d), BF16),
                        pltpu.VMEM((2, t, 2 * FFN_CHUNK), F32), pltpu.VMEM((2, t, FFN_CHUNK), BF16),
                        pltpu.VMEM((t, d), F32)],
        compiler_params=_params("arbitrary"),
        name="out_ffn",
    )(xt, mod, *acts[1:], *consts)


def _block_diag(w):
    g, a, b = w.shape
    eye = jnp.eye(g, dtype=w.dtype)
    return (eye[:, None, :, None] * w[:, :, None, :]).reshape(g * a, g * b)


def _reorder_w_in(w):
    col = {name: w[:, s:s + z] for name, s, z in zip(
        ("k", "v", "af", "ab", "q", "g", "fft", "h", "bg", "cg", "pool"), COL_STARTS, COL_SIZES)}
    pad = jnp.zeros((w.shape[0], LANES - 2 * GLA_RANK), w.dtype)
    order = ("k", "q", "v", "g", "fft", "h", "bg", "cg", "pool", "af", "ab")
    return jnp.concatenate([col[o] for o in order] + [pad], axis=1).astype(BF16)


def _decay_weights(w_a2, b_a2):
    wa2 = jnp.zeros((LANES, 2 * GLA_QK), F32)
    wa2 = wa2.at[0:GLA_RANK, 0:GLA_QK].set(w_a2[0])
    wa2 = wa2.at[GLA_RANK:2 * GLA_RANK, GLA_QK:].set(w_a2[1])
    return wa2.astype(BF16), b_a2.reshape(1, 2 * GLA_QK)


def kernel(x, c, ctx, c_ctx, norm1_g, norm2_g, w_mod, b_mod, w_in, gla_w_a2, gla_b_a2, gla_norm_g,
           fft_w, conv_w, conv_b, pool_w, pool_scale, w_out, ffn_w_up, ffn_conv_w, ffn_conv_b,
           ffn_w_down, final_norm_g):
    batch, seq, d = x.shape
    ctx_len = ctx.shape[1]
    depth = w_mod.shape[0]
    assert seq % GLA_BLOCK == 0 and seq % FFT_N2 == 0 and ctx_len == TOKEN_TILE

    rows = 8
    cc = jnp.concatenate([c, c_ctx[None, :], jnp.zeros((rows - batch - 1, d), F32)], axis=0)
    mod = _mod_call(cc, w_mod, b_mod)

    xt = x.reshape(batch * seq, d)
    xc = ctx.reshape(batch * ctx_len, d)
    zero_state = jnp.zeros((batch, W_GLA, GLA_QK), F32)
    fing = final_norm_g.reshape(1, d)
    for i in range(depth):
        last = i == depth - 1
        mod_lat = mod[i, :batch].reshape(batch, 6, d)
        mod_ctx = mod[i, batch].reshape(1, 6, d)
        w_r = _reorder_w_in(w_in[i])
        wa2, ba2 = _decay_weights(gla_w_a2[i], gla_b_a2[i])
        wpool = _block_diag(pool_w[i]).astype(BF16)
        wf_bd = _block_diag(fft_w[i]).astype(BF16)
        in_args = (norm1_g[i].reshape(1, d), w_r, wa2, ba2, conv_w[i], conv_b[i].reshape(1, -1), wpool,
                   pool_scale[i].reshape(1, -1))
        out_args = (jnp.tile(gla_norm_g[i], GLA_HEADS).reshape(1, W_GLA), w_out[i].astype(BF16),
                    norm2_g[i].reshape(1, d), ffn_w_up[i].astype(BF16), ffn_conv_w[i],
                    ffn_conv_b[i].reshape(1, -1), ffn_w_down[i].astype(BF16), fing)

        kq, v, la, gate, ufft, ycp = _in_call(xc, mod_ctx, *in_args, row_len=ctx_len)
        o_f, o_b, s_f, s_b = _gla_call(kq, v, la, zero_state, zero_state, batch=batch)
        if not last:
            yfft = _fft_direct_call(ufft, wf_bd, batch=batch)
            xc = _out_call(xc, mod_ctx, o_f, o_b, gate, yfft, ycp, *out_args,
                           row_len=ctx_len, final=False)

        kq, v, la, gate, ufft, ycp = _in_call(xt, mod_lat, *in_args, row_len=GRID_W)
        o_f, o_b, _, _ = _gla_call(kq, v, la, s_f, s_b, batch=batch)
        yfft = _fft_latent_call(ufft, wf_bd, batch=batch)
        xt = _out_call(xt, mod_lat, o_f, o_b, gate, yfft, ycp, *out_args,
                       row_len=GRID_W, final=last)
    return xt.reshape(batch, seq, d)
```

```python
import functools

import numpy as np
import jax
import jax.numpy as jnp
from jax import lax
from jax.experimental import pallas as pl
from jax.experimental.pallas import tpu as pltpu

F32 = jnp.float32
BF16 = jnp.bfloat16

GRID_W = 64
EPS = 1e-6
GLA_HEADS = 4
GLA_DK = 32
GLA_DV = 64
GLA_QK = GLA_HEADS * GLA_DK
W_GLA = GLA_HEADS * GLA_DV
GLA_RANK = 16
GLA_TAU = 16.0
FFT_GROUPS = 4
FFT_DG = 64
W_FFT = FFT_GROUPS * FFT_DG
W_CONV = 256
POOL_WINDOWS = (2, 4, 8, 16)
POOL_DG = 64
W_POOL = len(POOL_WINDOWS) * POOL_DG
COL_SIZES = (GLA_QK, W_GLA, GLA_RANK, GLA_RANK, GLA_QK, W_GLA, W_FFT, W_CONV, W_CONV, W_CONV, W_POOL)
COL_STARTS = tuple(int(s) for s in np.cumsum((0,) + COL_SIZES)[:-1])

LANES = 128
VMEM_LIMIT_BYTES = 56 * 1024 * 1024

TOKEN_TILE = 256
IN_TILE = 1024
OUT_TILE = 512
GLA_CHUNK = 64
GLA_SUB = 8
GLA_BLOCK = 256
FFN_CHUNK = 256
FFT_N2 = 128
FFT_STAGE1_GROUP = 16
FFT_STAGE2_GROUP = 8
NEG_BIG = -1e30
LOG2_E = 1.4426950408889634

IN_K, IN_V, IN_Q, IN_G, IN_FFT, IN_H, IN_BG, IN_CG, IN_POOL, IN_A = (
    0, 128, 384, 512, 768, 1024, 1280, 1536, 1792, 2048)
IN_COLS = IN_A + LANES
MOD_ROWS = 8


def _const_spec(shape):
    nd = len(shape)
    return pl.BlockSpec(shape, lambda *_: (0,) * nd, pipeline_mode=pl.Buffered(1))


def _layer_spec(arr, layer):
    nd = arr.ndim
    return pl.BlockSpec((None,) + arr.shape[1:], lambda *_: (layer,) + (0,) * (nd - 1),
                        pipeline_mode=pl.Buffered(1))


def _mod_spec(mod4, layer, base, tiles_per_group):
    return pl.BlockSpec((None, None) + mod4.shape[2:], lambda i: (layer, base + i // tiles_per_group, 0, 0))


def _params(*sem):
    return pltpu.CompilerParams(dimension_semantics=sem, vmem_limit_bytes=VMEM_LIMIT_BYTES)


def _rms_mod(x, g, scale, shift):
    ms = jnp.mean(x * x, axis=-1, keepdims=True)
    return (x * lax.rsqrt(ms + EPS)) * g * (1.0 + scale) + shift


def _silu(a):
    return a * jax.nn.sigmoid(a)


def _mod_kernel(c_ref, w_ref, b_ref, o_ref):
    s = _silu(c_ref[...]).astype(BF16)
    o_ref[0] = jnp.dot(s, w_ref[0].astype(BF16), preferred_element_type=F32) + b_ref[0]


def _mod_call(cc, w_mod, b_mod):
    depth, d, n = w_mod.shape
    tn = 1536
    return pl.pallas_call(
        _mod_kernel,
        grid=(depth, n // tn),
        in_specs=[pl.BlockSpec(cc.shape, lambda i, j: (0, 0)),
                  pl.BlockSpec((1, d, tn), lambda i, j: (i, 0, j)),
                  pl.BlockSpec((1, 1, tn), lambda i, j: (i, 0, j))],
        out_specs=pl.BlockSpec((1, cc.shape[0], tn), lambda i, j: (i, 0, j)),
        out_shape=jax.ShapeDtypeStruct((depth, cc.shape[0], n), F32),
        compiler_params=_params("arbitrary", "arbitrary"),
        name="modulation",
    )(cc, w_mod, b_mod.reshape(depth, 1, n))


def _in_kernel(x_ref, mod_ref, g_ref, w_ref, wa2_ref, ba2_ref, cw_ref, cb_ref, cnt_ref, wpool_ref, pscale_ref,
               kq_ref, v_ref, la_ref, gate_ref, fft_ref, ycp_ref, *, row_len):
    nsub = x_ref.shape[0] // TOKEN_TILE
    ps = []
    for s in range(nsub):
        rows = slice(s * TOKEN_TILE, (s + 1) * TOKEN_TILE)
        h = _rms_mod(x_ref[rows, :], g_ref[...], mod_ref[1:2, :], mod_ref[0:1, :]).astype(BF16)
        ps.append(jnp.dot(h, w_ref[...], preferred_element_type=F32))
    for s in range(nsub):
        rows = slice(s * TOKEN_TILE, (s + 1) * TOKEN_TILE)
        _in_mixers(ps[s], rows, wa2_ref, ba2_ref, cw_ref, cb_ref, cnt_ref, wpool_ref, pscale_ref,
                   kq_ref, v_ref, la_ref, gate_ref, fft_ref, ycp_ref, row_len)


def _in_mixers(p, rows, wa2_ref, ba2_ref, cw_ref, cb_ref, cnt_ref, wpool_ref, pscale_ref,
               kq_ref, v_ref, la_ref, gate_ref, fft_ref, ycp_ref, row_len):
    t_rows = p.shape[0]
    kq_ref[rows, :GLA_QK] = p[:, IN_K:IN_K + GLA_QK]
    kq_ref[rows, GLA_QK:] = p[:, IN_Q:IN_Q + GLA_QK] * (GLA_DK ** -0.5)
    v_ref[rows, :] = p[:, IN_V:IN_V + W_GLA]
    gate_ref[rows, :] = p[:, IN_G:IN_G + W_GLA]
    fft_ref[rows, :] = p[:, IN_FFT:IN_FFT + W_FFT]
    z = jnp.dot(p[:, IN_A:IN_A + LANES].astype(BF16), wa2_ref[...], preferred_element_type=F32) + ba2_ref[...]
    la_ref[rows, :] = (jnp.minimum(z, 0.0) - jnp.log(1.0 + jnp.exp(-jnp.abs(z)))) * (1.0 / GLA_TAU)

    def pos(a):
        return lax.broadcasted_iota(jnp.int32, a.shape, 0) & (row_len - 1)

    def prev(a, s):
        return jnp.where(pos(a) >= s, pltpu.roll(a, s, 0), 0.0)

    def nxt(a, s):
        return jnp.where(pos(a) < row_len - s, pltpu.roll(a, t_rows - s, 0), 0.0)

    t = p[:, IN_CG:IN_CG + W_CONV] * p[:, IN_H:IN_H + W_CONV]
    cw = cw_ref[...]
    conv = prev(t, 1) * cw[0:1] + t * cw[1:2] + nxt(t, 1) * cw[2:3] + cb_ref[...]
    ycp_ref[rows, :W_CONV] = (p[:, IN_BG:IN_BG + W_CONV] * conv).astype(BF16)

    u = p[:, IN_POOL:IN_POOL + W_POOL]
    halves = []
    for side, steps in ((0, 1), (1, 3)):
        f = u[:, side * LANES:(side + 1) * LANES]
        g = prev(f, 1)
        sums = [g + f]
        for i in range(steps):
            f = f + nxt(f, 1 << i)
            g = g + prev(g, 1 << i)
            sums.append(g + f)
        lane = lax.broadcasted_iota(jnp.int32, f.shape, 1)
        halves.append(jnp.where(lane < POOL_DG, sums[-2], sums[-1]))
    tot = jnp.concatenate(halves, axis=1)
    pooled = tot / cnt_ref[...] - u
    yp = jnp.dot(pooled.astype(BF16), wpool_ref[...], preferred_element_type=F32) * pscale_ref[...]
    ycp_ref[rows, W_CONV:] = yp.astype(BF16)


def _pool_counts(t_rows, row_len):
    pos = np.arange(t_rows) % row_len
    cols = []
    for w in POOL_WINDOWS:
        lo = np.clip(pos - w // 2, 0, row_len - 1)
        hi = np.clip(pos + w // 2 - 1, 0, row_len - 1)
        cols.append(np.repeat((hi - lo + 1).astype(np.float32)[:, None], POOL_DG, axis=1))
    return np.concatenate(cols, axis=1)


def _in_call(xt, mod4, layer, mod_base, groups, params, *, row_len):
    n, d = xt.shape
    t = min(IN_TILE, n // groups)
    tiles_per_group = n // groups // t
    assert tiles_per_group * t * groups == n
    cnt = jnp.asarray(_pool_counts(TOKEN_TILE, row_len))
    row = lambda i: (i, 0)
    outs = [jax.ShapeDtypeStruct((n, 2 * GLA_QK), F32), jax.ShapeDtypeStruct((n, W_GLA), F32),
            jax.ShapeDtypeStruct((n, 2 * GLA_QK), F32), jax.ShapeDtypeStruct((n, W_GLA), F32),
            jax.ShapeDtypeStruct((n, W_FFT), F32), jax.ShapeDtypeStruct((n, W_CONV + W_POOL), BF16)]
    g, w, wa2, ba2, cw, cb, wpool, pscale = params
    return pl.pallas_call(
        functools.partial(_in_kernel, row_len=row_len),
        grid=(n // t,),
        in_specs=[pl.BlockSpec((t, d), row), _mod_spec(mod4, layer, mod_base, tiles_per_group),
                  _layer_spec(g, layer), _layer_spec(w, layer), _layer_spec(wa2, layer), _layer_spec(ba2, layer),
                  _layer_spec(cw, layer), _layer_spec(cb, layer), _const_spec(cnt.shape),
                  _layer_spec(wpool, layer), _layer_spec(pscale, layer)],
        out_specs=[pl.BlockSpec((t, o.shape[1]), row) for o in outs],
        out_shape=outs,
        compiler_params=_params("arbitrary"),
        name="in_proj",
    )(xt, mod4, g, w, wa2, ba2, cw, cb, cnt, wpool, pscale)


def _gla_block(q, k, v, la, s_ref, consts, fwd):
    tri2, rexp, bdmask, hq, hv = consts
    t_rows = q.shape[0]
    c, sub = GLA_CHUNK, GLA_SUB
    nchunk = t_rows // c
    nt = (((1,), (1,)), ((), ()))
    tn = (((0,), (0,)), ((), ()))
    la_hi = la.astype(BF16)
    la_lo = (la - la_hi.astype(F32)).astype(BF16)
    ce = (jnp.dot(tri2, la_hi, preferred_element_type=F32)
          + jnp.dot(tri2, la_lo, preferred_element_type=F32)) * LOG2_E
    cum, end = ce[:t_rows], ce[t_rows:]
    qe = (q * jnp.exp2(cum)).astype(BF16)
    kd = (k * jnp.exp2(end - cum)).astype(BF16)
    v16 = v.astype(BF16)

    s = s_ref[0]
    o_parts = [None] * nchunk
    for ci in (range(nchunk) if fwd else reversed(range(nchunk))):
        rows = slice(ci * c, (ci + 1) * c)
        o_parts[ci] = lax.dot_general(qe[rows], s.astype(BF16), nt, preferred_element_type=F32)
        upd = lax.dot_general(v16[rows], kd[rows], tn, preferred_element_type=F32)
        s = s * jnp.exp2(end[ci * c:ci * c + 1]) + upd * bdmask
    s_ref[0] = s
    o = jnp.concatenate(o_parts, axis=0)

    half = t_rows // 2
    b = c // 2
    while b >= sub:
        npair = t_rows // (2 * b)
        refs = []
        for p in range(npair):
            r = 2 * b * p + (b - 1 if fwd else b)
            refs.append(jnp.broadcast_to(cum[r:r + 1], (2 * b, GLA_QK)))
        e = jnp.exp2(-jnp.abs(cum - jnp.concatenate(refs, axis=0)))
        qfull, kfull = q * e, k * e
        first = [slice(2 * b * p, 2 * b * p + b) for p in range(npair)]
        second = [slice(2 * b * p + b, 2 * b * p + 2 * b) for p in range(npair)]
        qrows, krows = (second, first) if fwd else (first, second)
        qsel = jnp.concatenate([qfull[r] for r in qrows], axis=0)
        ksel = jnp.concatenate([kfull[r] for r in krows], axis=0).astype(BF16)
        vsel = jnp.concatenate([v16[r] for r in krows], axis=0)
        qst = jnp.concatenate([qsel * hq[h] for h in range(GLA_HEADS)], axis=0).astype(BF16)
        att = lax.dot_general(qst, ksel, nt, preferred_element_type=F32)
        ri = lax.broadcasted_iota(jnp.int32, att.shape, 0)
        ci = lax.broadcasted_iota(jnp.int32, att.shape, 1)
        att = jnp.where(((ri & (half - 1)) ^ ci) < b, att, 0.0).astype(BF16)
        att_k = jnp.concatenate([att[h * half:(h + 1) * half] for h in range(GLA_HEADS)], axis=1)
        v_k = jnp.concatenate([vsel * hv[h].astype(BF16) for h in range(GLA_HEADS)], axis=0)
        res = jnp.dot(att_k, v_k, preferred_element_type=F32)
        zero = jnp.zeros((b, W_GLA), F32)
        pieces = []
        for p in range(npair):
            piece = res[p * b:(p + 1) * b]
            pieces += [zero, piece] if fwd else [piece, zero]
        o = o + jnp.concatenate(pieces, axis=0)
        b //= 2

    nblk = t_rows // sub

    def row_of_block(x, j):
        w = x.shape[1]
        xb = x.reshape(nblk, sub, w)[:, j:j + 1, :]
        return jnp.broadcast_to(xb, (nblk, sub, w)).reshape(t_rows, w)

    ii = lax.broadcasted_iota(jnp.int32, (t_rows, GLA_QK), 0) & (sub - 1)
    es = []
    for j in range(sub):
        keep = (ii >= j) if fwd else (ii <= j)
        w = jnp.exp2(jnp.where(keep, cum - row_of_block(cum, j), NEG_BIG))
        es.append((w * (q * row_of_block(k, j))).astype(BF16))
    pr = jnp.dot(jnp.concatenate(es, axis=0), rexp, preferred_element_type=F32)
    for j in range(sub):
        o = o + pr[t_rows * j:t_rows * (j + 1)] * row_of_block(v, j)
    return o


def _gla_kernel(kqf_ref, vf_ref, laf_ref, kqb_ref, vb_ref, lab_ref, h0f_ref, h0b_ref,
                trif_ref, trib_ref, rexp_ref, bd_ref,
                of_ref, ob_ref, sf_ref, sb_ref):
    @pl.when(pl.program_id(1) == 0)
    def _():
        sf_ref[...] = h0f_ref[...]
        sb_ref[...] = h0b_ref[...]

    lq = lax.broadcasted_iota(jnp.int32, (1, GLA_QK), 1)
    lv = lax.broadcasted_iota(jnp.int32, (1, W_GLA), 1)
    hq = [jnp.where((lq >= h * GLA_DK) & (lq < (h + 1) * GLA_DK), 1.0, 0.0) for h in range(GLA_HEADS)]
    hv = [jnp.where((lv >= h * GLA_DV) & (lv < (h + 1) * GLA_DV), 1.0, 0.0) for h in range(GLA_HEADS)]
    rexp, bdmask = rexp_ref[...], bd_ref[...]
    of_ref[...] = _gla_block(kqf_ref[:, GLA_QK:], kqf_ref[:, :GLA_QK], vf_ref[...], laf_ref[...], sf_ref,
                             (trif_ref[...], rexp, bdmask, hq, hv), True)
    ob_ref[...] = _gla_block(kqb_ref[:, GLA_QK:], kqb_ref[:, :GLA_QK], vb_ref[...], lab_ref[...], sb_ref,
                             (trib_ref[...], rexp, bdmask, hq, hv), False)


def _gla_consts(t):
    c = GLA_CHUNK
    i = np.arange(t)
    same = (i[None, :] // c) == (i[:, None] // c)
    trif = np.concatenate([same & (i[None, :] <= i[:, None]), same], axis=0).astype(np.float32)
    trib = np.concatenate([same & (i[None, :] >= i[:, None]), same], axis=0).astype(np.float32)
    hk = np.arange(GLA_QK) // GLA_DK
    hd = np.arange(W_GLA) // GLA_DV
    rexp = (hk[:, None] == hd[None, :]).astype(np.float32)
    bd = (hd[:, None] == hk[None, :]).astype(np.float32)
    return (jnp.asarray(trif, BF16), jnp.asarray(trib, BF16), jnp.asarray(rexp, BF16), jnp.asarray(bd, F32))


def _gla_call(kq, v, la, h0f, h0b, *, batch):
    n = kq.shape[0]
    tg = min(GLA_BLOCK, n // batch)
    nb = n // batch // tg
    trif, trib, rexp, bd = _gla_consts(tg)
    fw = lambda b, i: (b * nb + i, 0)
    bw = lambda b, i: (b * nb + nb - 1 - i, 0)
    bw1 = lambda b, i: (b * nb + nb - 1 - i, 1)
    st = lambda b, i: (b, 0, 0)
    sshape = jax.ShapeDtypeStruct((batch, W_GLA, GLA_QK), F32)
    return pl.pallas_call(
        _gla_kernel,
        grid=(batch, nb),
        in_specs=[pl.BlockSpec((tg, 2 * GLA_QK), fw), pl.BlockSpec((tg, W_GLA), fw), pl.BlockSpec((tg, GLA_QK), fw),
                  pl.BlockSpec((tg, 2 * GLA_QK), bw), pl.BlockSpec((tg, W_GLA), bw), pl.BlockSpec((tg, GLA_QK), bw1),
                  pl.BlockSpec((1, W_GLA, GLA_QK), st), pl.BlockSpec((1, W_GLA, GLA_QK), st),
                  _const_spec(trif.shape), _const_spec(trib.shape), _const_spec(rexp.shape), _const_spec(bd.shape)],
        out_specs=[pl.BlockSpec((tg, W_GLA), fw), pl.BlockSpec((tg, W_GLA), bw),
                   pl.BlockSpec((1, W_GLA, GLA_QK), st), pl.BlockSpec((1, W_GLA, GLA_QK), st)],
        out_shape=[jax.ShapeDtypeStruct((n, W_GLA), F32), jax.ShapeDtypeStruct((n, W_GLA), F32), sshape, sshape],
        compiler_params=_params("arbitrary", "arbitrary"),
        name="gla",
    )(kq, v, la, kq, v, la, h0f, h0b, trif, trib, rexp, bd)


def _fft_tail(ab, cs_ref, wf_ref, norm):
    f = jnp.dot(ab, cs_ref[...].astype(BF16), preferred_element_type=F32) * norm
    return jnp.dot(f.astype(BF16), wf_ref[...], preferred_element_type=F32)


def _fft1_kernel(w_ref, u_ref, y_ref):
    w = w_ref[...].astype(BF16)
    for j in range(u_ref.shape[1]):
        y_ref[:, j, :] = jnp.dot(w, u_ref[:, j, :].astype(BF16), preferred_element_type=F32)


def _fft2_kernel(m_ref, y_ref, cs_ref, wf_ref, o_ref, *, norm):
    group, n2 = y_ref.shape[1], y_ref.shape[2]
    parts = []
    for g in range(group):
        ys = jnp.concatenate([y_ref[0, g], y_ref[1, g]], axis=0).astype(BF16)
        z = jnp.dot(m_ref[g].astype(BF16), ys, preferred_element_type=F32)
        parts.append(jnp.concatenate([z[:n2], z[n2:]], axis=1).astype(BF16))
    out = _fft_tail(jnp.concatenate(parts, axis=0), cs_ref, wf_ref, norm)
    for g in range(group):
        o_ref[:, g, :] = out[g * n2:(g + 1) * n2]


def _fft_direct_kernel(m_ref, u_ref, cs_ref, wf_ref, o_ref, *, norm):
    n = u_ref.shape[0]
    z = jnp.dot(m_ref[...].astype(BF16), u_ref[...].astype(BF16), preferred_element_type=F32)
    o_ref[...] = _fft_tail(jnp.concatenate([z[:n], z[n:]], axis=1).astype(BF16), cs_ref, wf_ref, norm)


def _dft_cos_sin(n):
    k = np.arange(n, dtype=np.int64)
    ang = 2.0 * np.pi * ((k[:, None] * k[None, :]) % n).astype(np.float64) / n
    return np.cos(ang), np.sin(ang)


def _channel_dft():
    c, s = _dft_cos_sin(FFT_DG)
    eye = np.eye(FFT_GROUPS)
    return jnp.asarray(np.concatenate([np.kron(eye, c), np.kron(eye, s)], axis=0), F32)


def _fft_latent_call(u, wf_bd, layer, *, batch):
    n = u.shape[0] // batch
    n1, n2 = n // FFT_N2, FFT_N2
    c1, s1 = _dft_cos_sin(n1)
    w1 = jnp.asarray(np.concatenate([c1, -s1], axis=0), F32)
    k1 = np.arange(n1, dtype=np.int64)[:, None, None]
    k2 = np.arange(n2, dtype=np.int64)[None, :, None]
    m2 = np.arange(n2, dtype=np.int64)[None, None, :]
    ang = 2.0 * np.pi * ((m2 * (k1 + n1 * k2)) % n).astype(np.float64) / n
    cm, sm = np.cos(ang), np.sin(ang)
    mats = jnp.asarray(np.concatenate([np.concatenate([cm, sm], axis=2),
                                       np.concatenate([-sm, cm], axis=2)], axis=1), F32)
    cs = _channel_dft()
    norm = float(1.0 / np.sqrt(n * FFT_DG))
    g1 = FFT_STAGE1_GROUP
    y = pl.pallas_call(
        _fft1_kernel,
        grid=(batch, n2 // g1),
        in_specs=[_const_spec(w1.shape), pl.BlockSpec((None, n1, g1, W_FFT), lambda b, j: (b, 0, j, 0))],
        out_specs=pl.BlockSpec((None, 2 * n1, g1, W_FFT), lambda b, j: (b, 0, j, 0)),
        out_shape=jax.ShapeDtypeStruct((batch, 2 * n1, n2, W_FFT), F32),
        compiler_params=_params("arbitrary", "arbitrary"),
        name="fft_stage1",
    )(w1, u.reshape(batch, n1, n2, W_FFT))
    g2 = min(n1, FFT_STAGE2_GROUP)
    out = pl.pallas_call(
        functools.partial(_fft2_kernel, norm=norm),
        grid=(n1 // g2, batch),
        in_specs=[pl.BlockSpec((g2, 2 * n2, 2 * n2), lambda j, b: (j, 0, 0)),
                  pl.BlockSpec((None, 2, g2, n2, W_FFT), lambda j, b: (b, 0, j, 0, 0)),
                  _const_spec(cs.shape), _layer_spec(wf_bd, layer)],
        out_specs=pl.BlockSpec((None, n2, g2, W_FFT), lambda j, b: (b, 0, j, 0)),
        out_shape=jax.ShapeDtypeStruct((batch, n2, n1, W_FFT), F32),
        compiler_params=_params("arbitrary", "arbitrary"),
        name="fft_stage2",
    )(mats, y.reshape(batch, 2, n1, n2, W_FFT), cs, wf_bd)
    return out.reshape(batch * n, W_FFT)


def _fft_direct_call(u, wf_bd, layer, *, batch):
    n = u.shape[0] // batch
    c, s = _dft_cos_sin(n)
    m = jnp.asarray(np.concatenate([c, -s], axis=0), F32)
    cs = _channel_dft()
    norm = float(1.0 / np.sqrt(n * FFT_DG))
    return pl.pallas_call(
        functools.partial(_fft_direct_kernel, norm=norm),
        grid=(batch,),
        in_specs=[_const_spec(m.shape), pl.BlockSpec((n, W_FFT), lambda b: (b, 0)),
                  _const_spec(cs.shape), _layer_spec(wf_bd, layer)],
        out_specs=pl.BlockSpec((n, W_FFT), lambda b: (b, 0)),
        out_shape=jax.ShapeDtypeStruct((batch * n, W_FFT), F32),
        compiler_params=_params("arbitrary"),
        name="fft_direct",
    )(m, u, cs, wf_bd)


def _out_kernel(x_ref, mod_ref, of_ref, ob_ref, gate_ref, fft_ref, ycp_ref, glag_ref, ones_ref, wout_ref,
                n2g_ref, wup_ref, cw_ref, cb_ref, wdn_ref, fing_ref, o_ref,
                x1_ref, h2_ref, au_ref, h_ref, acc_ref, *, row_len, final):
    t_rows = x_ref.shape[0]
    d_ff = wdn_ref.shape[0]
    nchunk = d_ff // FFN_CHUNK
    o = of_ref[...] + ob_ref[...]
    osq = o * o
    hi = osq.astype(BF16)
    lo = (osq - hi.astype(F32)).astype(BF16)
    ms = (jnp.dot(hi, ones_ref[...], preferred_element_type=F32)
          + jnp.dot(lo, ones_ref[...], preferred_element_type=F32)) * (1.0 / GLA_DV)
    gl = (o * lax.rsqrt(ms + EPS)) * glag_ref[...] * _silu(gate_ref[...])
    ymix = jnp.concatenate([gl.astype(BF16), fft_ref[...].astype(BF16), ycp_ref[...]], axis=1)
    x1 = x_ref[...] + mod_ref[2:3, :] * jnp.dot(ymix, wout_ref[...], preferred_element_type=F32)
    x1_ref[...] = x1
    h2_ref[...] = _rms_mod(x1, n2g_ref[...], mod_ref[4:5, :], mod_ref[3:4, :]).astype(BF16)
    acc_ref[...] = jnp.zeros(acc_ref.shape, F32)

    def up(c, slot):
        for half in range(2):
            cols = slice(half * d_ff + c * FFN_CHUNK, half * d_ff + (c + 1) * FFN_CHUNK)
            au_ref[slot, :, half * FFN_CHUNK:(half + 1) * FFN_CHUNK] = jnp.dot(
                h2_ref[...], wup_ref[:, cols], preferred_element_type=F32)

    def elem(c, slot):
        cols = slice(c * FFN_CHUNK, (c + 1) * FFN_CHUNK)
        pos = lax.broadcasted_iota(jnp.int32, (t_rows, FFN_CHUNK), 0) & (row_len - 1)
        a = au_ref[slot, :, :FFN_CHUNK]
        cw = cw_ref[:, cols]
        a = (jnp.where(pos == 0, 0.0, pltpu.roll(a, 1, 0)) * cw[0:1] + a * cw[1:2]
             + jnp.where(pos == row_len - 1, 0.0, pltpu.roll(a, t_rows - 1, 0)) * cw[2:3] + cb_ref[:, cols])
        h_ref[slot] = (_silu(a) * au_ref[slot, :, FFN_CHUNK:]).astype(BF16)

    def down(c, slot):
        rows = slice(c * FFN_CHUNK, (c + 1) * FFN_CHUNK)
        acc_ref[...] += jnp.dot(h_ref[slot], wdn_ref[rows, :], preferred_element_type=F32)

    def stage(c):
        up(c + 1, 0)
        elem(c, 1)
        down(c - 1, 0)
        up(c + 2, 1)
        elem(c + 1, 0)
        down(c, 1)

    assert nchunk % 2 == 1 and nchunk >= 3
    up(0, 0)
    up(1, 1)
    elem(0, 0)

    for i in range((nchunk - 3) // 2):
        stage(2 * i + 1)
    c = nchunk - 2
    up(c + 1, 0)
    elem(c, 1)
    down(c - 1, 0)
    elem(c + 1, 0)
    down(c, 1)
    down(c + 1, 0)
    x2 = x1_ref[...] + mod_ref[5:6, :] * acc_ref[...]
    if final:
        ms2 = jnp.mean(x2 * x2, axis=-1, keepdims=True)
        x2 = (x2 * lax.rsqrt(ms2 + EPS)) * fing_ref[...]
    o_ref[...] = x2


def _out_call(xt, mod4, layer, mod_base, groups, o_f, o_b, gate, yfft, ycp, params, fing, *, row_len, final):
    n, d = xt.shape
    t = OUT_TILE
    tiles_per_group = n // groups // t
    assert tiles_per_group * t * groups == n
    hd = np.arange(W_GLA) // GLA_DV
    ones = jnp.asarray((hd[:, None] == hd[None, :]).astype(np.float32), BF16)
    row = lambda i: (i, 0)
    acts = [xt, o_f, o_b, gate, yfft, ycp]
    glag, wout, n2g, wup, cw, cb, wdn = params
    consts = [glag, ones, wout, n2g, wup, cw, cb, wdn, fing]
    return pl.pallas_call(
        functools.partial(_out_kernel, row_len=row_len, final=final),
        grid=(n // t,),
        in_specs=[pl.BlockSpec((t, d), row), _mod_spec(mod4, layer, mod_base, tiles_per_group)]
                 + [pl.BlockSpec((t, a.shape[1]), row) for a in acts[1:]]
                 + [_const_spec(a.shape) if a is ones or a is fing else _layer_spec(a, layer) for a in consts],
        out_specs=pl.BlockSpec((t, d), row),
        out_shape=jax.ShapeDtypeStruct((n, d), F32),
        scratch_shapes=[pltpu.VMEM((t, d), F32), pltpu.VMEM((t, d), BF16),
                        pltpu.VMEM((2, t, 2 * FFN_CHUNK), F32), pltpu.VMEM((2, t, FFN_CHUNK), BF16),
                        pltpu.VMEM((t, d), F32)],
        compiler_params=_params("arbitrary"),
        name="out_ffn",
    )(xt, mod4, *acts[1:], *consts)


def _block_diag(w):
    dep, g, a, b = w.shape
    eye = jnp.eye(g, dtype=w.dtype)
    return (eye[None, :, None, :, None] * w[:, :, :, None, :]).reshape(dep, g * a, g * b)


def _prep_params(norm1_g, norm2_g, w_in, gla_w_a2, gla_b_a2, gla_norm_g, fft_w, conv_w, conv_b, pool_w,
                 pool_scale, w_out, ffn_w_up, ffn_conv_w, ffn_conv_b, ffn_w_down):
    dep, d, _ = w_in.shape
    a0, a1 = COL_STARTS[2], COL_STARTS[4]
    w_in_r = jnp.concatenate([w_in[:, :, :a0], w_in[:, :, a1:], w_in[:, :, a0:a1],
                              jnp.zeros((dep, d, LANES - 2 * GLA_RANK), w_in.dtype)], axis=2).astype(BF16)
    wa2 = jnp.zeros((dep, LANES, 2 * GLA_QK), F32)
    wa2 = wa2.at[:, 0:GLA_RANK, 0:GLA_QK].set(gla_w_a2[:, 0])
    wa2 = wa2.at[:, GLA_RANK:2 * GLA_RANK, GLA_QK:].set(gla_w_a2[:, 1])
    in_params = (norm1_g[:, None, :], w_in_r, wa2.astype(BF16), gla_b_a2.reshape(dep, 1, 2 * GLA_QK),
                 conv_w, conv_b[:, None, :], _block_diag(pool_w).astype(BF16), pool_scale[:, None, :])
    out_params = (jnp.tile(gla_norm_g, (1, GLA_HEADS))[:, None, :], w_out.astype(BF16), norm2_g[:, None, :],
                  ffn_w_up.astype(BF16), ffn_conv_w, ffn_conv_b[:, None, :], ffn_w_down.astype(BF16))
    return in_params, out_params, _block_diag(fft_w).astype(BF16)


def kernel(x, c, ctx, c_ctx, norm1_g, norm2_g, w_mod, b_mod, w_in, gla_w_a2, gla_b_a2, gla_norm_g,
           fft_w, conv_w, conv_b, pool_w, pool_scale, w_out, ffn_w_up, ffn_conv_w, ffn_conv_b,
           ffn_w_down, final_norm_g):
    batch, seq, d = x.shape
    ctx_len = ctx.shape[1]
    depth = w_mod.shape[0]
    assert seq % GLA_BLOCK == 0 and seq % FFT_N2 == 0 and ctx_len == TOKEN_TILE

    cc = jnp.concatenate([c, c_ctx[None, :], jnp.zeros((MOD_ROWS - batch - 1, d), F32)], axis=0)
    mod4 = _mod_call(cc, w_mod, b_mod).reshape(depth, MOD_ROWS, 6, d)
    in_params, out_params, wf_bd = _prep_params(
        norm1_g, norm2_g, w_in, gla_w_a2, gla_b_a2, gla_norm_g, fft_w, conv_w, conv_b, pool_w, pool_scale,
        w_out, ffn_w_up, ffn_conv_w, ffn_conv_b, ffn_w_down)

    xt = x.reshape(batch * seq, d)
    xc = ctx.reshape(batch * ctx_len, d)
    zero_state = jnp.zeros((batch, W_GLA, GLA_QK), F32)
    fing = final_norm_g.reshape(1, d)
    for i in range(depth):
        last = i == depth - 1
        kq, v, la, gate, ufft, ycp = _in_call(xc, mod4, i, batch, 1, in_params, row_len=ctx_len)
        o_f, o_b, s_f, s_b = _gla_call(kq, v, la, zero_state, zero_state, batch=batch)
        if not last:
            yfft = _fft_direct_call(ufft, wf_bd, i, batch=batch)
            xc = _out_call(xc, mod4, i, batch, 1, o_f, o_b, gate, yfft, ycp, out_params, fing,
                           row_len=ctx_len, final=False)

        kq, v, la, gate, ufft, ycp = _in_call(xt, mod4, i, 0, batch, in_params, row_len=GRID_W)
        o_f, o_b, _, _ = _gla_call(kq, v, la, s_f, s_b, batch=batch)
        yfft = _fft_latent_call(ufft, wf_bd, i, batch=batch)
        xt = _out_call(xt, mod4, i, 0, batch, o_f, o_b, gate, yfft, ycp, out_params, fing,
                       row_len=GRID_W, final=last)
    return xt.reshape(batch, seq, d)
```

```python
import functools

import numpy as np
import jax
import jax.numpy as jnp
from jax import lax
from jax.experimental import pallas as pl
from jax.experimental.pallas import tpu as pltpu

F32 = jnp.float32
BF16 = jnp.bfloat16

GRID_W = 64
EPS = 1e-6
GLA_HEADS = 4
GLA_DK = 32
GLA_DV = 64
GLA_QK = GLA_HEADS * GLA_DK
W_GLA = GLA_HEADS * GLA_DV
GLA_RANK = 16
GLA_TAU = 16.0
FFT_GROUPS = 4
FFT_DG = 64
W_FFT = FFT_GROUPS * FFT_DG
W_CONV = 256
POOL_WINDOWS = (2, 4, 8, 16)
POOL_DG = 64
W_POOL = len(POOL_WINDOWS) * POOL_DG
COL_SIZES = (GLA_QK, W_GLA, GLA_RANK, GLA_RANK, GLA_QK, W_GLA, W_FFT, W_CONV, W_CONV, W_CONV, W_POOL)
COL_STARTS = tuple(int(s) for s in np.cumsum((0,) + COL_SIZES)[:-1])

LANES = 128
VMEM_LIMIT_BYTES = 56 * 1024 * 1024

TOKEN_TILE = 256
IN_TILE = 1024
OUT_TILE = 512
GLA_CHUNK = 64
GLA_SUB = 8
GLA_BLOCK = 256
FFN_CHUNK = 256
FFT_N2 = 128
FFT_STAGE1_GROUP = 16
FFT_STAGE2_GROUP = 8
NEG_BIG = -1e30
LOG2_E = 1.4426950408889634

IN_K, IN_V, IN_Q, IN_G, IN_FFT, IN_H, IN_BG, IN_CG, IN_POOL, IN_A = (
    0, 128, 384, 512, 768, 1024, 1280, 1536, 1792, 2048)
IN_COLS = IN_A + LANES
MOD_ROWS = 8


def _const_spec(shape):
    nd = len(shape)
    return pl.BlockSpec(shape, lambda *_: (0,) * nd, pipeline_mode=pl.Buffered(1))


def _layer_spec(arr, layer):
    nd = arr.ndim
    return pl.BlockSpec((None,) + arr.shape[1:], lambda *_: (layer,) + (0,) * (nd - 1),
                        pipeline_mode=pl.Buffered(1))


def _mod_spec(mod4, layer, base, tiles_per_group):
    return pl.BlockSpec((None, None) + mod4.shape[2:], lambda i: (layer, base + i // tiles_per_group, 0, 0))


def _params(*sem):
    return pltpu.CompilerParams(dimension_semantics=sem, vmem_limit_bytes=VMEM_LIMIT_BYTES)


def _rms_mod(x, g, scale, shift):
    ms = jnp.mean(x * x, axis=-1, keepdims=True)
    return (x * lax.rsqrt(ms + EPS)) * g * (1.0 + scale) + shift


def _silu(a):
    return a * jax.nn.sigmoid(a)


def _mod_kernel(c_ref, w_ref, b_ref, o_ref):
    s = _silu(c_ref[...]).astype(BF16)
    o_ref[0] = jnp.dot(s, w_ref[0].astype(BF16), preferred_element_type=F32) + b_ref[0]


def _mod_call(cc, w_mod, b_mod):
    depth, d, n = w_mod.shape
    tn = 1536
    return pl.pallas_call(
        _mod_kernel,
        grid=(depth, n // tn),
        in_specs=[pl.BlockSpec(cc.shape, lambda i, j: (0, 0)),
                  pl.BlockSpec((1, d, tn), lambda i, j: (i, 0, j)),
                  pl.BlockSpec((1, 1, tn), lambda i, j: (i, 0, j))],
        out_specs=pl.BlockSpec((1, cc.shape[0], tn), lambda i, j: (i, 0, j)),
        out_shape=jax.ShapeDtypeStruct((depth, cc.shape[0], n), F32),
        compiler_params=_params("arbitrary", "arbitrary"),
        name="modulation",
    )(cc, w_mod, b_mod.reshape(depth, 1, n))


def _in_kernel(x_ref, mod_ref, g_ref, w_ref, wa2_ref, ba2_ref, cw_ref, cb_ref, cnt_ref, wpool_ref, pscale_ref,
               kq_ref, v_ref, la_ref, gate_ref, fft_ref, ycp_ref, *, row_len):
    nsub = x_ref.shape[0] // TOKEN_TILE
    ps = []
    for s in range(nsub):
        rows = slice(s * TOKEN_TILE, (s + 1) * TOKEN_TILE)
        h = _rms_mod(x_ref[rows, :], g_ref[...], mod_ref[1:2, :], mod_ref[0:1, :]).astype(BF16)
        ps.append(jnp.dot(h, w_ref[...], preferred_element_type=F32))
    for s in range(nsub):
        rows = slice(s * TOKEN_TILE, (s + 1) * TOKEN_TILE)
        _in_mixers(ps[s], rows, wa2_ref, ba2_ref, cw_ref, cb_ref, cnt_ref, wpool_ref, pscale_ref,
                   kq_ref, v_ref, la_ref, gate_ref, fft_ref, ycp_ref, row_len)


def _in_mixers(p, rows, wa2_ref, ba2_ref, cw_ref, cb_ref, cnt_ref, wpool_ref, pscale_ref,
               kq_ref, v_ref, la_ref, gate_ref, fft_ref, ycp_ref, row_len):
    t_rows = p.shape[0]
    kq_ref[rows, :GLA_QK] = p[:, IN_K:IN_K + GLA_QK]
    kq_ref[rows, GLA_QK:] = p[:, IN_Q:IN_Q + GLA_QK] * (GLA_DK ** -0.5)
    v_ref[rows, :] = p[:, IN_V:IN_V + W_GLA]
    gate_ref[rows, :] = p[:, IN_G:IN_G + W_GLA]
    fft_ref[rows, :] = p[:, IN_FFT:IN_FFT + W_FFT]
    z = jnp.dot(p[:, IN_A:IN_A + LANES].astype(BF16), wa2_ref[...], preferred_element_type=F32) + ba2_ref[...]
    la_ref[rows, :] = (jnp.minimum(z, 0.0) - jnp.log(1.0 + jnp.exp(-jnp.abs(z)))) * (1.0 / GLA_TAU)

    def pos(a):
        return lax.broadcasted_iota(jnp.int32, a.shape, 0) & (row_len - 1)

    def prev(a, s):
        return jnp.where(pos(a) >= s, pltpu.roll(a, s, 0), 0.0)

    def nxt(a, s):
        return jnp.where(pos(a) < row_len - s, pltpu.roll(a, t_rows - s, 0), 0.0)

    t = p[:, IN_CG:IN_CG + W_CONV] * p[:, IN_H:IN_H + W_CONV]
    cw = cw_ref[...]
    conv = prev(t, 1) * cw[0:1] + t * cw[1:2] + nxt(t, 1) * cw[2:3] + cb_ref[...]
    ycp_ref[rows, :W_CONV] = (p[:, IN_BG:IN_BG + W_CONV] * conv).astype(BF16)

    u = p[:, IN_POOL:IN_POOL + W_POOL]
    halves = []
    for side, steps in ((0, 1), (1, 3)):
        f = u[:, side * LANES:(side + 1) * LANES]
        g = prev(f, 1)
        sums = [g + f]
        for i in range(steps):
            f = f + nxt(f, 1 << i)
            g = g + prev(g, 1 << i)
            sums.append(g + f)
        lane = lax.broadcasted_iota(jnp.int32, f.shape, 1)
        halves.append(jnp.where(lane < POOL_DG, sums[-2], sums[-1]))
    tot = jnp.concatenate(halves, axis=1)
    pooled = tot / cnt_ref[...] - u
    yp = jnp.dot(pooled.astype(BF16), wpool_ref[...], preferred_element_type=F32) * pscale_ref[...]
    ycp_ref[rows, W_CONV:] = yp.astype(BF16)


def _pool_counts(t_rows, row_len):
    pos = np.arange(t_rows) % row_len
    cols = []
    for w in POOL_WINDOWS:
        lo = np.clip(pos - w // 2, 0, row_len - 1)
        hi = np.clip(pos + w // 2 - 1, 0, row_len - 1)
        cols.append(np.repeat((hi - lo + 1).astype(np.float32)[:, None], POOL_DG, axis=1))
    return np.concatenate(cols, axis=1)


def _in_call(xt, mod4, layer, mod_base, groups, params, *, row_len):
    n, d = xt.shape
    t = min(IN_TILE, n // groups)
    tiles_per_group = n // groups // t
    assert tiles_per_group * t * groups == n
    cnt = jnp.asarray(_pool_counts(TOKEN_TILE, row_len))
    row = lambda i: (i, 0)
    outs = [jax.ShapeDtypeStruct((n, 2 * GLA_QK), F32), jax.ShapeDtypeStruct((n, W_GLA), F32),
            jax.ShapeDtypeStruct((n, 2 * GLA_QK), F32), jax.ShapeDtypeStruct((n, W_GLA), F32),
            jax.ShapeDtypeStruct((n, W_FFT), F32), jax.ShapeDtypeStruct((n, W_CONV + W_POOL), BF16)]
    g, w, wa2, ba2, cw, cb, wpool, pscale = params
    return pl.pallas_call(
        functools.partial(_in_kernel, row_len=row_len),
        grid=(n // t,),
        in_specs=[pl.BlockSpec((t, d), row), _mod_spec(mod4, layer, mod_base, tiles_per_group),
                  _layer_spec(g, layer), _layer_spec(w, layer), _layer_spec(wa2, layer), _layer_spec(ba2, layer),
                  _layer_spec(cw, layer), _layer_spec(cb, layer), _const_spec(cnt.shape),
                  _layer_spec(wpool, layer), _layer_spec(pscale, layer)],
        out_specs=[pl.BlockSpec((t, o.shape[1]), row) for o in outs],
        out_shape=outs,
        compiler_params=_params("arbitrary"),
        name="in_proj",
    )(xt, mod4, g, w, wa2, ba2, cw, cb, cnt, wpool, pscale)


def _gla_block(q, k, v, la, s_ref, consts, fwd):
    tri2, rexp, bdmask, hq, hv = consts
    t_rows = q.shape[0]
    c, sub = GLA_CHUNK, GLA_SUB
    nchunk = t_rows // c
    nt = (((1,), (1,)), ((), ()))
    tn = (((0,), (0,)), ((), ()))
    la_hi = la.astype(BF16)
    la_lo = (la - la_hi.astype(F32)).astype(BF16)
    ce = (jnp.dot(tri2, la_hi, preferred_element_type=F32)
          + jnp.dot(tri2, la_lo, preferred_element_type=F32)) * LOG2_E
    cum, end = ce[:t_rows], ce[t_rows:]
    qe = (q * jnp.exp2(cum)).astype(BF16)
    kd = (k * jnp.exp2(end - cum)).astype(BF16)
    v16 = v.astype(BF16)

    s = s_ref[0]
    o_parts = [None] * nchunk
    for ci in (range(nchunk) if fwd else reversed(range(nchunk))):
        rows = slice(ci * c, (ci + 1) * c)
        o_parts[ci] = lax.dot_general(qe[rows], s.astype(BF16), nt, preferred_element_type=F32)
        upd = lax.dot_general(v16[rows], kd[rows], tn, preferred_element_type=F32)
        s = s * jnp.exp2(end[ci * c:ci * c + 1]) + upd * bdmask
    s_ref[0] = s
    o = jnp.concatenate(o_parts, axis=0)

    half = t_rows // 2
    b = c // 2
    while b >= sub:
        npair = t_rows // (2 * b)
        refs = []
        for p in range(npair):
            r = 2 * b * p + (b - 1 if fwd else b)
            refs.append(jnp.broadcast_to(cum[r:r + 1], (2 * b, GLA_QK)))
        e = jnp.exp2(-jnp.abs(cum - jnp.concatenate(refs, axis=0)))
        qfull, kfull = q * e, k * e
        first = [slice(2 * b * p, 2 * b * p + b) for p in range(npair)]
        second = [slice(2 * b * p + b, 2 * b * p + 2 * b) for p in range(npair)]
        qrows, krows = (second, first) if fwd else (first, second)
        qsel = jnp.concatenate([qfull[r] for r in qrows], axis=0)
        ksel = jnp.concatenate([kfull[r] for r in krows], axis=0).astype(BF16)
        vsel = jnp.concatenate([v16[r] for r in krows], axis=0)
        qst = jnp.concatenate([qsel * hq[h] for h in range(GLA_HEADS)], axis=0).astype(BF16)
        att = lax.dot_general(qst, ksel, nt, preferred_element_type=F32)
        ri = lax.broadcasted_iota(jnp.int32, att.shape, 0)
        ci = lax.broadcasted_iota(jnp.int32, att.shape, 1)
        att = jnp.where(((ri & (half - 1)) ^ ci) < b, att, 0.0).astype(BF16)
        att_k = jnp.concatenate([att[h * half:(h + 1) * half] for h in range(GLA_HEADS)], axis=1)
        v_k = jnp.concatenate([vsel * hv[h].astype(BF16) for h in range(GLA_HEADS)], axis=0)
        res = jnp.dot(att_k, v_k, preferred_element_type=F32)
        zero = jnp.zeros((b, W_GLA), F32)
        pieces = []
        for p in range(npair):
            piece = res[p * b:(p + 1) * b]
            pieces += [zero, piece] if fwd else [piece, zero]
        o = o + jnp.concatenate(pieces, axis=0)
        b //= 2

    nblk = t_rows // sub

    def row_of_block(x, j):
        w = x.shape[1]
        xb = x.reshape(nblk, sub, w)[:, j:j + 1, :]
        return jnp.broadcast_to(xb, (nblk, sub, w)).reshape(t_rows, w)

    ii = lax.broadcasted_iota(jnp.int32, (t_rows, GLA_QK), 0) & (sub - 1)
    es = []
    for j in range(sub):
        keep = (ii >= j) if fwd else (ii <= j)
        w = jnp.exp2(jnp.where(keep, cum - row_of_block(cum, j), NEG_BIG))
        es.append((w * (q * row_of_block(k, j))).astype(BF16))
    pr = jnp.dot(jnp.concatenate(es, axis=0), rexp, preferred_element_type=F32)
    for j in range(sub):
        o = o + pr[t_rows * j:t_rows * (j + 1)] * row_of_block(v, j)
    return o


def _gla_kernel(kqf_ref, vf_ref, laf_ref, kqb_ref, vb_ref, lab_ref, h0f_ref, h0b_ref,
                trif_ref, trib_ref, rexp_ref, bd_ref,
                of_ref, ob_ref, sf_ref, sb_ref):
    @pl.when(pl.program_id(1) == 0)
    def _():
        sf_ref[...] = h0f_ref[...]
        sb_ref[...] = h0b_ref[...]

    lq = lax.broadcasted_iota(jnp.int32, (1, GLA_QK), 1)
    lv = lax.broadcasted_iota(jnp.int32, (1, W_GLA), 1)
    hq = [jnp.where((lq >= h * GLA_DK) & (lq < (h + 1) * GLA_DK), 1.0, 0.0) for h in range(GLA_HEADS)]
    hv = [jnp.where((lv >= h * GLA_DV) & (lv < (h + 1) * GLA_DV), 1.0, 0.0) for h in range(GLA_HEADS)]
    rexp, bdmask = rexp_ref[...], bd_ref[...]
    of_ref[...] = _gla_block(kqf_ref[:, GLA_QK:], kqf_ref[:, :GLA_QK], vf_ref[...], laf_ref[...], sf_ref,
                             (trif_ref[...], rexp, bdmask, hq, hv), True)
    ob_ref[...] = _gla_block(kqb_ref[:, GLA_QK:], kqb_ref[:, :GLA_QK], vb_ref[...], lab_ref[...], sb_ref,
                             (trib_ref[...], rexp, bdmask, hq, hv), False)


def _gla_consts(t):
    c = GLA_CHUNK
    i = np.arange(t)
    same = (i[None, :] // c) == (i[:, None] // c)
    trif = np.concatenate([same & (i[None, :] <= i[:, None]), same], axis=0).astype(np.float32)
    trib = np.concatenate([same & (i[None, :] >= i[:, None]), same], axis=0).astype(np.float32)
    hk = np.arange(GLA_QK) // GLA_DK
    hd = np.arange(W_GLA) // GLA_DV
    rexp = (hk[:, None] == hd[None, :]).astype(np.float32)
    bd = (hd[:, None] == hk[None, :]).astype(np.float32)
    return (jnp.asarray(trif, BF16), jnp.asarray(trib, BF16), jnp.asarray(rexp, BF16), jnp.asarray(bd, F32))


def _gla_call(kq, v, la, h0f, h0b, *, batch):
    n = kq.shape[0]
    tg = min(GLA_BLOCK, n // batch)
    nb = n // batch // tg
    trif, trib, rexp, bd = _gla_consts(tg)
    fw = lambda b, i: (b * nb + i, 0)
    bw = lambda b, i: (b * nb + nb - 1 - i, 0)
    bw1 = lambda b, i: (b * nb + nb - 1 - i, 1)
    st = lambda b, i: (b, 0, 0)
    sshape = jax.ShapeDtypeStruct((batch, W_GLA, GLA_QK), F32)
    return pl.pallas_call(
        _gla_kernel,
        grid=(batch, nb),
        in_specs=[pl.BlockSpec((tg, 2 * GLA_QK), fw), pl.BlockSpec((tg, W_GLA), fw), pl.BlockSpec((tg, GLA_QK), fw),
                  pl.BlockSpec((tg, 2 * GLA_QK), bw), pl.BlockSpec((tg, W_GLA), bw), pl.BlockSpec((tg, GLA_QK), bw1),
                  pl.BlockSpec((1, W_GLA, GLA_QK), st), pl.BlockSpec((1, W_GLA, GLA_QK), st),
                  _const_spec(trif.shape), _const_spec(trib.shape), _const_spec(rexp.shape), _const_spec(bd.shape)],
        out_specs=[pl.BlockSpec((tg, W_GLA), fw), pl.BlockSpec((tg, W_GLA), bw),
                   pl.BlockSpec((1, W_GLA, GLA_QK), st), pl.BlockSpec((1, W_GLA, GLA_QK), st)],
        out_shape=[jax.ShapeDtypeStruct((n, W_GLA), F32), jax.ShapeDtypeStruct((n, W_GLA), F32), sshape, sshape],
        compiler_params=_params("arbitrary", "arbitrary"),
        name="gla",
    )(kq, v, la, kq, v, la, h0f, h0b, trif, trib, rexp, bd)


def _fft_tail(ab, cs_ref, wf_ref, norm):
    f = jnp.dot(ab, cs_ref[...].astype(BF16), preferred_element_type=F32) * norm
    return jnp.dot(f.astype(BF16), wf_ref[...], preferred_element_type=F32)


def _fft1_kernel(w_ref, u_ref, y_ref, rows_ref):
    n1, group, lanes = u_ref.shape
    w = w_ref[...].astype(BF16)
    rows_ref[...] = u_ref[...].reshape(n1 * group, lanes)
    for j in range(group):
        y_ref[j] = jnp.dot(w, rows_ref[pl.ds(j, n1, stride=group), :].astype(BF16), preferred_element_type=F32)


def _fft2_kernel(tw_ref, dft_ref, yre_ref, yim_ref, cs_ref, wf_ref, o_ref, rows_ref, *, norm):
    halves, n2, group, lanes = yre_ref.shape
    for p, r in enumerate((yre_ref, yim_ref)):
        for h in range(halves):
            rows_ref[p, h] = r[h].reshape(n2 * group, lanes)
    c2, s2 = dft_ref[0], dft_ref[1]
    parts = []
    for g in range(group):
        tc, ts = tw_ref[g, 0:1, :], tw_ref[g, 1:2, :]
        mc = c2 * tc - s2 * ts
        ms = s2 * tc + c2 * ts
        m = jnp.concatenate([jnp.concatenate([mc, ms], axis=1),
                             jnp.concatenate([-ms, mc], axis=1)], axis=0).astype(BF16)
        ys = jnp.concatenate(
            [jnp.concatenate([rows_ref[p, h, pl.ds(g, n2, stride=group), :] for h in range(halves)], axis=1)
             for p in range(2)], axis=0).astype(BF16)
        z = jnp.dot(m, ys, preferred_element_type=F32)
        parts.append(jnp.concatenate([z[:n2], z[n2:]], axis=1).astype(BF16))
    out = _fft_tail(jnp.concatenate(parts, axis=0), cs_ref, wf_ref, norm)
    for g in range(group):
        o_ref[:, g, :] = out[g * n2:(g + 1) * n2]


def _fft_direct_kernel(m_ref, u_ref, cs_ref, wf_ref, o_ref, *, norm):
    n = u_ref.shape[0]
    z = jnp.dot(m_ref[...].astype(BF16), u_ref[...].astype(BF16), preferred_element_type=F32)
    o_ref[...] = _fft_tail(jnp.concatenate([z[:n], z[n:]], axis=1).astype(BF16), cs_ref, wf_ref, norm)


def _dft_cos_sin(n):
    k = np.arange(n, dtype=np.int64)
    ang = 2.0 * np.pi * ((k[:, None] * k[None, :]) % n).astype(np.float64) / n
    return np.cos(ang), np.sin(ang)


def _channel_dft():
    c, s = _dft_cos_sin(FFT_DG)
    eye = np.eye(FFT_GROUPS)
    return jnp.asarray(np.concatenate([np.kron(eye, c), np.kron(eye, s)], axis=0), F32)


def _fft_latent_call(u, wf_bd, layer, *, batch):
    n = u.shape[0] // batch
    n1, n2 = n // FFT_N2, FFT_N2
    halves = W_FFT // LANES
    c1, s1 = _dft_cos_sin(n1)
    w1 = jnp.asarray(np.concatenate([c1, -s1], axis=0), F32)
    tw_ang = 2.0 * np.pi * (np.arange(n1, dtype=np.int64)[:, None] * np.arange(n2, dtype=np.int64)[None, :]) / n
    tw = jnp.asarray(np.stack([np.cos(tw_ang), np.sin(tw_ang)], axis=1), F32)
    dft2 = jnp.asarray(np.stack(_dft_cos_sin(n2), axis=0), F32)
    cs = _channel_dft()
    norm = float(1.0 / np.sqrt(n * FFT_DG))
    g1 = FFT_STAGE1_GROUP
    y = pl.pallas_call(
        _fft1_kernel,
        grid=(batch, n2 // g1, halves),
        in_specs=[_const_spec(w1.shape), pl.BlockSpec((None, n1, g1, LANES), lambda b, j, h: (b, 0, j, h))],
        out_specs=pl.BlockSpec((None, None, g1, 2 * n1, LANES), lambda b, j, h: (b, h, j, 0, 0)),
        out_shape=jax.ShapeDtypeStruct((batch, halves, n2, 2 * n1, LANES), F32),
        scratch_shapes=[pltpu.VMEM((n1 * g1, LANES), F32)],
        compiler_params=_params("arbitrary", "arbitrary", "arbitrary"),
        name="fft_stage1",
    )(w1, u.reshape(batch, n1, n2, W_FFT))
    g2 = min(n1, FFT_STAGE2_GROUP)
    nj = n1 // g2
    y_spec = lambda part: pl.BlockSpec((None, halves, n2, g2, LANES), lambda j, b: (b, 0, 0, part * nj + j, 0))
    out = pl.pallas_call(
        functools.partial(_fft2_kernel, norm=norm),
        grid=(nj, batch),
        in_specs=[pl.BlockSpec((g2, 2, n2), lambda j, b: (j, 0, 0)), _const_spec(dft2.shape), y_spec(0), y_spec(1),
                  _const_spec(cs.shape), _layer_spec(wf_bd, layer)],
        out_specs=pl.BlockSpec((None, n2, g2, W_FFT), lambda j, b: (b, 0, j, 0)),
        out_shape=jax.ShapeDtypeStruct((batch, n2, n1, W_FFT), F32),
        scratch_shapes=[pltpu.VMEM((2, halves, n2 * g2, LANES), F32)],
        compiler_params=_params("arbitrary", "arbitrary"),
        name="fft_stage2",
    )(tw, dft2, y, y, cs, wf_bd)
    return out.reshape(batch * n, W_FFT)


def _fft_direct_call(u, wf_bd, layer, *, batch):
    n = u.shape[0] // batch
    c, s = _dft_cos_sin(n)
    m = jnp.asarray(np.concatenate([c, -s], axis=0), F32)
    cs = _channel_dft()
    norm = float(1.0 / np.sqrt(n * FFT_DG))
    return pl.pallas_call(
        functools.partial(_fft_direct_kernel, norm=norm),
        grid=(batch,),
        in_specs=[_const_spec(m.shape), pl.BlockSpec((n, W_FFT), lambda b: (b, 0)),
                  _const_spec(cs.shape), _layer_spec(wf_bd, layer)],
        out_specs=pl.BlockSpec((n, W_FFT), lambda b: (b, 0)),
        out_shape=jax.ShapeDtypeStruct((batch * n, W_FFT), F32),
        compiler_params=_params("arbitrary"),
        name="fft_direct",
    )(m, u, cs, wf_bd)


def _out_kernel(x_ref, mod_ref, of_ref, ob_ref, gate_ref, fft_ref, ycp_ref, glag_ref, ones_ref, wout_ref,
                n2g_ref, wup_ref, cw_ref, cb_ref, wdn_ref, fing_ref, o_ref,
                x1_ref, h2_ref, au_ref, h_ref, acc_ref, *, row_len, final):
    t_rows = x_ref.shape[0]
    d_ff = wdn_ref.shape[0]
    nchunk = d_ff // FFN_CHUNK
    o = of_ref[...] + ob_ref[...]
    osq = o * o
    hi = osq.astype(BF16)
    lo = (osq - hi.astype(F32)).astype(BF16)
    ms = (jnp.dot(hi, ones_ref[...], preferred_element_type=F32)
          + jnp.dot(lo, ones_ref[...], preferred_element_type=F32)) * (1.0 / GLA_DV)
    gl = (o * lax.rsqrt(ms + EPS)) * glag_ref[...] * _silu(gate_ref[...])
    ymix = jnp.concatenate([gl.astype(BF16), fft_ref[...].astype(BF16), ycp_ref[...]], axis=1)
    x1 = x_ref[...] + mod_ref[2:3, :] * jnp.dot(ymix, wout_ref[...], preferred_element_type=F32)
    x1_ref[...] = x1
    h2_ref[...] = _rms_mod(x1, n2g_ref[...], mod_ref[4:5, :], mod_ref[3:4, :]).astype(BF16)
    acc_ref[...] = jnp.zeros(acc_ref.shape, F32)

    def up(c, slot):
        for half in range(2):
            cols = slice(half * d_ff + c * FFN_CHUNK, half * d_ff + (c + 1) * FFN_CHUNK)
            au_ref[slot, :, half * FFN_CHUNK:(half + 1) * FFN_CHUNK] = jnp.dot(
                h2_ref[...], wup_ref[:, cols], preferred_element_type=F32)

    def elem(c, slot):
        cols = slice(c * FFN_CHUNK, (c + 1) * FFN_CHUNK)
        pos = lax.broadcasted_iota(jnp.int32, (t_rows, FFN_CHUNK), 0) & (row_len - 1)
        a = au_ref[slot, :, :FFN_CHUNK]
        cw = cw_ref[:, cols]
        a = (jnp.where(pos == 0, 0.0, pltpu.roll(a, 1, 0)) * cw[0:1] + a * cw[1:2]
             + jnp.where(pos == row_len - 1, 0.0, pltpu.roll(a, t_rows - 1, 0)) * cw[2:3] + cb_ref[:, cols])
        h_ref[slot] = (_silu(a) * au_ref[slot, :, FFN_CHUNK:]).astype(BF16)

    def down(c, slot):
        rows = slice(c * FFN_CHUNK, (c + 1) * FFN_CHUNK)
        acc_ref[...] += jnp.dot(h_ref[slot], wdn_ref[rows, :], preferred_element_type=F32)

    def stage(c):
        up(c + 1, 0)
        elem(c, 1)
        down(c - 1, 0)
        up(c + 2, 1)
        elem(c + 1, 0)
        down(c, 1)

    assert nchunk % 2 == 1 and nchunk >= 3
    up(0, 0)
    up(1, 1)
    elem(0, 0)

    for i in range((nchunk - 3) // 2):
        stage(2 * i + 1)
    c = nchunk - 2
    up(c + 1, 0)
    elem(c, 1)
    down(c - 1, 0)
    elem(c + 1, 0)
    down(c, 1)
    down(c + 1, 0)
    x2 = x1_ref[...] + mod_ref[5:6, :] * acc_ref[...]
    if final:
        ms2 = jnp.mean(x2 * x2, axis=-1, keepdims=True)
        x2 = (x2 * lax.rsqrt(ms2 + EPS)) * fing_ref[...]
    o_ref[...] = x2


def _out_call(xt, mod4, layer, mod_base, groups, o_f, o_b, gate, yfft, ycp, params, fing, *, row_len, final):
    n, d = xt.shape
    t = OUT_TILE
    tiles_per_group = n // groups // t
    assert tiles_per_group * t * groups == n
    hd = np.arange(W_GLA) // GLA_DV
    ones = jnp.asarray((hd[:, None] == hd[None, :]).astype(np.float32), BF16)
    row = lambda i: (i, 0)
    acts = [xt, o_f, o_b, gate, yfft, ycp]
    glag, wout, n2g, wup, cw, cb, wdn = params
    consts = [glag, ones, wout, n2g, wup, cw, cb, wdn, fing]
    return pl.pallas_call(
        functools.partial(_out_kernel, row_len=row_len, final=final),
        grid=(n // t,),
        in_specs=[pl.BlockSpec((t, d), row), _mod_spec(mod4, layer, mod_base, tiles_per_group)]
                 + [pl.BlockSpec((t, a.shape[1]), row) for a in acts[1:]]
                 + [_const_spec(a.shape) if a is ones or a is fing else _layer_spec(a, layer) for a in consts],
        out_specs=pl.BlockSpec((t, d), row),
        out_shape=jax.ShapeDtypeStruct((n, d), F32),
        scratch_shapes=[pltpu.VMEM((t, d), F32), pltpu.VMEM((t, d), BF16),
                        pltpu.VMEM((2, t, 2 * FFN_CHUNK), F32), pltpu.VMEM((2, t, FFN_CHUNK), BF16),
                        pltpu.VMEM((t, d), F32)],
        compiler_params=_params("arbitrary"),
        name="out_ffn",
    )(xt, mod4, *acts[1:], *consts)


def _block_diag(w):
    dep, g, a, b = w.shape
    eye = jnp.eye(g, dtype=w.dtype)
    return (eye[None, :, None, :, None] * w[:, :, :, None, :]).reshape(dep, g * a, g * b)


def _w_in_layout_kernel(w_ref, o_ref):
    a0, a1 = COL_STARTS[2], COL_STARTS[4]
    w = w_ref[...]
    pad = jnp.zeros((w.shape[0], IN_COLS - w.shape[1]), w.dtype)
    o_ref[...] = jnp.concatenate([w[:, :a0], w[:, a1:], w[:, a0:a1], pad], axis=1).astype(BF16)


def _w_in_layout_call(w_in):
    dep, d, cols = w_in.shape
    rows = TOKEN_TILE
    return pl.pallas_call(
        _w_in_layout_kernel,
        grid=(dep, d // rows),
        in_specs=[pl.BlockSpec((None, rows, cols), lambda i, j: (i, j, 0))],
        out_specs=pl.BlockSpec((None, rows, IN_COLS), lambda i, j: (i, j, 0)),
        out_shape=jax.ShapeDtypeStruct((dep, d, IN_COLS), BF16),
        compiler_params=_params("arbitrary", "arbitrary"),
        name="w_in_layout",
    )(w_in)


def _prep_params(norm1_g, norm2_g, w_in, gla_w_a2, gla_b_a2, gla_norm_g, fft_w, conv_w, conv_b, pool_w,
                 pool_scale, w_out, ffn_w_up, ffn_conv_w, ffn_conv_b, ffn_w_down):
    dep = w_in.shape[0]
    w_in_r = _w_in_layout_call(w_in)
    wa2 = jnp.zeros((dep, LANES, 2 * GLA_QK), F32)
    wa2 = wa2.at[:, 0:GLA_RANK, 0:GLA_QK].set(gla_w_a2[:, 0])
    wa2 = wa2.at[:, GLA_RANK:2 * GLA_RANK, GLA_QK:].set(gla_w_a2[:, 1])
    in_params = (norm1_g[:, None, :], w_in_r, wa2.astype(BF16), gla_b_a2.reshape(dep, 1, 2 * GLA_QK),
                 conv_w, conv_b[:, None, :], _block_diag(pool_w).astype(BF16), pool_scale[:, None, :])
    out_params = (jnp.tile(gla_norm_g, (1, GLA_HEADS))[:, None, :], w_out.astype(BF16), norm2_g[:, None, :],
                  ffn_w_up.astype(BF16), ffn_conv_w, ffn_conv_b[:, None, :], ffn_w_down.astype(BF16))
    return in_params, out_params, _block_diag(fft_w).astype(BF16)


def kernel(x, c, ctx, c_ctx, norm1_g, norm2_g, w_mod, b_mod, w_in, gla_w_a2, gla_b_a2, gla_norm_g,
           fft_w, conv_w, conv_b, pool_w, pool_scale, w_out, ffn_w_up, ffn_conv_w, ffn_conv_b,
           ffn_w_down, final_norm_g):
    batch, seq, d = x.shape
    ctx_len = ctx.shape[1]
    depth = w_mod.shape[0]
    assert seq % GLA_BLOCK == 0 and seq % FFT_N2 == 0 and ctx_len == TOKEN_TILE

    cc = jnp.concatenate([c, c_ctx[None, :], jnp.zeros((MOD_ROWS - batch - 1, d), F32)], axis=0)
    mod4 = _mod_call(cc, w_mod, b_mod).reshape(depth, MOD_ROWS, 6, d)
    in_params, out_params, wf_bd = _prep_params(
        norm1_g, norm2_g, w_in, gla_w_a2, gla_b_a2, gla_norm_g, fft_w, conv_w, conv_b, pool_w, pool_scale,
        w_out, ffn_w_up, ffn_conv_w, ffn_conv_b, ffn_w_down)

    xt = x.reshape(batch * seq, d)
    xc = ctx.reshape(batch * ctx_len, d)
    zero_state = jnp.zeros((batch, W_GLA, GLA_QK), F32)
    fing = final_norm_g.reshape(1, d)
    for i in range(depth):
        last = i == depth - 1
        kq, v, la, gate, ufft, ycp = _in_call(xc, mod4, i, batch, 1, in_params, row_len=ctx_len)
        o_f, o_b, s_f, s_b = _gla_call(kq, v, la, zero_state, zero_state, batch=batch)
        if not last:
            yfft = _fft_direct_call(ufft, wf_bd, i, batch=batch)
            xc = _out_call(xc, mod4, i, batch, 1, o_f, o_b, gate, yfft, ycp, out_params, fing,
                           row_len=ctx_len, final=False)

        kq, v, la, gate, ufft, ycp = _in_call(xt, mod4, i, 0, batch, in_params, row_len=GRID_W)
        o_f, o_b, _, _ = _gla_call(kq, v, la, s_f, s_b, batch=batch)
        yfft = _fft_latent_call(ufft, wf_bd, i, batch=batch)
        xt = _out_call(xt, mod4, i, 0, batch, o_f, o_b, gate, yfft, ycp, out_params, fing,
                       row_len=GRID_W, final=last)
    return xt.reshape(batch, seq, d)
```

```python
import functools

import numpy as np
import jax
import jax.numpy as jnp
from jax import lax
from jax.experimental import pallas as pl
from jax.experimental.pallas import tpu as pltpu

F32 = jnp.float32
BF16 = jnp.bfloat16

GRID_W = 64
EPS = 1e-6
GLA_HEADS = 4
GLA_DK = 32
GLA_DV = 64
GLA_QK = GLA_HEADS * GLA_DK
W_GLA = GLA_HEADS * GLA_DV
GLA_RANK = 16
GLA_TAU = 16.0
FFT_GROUPS = 4
FFT_DG = 64
W_FFT = FFT_GROUPS * FFT_DG
W_CONV = 256
POOL_WINDOWS = (2, 4, 8, 16)
POOL_DG = 64
W_POOL = len(POOL_WINDOWS) * POOL_DG
COL_SIZES = (GLA_QK, W_GLA, GLA_RANK, GLA_RANK, GLA_QK, W_GLA, W_FFT, W_CONV, W_CONV, W_CONV, W_POOL)
COL_STARTS = tuple(int(s) for s in np.cumsum((0,) + COL_SIZES)[:-1])

LANES = 128
VMEM_LIMIT_BYTES = 56 * 1024 * 1024

TOKEN_TILE = 256
IN_TILE = 1024
OUT_TILE = 512
GLA_CHUNK = 64
GLA_SUB = 8
GLA_BLOCK = 256
FFN_CHUNK = 256
FFT_N2 = 128
FFT_STAGE1_GROUP = 32
FFT_STAGE2_GROUP = 16
NEG_BIG = -1e30
LOG2_E = 1.4426950408889634

IN_K, IN_V, IN_Q, IN_G, IN_FFT, IN_H, IN_BG, IN_CG, IN_POOL, IN_A = (
    0, 128, 384, 512, 768, 1024, 1280, 1536, 1792, 2048)
IN_COLS = IN_A + LANES
MOD_ROWS = 8


def _const_spec(shape):
    nd = len(shape)
    return pl.BlockSpec(shape, lambda *_: (0,) * nd, pipeline_mode=pl.Buffered(1))


def _layer_spec(arr, layer):
    nd = arr.ndim
    return pl.BlockSpec((None,) + arr.shape[1:], lambda *_: (layer,) + (0,) * (nd - 1),
                        pipeline_mode=pl.Buffered(1))


def _mod_spec(mod4, layer, base, tiles_per_group):
    return pl.BlockSpec((None, None) + mod4.shape[2:], lambda i: (layer, base + i // tiles_per_group, 0, 0))


def _params(*sem):
    return pltpu.CompilerParams(dimension_semantics=sem, vmem_limit_bytes=VMEM_LIMIT_BYTES)


def _rms_mod(x, g, scale, shift):
    ms = jnp.mean(x * x, axis=-1, keepdims=True)
    return (x * lax.rsqrt(ms + EPS)) * g * (1.0 + scale) + shift


def _silu(a):
    return a * jax.nn.sigmoid(a)


def _mod_kernel(c_ref, w_ref, b_ref, o_ref):
    s = _silu(c_ref[...]).astype(BF16)
    o_ref[0] = jnp.dot(s, w_ref[0].astype(BF16), preferred_element_type=F32) + b_ref[0]


def _mod_call(cc, w_mod, b_mod):
    depth, d, n = w_mod.shape
    tn = 1536
    return pl.pallas_call(
        _mod_kernel,
        grid=(depth, n // tn),
        in_specs=[pl.BlockSpec(cc.shape, lambda i, j: (0, 0)),
                  pl.BlockSpec((1, d, tn), lambda i, j: (i, 0, j)),
                  pl.BlockSpec((1, 1, tn), lambda i, j: (i, 0, j))],
        out_specs=pl.BlockSpec((1, cc.shape[0], tn), lambda i, j: (i, 0, j)),
        out_shape=jax.ShapeDtypeStruct((depth, cc.shape[0], n), F32),
        compiler_params=_params("arbitrary", "arbitrary"),
        name="modulation",
    )(cc, w_mod, b_mod.reshape(depth, 1, n))


def _in_kernel(x_ref, mod_ref, g_ref, w_ref, wa2_ref, ba2_ref, cw_ref, cb_ref, cnt_ref, wpool_ref, pscale_ref,
               kq_ref, v_ref, la_ref, gate_ref, fft_ref, ycp_ref, *, row_len):
    nsub = x_ref.shape[0] // TOKEN_TILE
    ps = []
    for s in range(nsub):
        rows = slice(s * TOKEN_TILE, (s + 1) * TOKEN_TILE)
        h = _rms_mod(x_ref[rows, :], g_ref[...], mod_ref[1:2, :], mod_ref[0:1, :]).astype(BF16)
        ps.append(jnp.dot(h, w_ref[...], preferred_element_type=F32))
    for s in range(nsub):
        rows = slice(s * TOKEN_TILE, (s + 1) * TOKEN_TILE)
        _in_mixers(ps[s], rows, wa2_ref, ba2_ref, cw_ref, cb_ref, cnt_ref, wpool_ref, pscale_ref,
                   kq_ref, v_ref, la_ref, gate_ref, fft_ref, ycp_ref, row_len)


def _in_mixers(p, rows, wa2_ref, ba2_ref, cw_ref, cb_ref, cnt_ref, wpool_ref, pscale_ref,
               kq_ref, v_ref, la_ref, gate_ref, fft_ref, ycp_ref, row_len):
    t_rows = p.shape[0]
    kq_ref[rows, :GLA_QK] = p[:, IN_K:IN_K + GLA_QK]
    kq_ref[rows, GLA_QK:] = p[:, IN_Q:IN_Q + GLA_QK] * (GLA_DK ** -0.5)
    v_ref[rows, :] = p[:, IN_V:IN_V + W_GLA]
    gate_ref[rows, :] = p[:, IN_G:IN_G + W_GLA]
    fft_ref[rows, :] = p[:, IN_FFT:IN_FFT + W_FFT]
    z = jnp.dot(p[:, IN_A:IN_A + LANES].astype(BF16), wa2_ref[...], preferred_element_type=F32) + ba2_ref[...]
    la_ref[rows, :] = (jnp.minimum(z, 0.0) - jnp.log(1.0 + jnp.exp(-jnp.abs(z)))) * (1.0 / GLA_TAU)

    def pos(a):
        return lax.broadcasted_iota(jnp.int32, a.shape, 0) & (row_len - 1)

    def prev(a, s):
        return jnp.where(pos(a) >= s, pltpu.roll(a, s, 0), 0.0)

    def nxt(a, s):
        return jnp.where(pos(a) < row_len - s, pltpu.roll(a, t_rows - s, 0), 0.0)

    t = p[:, IN_CG:IN_CG + W_CONV] * p[:, IN_H:IN_H + W_CONV]
    cw = cw_ref[...]
    conv = prev(t, 1) * cw[0:1] + t * cw[1:2] + nxt(t, 1) * cw[2:3] + cb_ref[...]
    ycp_ref[rows, :W_CONV] = (p[:, IN_BG:IN_BG + W_CONV] * conv).astype(BF16)

    u = p[:, IN_POOL:IN_POOL + W_POOL]
    halves = []
    for side, steps in ((0, 1), (1, 3)):
        f = u[:, side * LANES:(side + 1) * LANES]
        g = prev(f, 1)
        sums = [g + f]
        for i in range(steps):
            f = f + nxt(f, 1 << i)
            g = g + prev(g, 1 << i)
            sums.append(g + f)
        lane = lax.broadcasted_iota(jnp.int32, f.shape, 1)
        halves.append(jnp.where(lane < POOL_DG, sums[-2], sums[-1]))
    tot = jnp.concatenate(halves, axis=1)
    pooled = tot / cnt_ref[...] - u
    yp = jnp.dot(pooled.astype(BF16), wpool_ref[...], preferred_element_type=F32) * pscale_ref[...]
    ycp_ref[rows, W_CONV:] = yp.astype(BF16)


def _pool_counts(t_rows, row_len):
    pos = np.arange(t_rows) % row_len
    cols = []
    for w in POOL_WINDOWS:
        lo = np.clip(pos - w // 2, 0, row_len - 1)
        hi = np.clip(pos + w // 2 - 1, 0, row_len - 1)
        cols.append(np.repeat((hi - lo + 1).astype(np.float32)[:, None], POOL_DG, axis=1))
    return np.concatenate(cols, axis=1)


def _in_call(xt, mod4, layer, mod_base, groups, params, *, row_len):
    n, d = xt.shape
    t = min(IN_TILE, n // groups)
    tiles_per_group = n // groups // t
    assert tiles_per_group * t * groups == n
    cnt = jnp.asarray(_pool_counts(TOKEN_TILE, row_len))
    row = lambda i: (i, 0)
    outs = [jax.ShapeDtypeStruct((n, 2 * GLA_QK), F32), jax.ShapeDtypeStruct((n, W_GLA), F32),
            jax.ShapeDtypeStruct((n, 2 * GLA_QK), F32), jax.ShapeDtypeStruct((n, W_GLA), F32),
            jax.ShapeDtypeStruct((n, W_FFT), F32), jax.ShapeDtypeStruct((n, W_CONV + W_POOL), BF16)]
    g, w, wa2, ba2, cw, cb, wpool, pscale = params
    return pl.pallas_call(
        functools.partial(_in_kernel, row_len=row_len),
        grid=(n // t,),
        in_specs=[pl.BlockSpec((t, d), row), _mod_spec(mod4, layer, mod_base, tiles_per_group),
                  _layer_spec(g, layer), _layer_spec(w, layer), _layer_spec(wa2, layer), _layer_spec(ba2, layer),
                  _layer_spec(cw, layer), _layer_spec(cb, layer), _const_spec(cnt.shape),
                  _layer_spec(wpool, layer), _layer_spec(pscale, layer)],
        out_specs=[pl.BlockSpec((t, o.shape[1]), row) for o in outs],
        out_shape=outs,
        compiler_params=_params("arbitrary"),
        name="in_proj",
    )(xt, mod4, g, w, wa2, ba2, cw, cb, cnt, wpool, pscale)


def _gla_block(kq_ref, v_ref, la_ref, s_ref, cum_ref, o_ref, consts, fwd):
    tri2, rexp, bdmask, hq, hv = consts
    k, q, v, la = kq_ref[:, :GLA_QK], kq_ref[:, GLA_QK:], v_ref[...], la_ref[...]
    t_rows = q.shape[0]
    c, sub = GLA_CHUNK, GLA_SUB
    nchunk = t_rows // c
    nt = (((1,), (1,)), ((), ()))
    tn = (((0,), (0,)), ((), ()))
    la_hi = la.astype(BF16)
    la_lo = (la - la_hi.astype(F32)).astype(BF16)
    ce = (jnp.dot(tri2, la_hi, preferred_element_type=F32)
          + jnp.dot(tri2, la_lo, preferred_element_type=F32)) * LOG2_E
    cum, end = ce[:t_rows], ce[t_rows:]
    cum_ref[...] = cum
    yield

    nblk = t_rows // sub

    def row_of_block(ref, width, j):
        cols = [jnp.concatenate([jnp.broadcast_to(ref[pl.ds(sub * blk + j, 1), c0:c0 + LANES], (sub, LANES))
                                 for blk in range(nblk)], axis=0) for c0 in range(0, width, LANES)]
        return cols[0] if len(cols) == 1 else jnp.concatenate(cols, axis=1)

    ii = lax.broadcasted_iota(jnp.int32, (t_rows, GLA_QK), 0) & (sub - 1)
    es = []
    for j in range(sub):
        keep = (ii >= j) if fwd else (ii <= j)
        w = jnp.exp2(jnp.where(keep, cum - row_of_block(cum_ref, GLA_QK, j), NEG_BIG))
        es.append((w * (q * row_of_block(kq_ref, GLA_QK, j))).astype(BF16))
    pr = jnp.dot(jnp.concatenate(es, axis=0), rexp, preferred_element_type=F32)
    yield

    qe = (q * jnp.exp2(cum)).astype(BF16)
    kd = (k * jnp.exp2(end - cum)).astype(BF16)
    v16 = v.astype(BF16)
    s = s_ref[0]
    o_parts = [None] * nchunk
    for ci in (range(nchunk) if fwd else reversed(range(nchunk))):
        rows = slice(ci * c, (ci + 1) * c)
        o_parts[ci] = lax.dot_general(qe[rows], s.astype(BF16), nt, preferred_element_type=F32)
        upd = lax.dot_general(v16[rows], kd[rows], tn, preferred_element_type=F32)
        s = s * jnp.exp2(end[ci * c:ci * c + 1]) + upd * bdmask
    s_ref[0] = s
    o = jnp.concatenate(o_parts, axis=0)

    half = t_rows // 2
    b = c // 2
    while b >= sub:
        npair = t_rows // (2 * b)
        refs = []
        for p in range(npair):
            r = 2 * b * p + (b - 1 if fwd else b)
            refs.append(jnp.broadcast_to(cum_ref[pl.ds(r, 1), :], (2 * b, GLA_QK)))
        e = jnp.exp2(-jnp.abs(cum - jnp.concatenate(refs, axis=0)))
        qfull, kfull = q * e, k * e
        first = [slice(2 * b * p, 2 * b * p + b) for p in range(npair)]
        second = [slice(2 * b * p + b, 2 * b * p + 2 * b) for p in range(npair)]
        qrows, krows = (second, first) if fwd else (first, second)
        qsel = jnp.concatenate([qfull[r] for r in qrows], axis=0)
        ksel = jnp.concatenate([kfull[r] for r in krows], axis=0).astype(BF16)
        vsel = jnp.concatenate([v[r] for r in krows], axis=0).astype(BF16)
        qst = jnp.concatenate([qsel * hq[h] for h in range(GLA_HEADS)], axis=0).astype(BF16)
        att = lax.dot_general(qst, ksel, nt, preferred_element_type=F32)
        ri = lax.broadcasted_iota(jnp.int32, att.shape, 0)
        ci = lax.broadcasted_iota(jnp.int32, att.shape, 1)
        att = jnp.where(((ri & (half - 1)) ^ ci) < b, att, 0.0).astype(BF16)
        att_k = jnp.concatenate([att[h * half:(h + 1) * half] for h in range(GLA_HEADS)], axis=1)
        v_k = jnp.concatenate([vsel * hv[h].astype(BF16) for h in range(GLA_HEADS)], axis=0)
        res = jnp.dot(att_k, v_k, preferred_element_type=F32)
        zero = jnp.zeros((b, W_GLA), F32)
        pieces = []
        for p in range(npair):
            piece = res[p * b:(p + 1) * b]
            pieces += [zero, piece] if fwd else [piece, zero]
        o = o + jnp.concatenate(pieces, axis=0)
        b //= 2
    yield

    for j in range(sub):
        o = o + pr[t_rows * j:t_rows * (j + 1)] * row_of_block(v_ref, W_GLA, j)
    o_ref[...] = o


def _gla_kernel(kqf_ref, vf_ref, laf_ref, kqb_ref, vb_ref, lab_ref, h0f_ref, h0b_ref,
                trif_ref, trib_ref, rexp_ref, bd_ref,
                of_ref, ob_ref, sf_ref, sb_ref, cum_ref):
    @pl.when(pl.program_id(1) == 0)
    def _():
        sf_ref[...] = h0f_ref[...]
        sb_ref[...] = h0b_ref[...]

    lq = lax.broadcasted_iota(jnp.int32, (1, GLA_QK), 1)
    lv = lax.broadcasted_iota(jnp.int32, (1, W_GLA), 1)
    hq = [jnp.where((lq >= h * GLA_DK) & (lq < (h + 1) * GLA_DK), 1.0, 0.0) for h in range(GLA_HEADS)]
    hv = [jnp.where((lv >= h * GLA_DV) & (lv < (h + 1) * GLA_DV), 1.0, 0.0) for h in range(GLA_HEADS)]
    rexp, bdmask = rexp_ref[...], bd_ref[...]
    scans = [_gla_block(kqf_ref, vf_ref, laf_ref, sf_ref, cum_ref.at[0], of_ref,
                        (trif_ref[...], rexp, bdmask, hq, hv), True),
             _gla_block(kqb_ref, vb_ref, lab_ref, sb_ref, cum_ref.at[1], ob_ref,
                        (trib_ref[...], rexp, bdmask, hq, hv), False)]
    live = True
    while live:
        live = False
        for scan in scans:
            live = next(scan, "done") != "done" or live


def _gla_consts(t):
    c = GLA_CHUNK
    i = np.arange(t)
    same = (i[None, :] // c) == (i[:, None] // c)
    trif = np.concatenate([same & (i[None, :] <= i[:, None]), same], axis=0).astype(np.float32)
    trib = np.concatenate([same & (i[None, :] >= i[:, None]), same], axis=0).astype(np.float32)
    hk = np.arange(GLA_QK) // GLA_DK
    hd = np.arange(W_GLA) // GLA_DV
    rexp = (hk[:, None] == hd[None, :]).astype(np.float32)
    bd = (hd[:, None] == hk[None, :]).astype(np.float32)
    return (jnp.asarray(trif, BF16), jnp.asarray(trib, BF16), jnp.asarray(rexp, BF16), jnp.asarray(bd, F32))


def _gla_call(kq, v, la, h0f, h0b, *, batch):
    n = kq.shape[0]
    tg = min(GLA_BLOCK, n // batch)
    nb = n // batch // tg
    trif, trib, rexp, bd = _gla_consts(tg)
    fw = lambda b, i: (b * nb + i, 0)
    bw = lambda b, i: (b * nb + nb - 1 - i, 0)
    bw1 = lambda b, i: (b * nb + nb - 1 - i, 1)
    st = lambda b, i: (b, 0, 0)
    sshape = jax.ShapeDtypeStruct((batch, W_GLA, GLA_QK), F32)
    return pl.pallas_call(
        _gla_kernel,
        grid=(batch, nb),
        in_specs=[pl.BlockSpec((tg, 2 * GLA_QK), fw), pl.BlockSpec((tg, W_GLA), fw), pl.BlockSpec((tg, GLA_QK), fw),
                  pl.BlockSpec((tg, 2 * GLA_QK), bw), pl.BlockSpec((tg, W_GLA), bw), pl.BlockSpec((tg, GLA_QK), bw1),
                  pl.BlockSpec((1, W_GLA, GLA_QK), st), pl.BlockSpec((1, W_GLA, GLA_QK), st),
                  _const_spec(trif.shape), _const_spec(trib.shape), _const_spec(rexp.shape), _const_spec(bd.shape)],
        out_specs=[pl.BlockSpec((tg, W_GLA), fw), pl.BlockSpec((tg, W_GLA), bw),
                   pl.BlockSpec((1, W_GLA, GLA_QK), st), pl.BlockSpec((1, W_GLA, GLA_QK), st)],
        out_shape=[jax.ShapeDtypeStruct((n, W_GLA), F32), jax.ShapeDtypeStruct((n, W_GLA), F32), sshape, sshape],
        scratch_shapes=[pltpu.VMEM((2, tg, GLA_QK), F32)],
        compiler_params=_params("arbitrary", "arbitrary"),
        name="gla",
    )(kq, v, la, kq, v, la, h0f, h0b, trif, trib, rexp, bd)


def _fft_tail(ab, cs_ref, wf_ref, norm):
    f = jnp.dot(ab, cs_ref[...].astype(BF16), preferred_element_type=F32) * norm
    return jnp.dot(f.astype(BF16), wf_ref[...], preferred_element_type=F32)


def _fft1_kernel(w_ref, u_ref, y_ref, rows_ref):
    n1, group, _ = u_ref.shape
    halves = rows_ref.shape[0]
    w = w_ref[...].astype(BF16)
    for h in range(halves):
        rows_ref[h] = u_ref[:, :, h * LANES:(h + 1) * LANES].reshape(n1 * group, LANES)
    for j in range(group):
        for h in range(halves):
            y_ref[h, j] = jnp.dot(w, rows_ref[h, pl.ds(j, n1, stride=group), :].astype(BF16),
                                  preferred_element_type=F32)


def _fft2_kernel(tw_ref, dft_ref, yre_ref, yim_ref, cs_ref, wf_ref, o_ref, rows_ref, *, norm):
    halves, n2, group, lanes = yre_ref.shape
    for p, r in enumerate((yre_ref, yim_ref)):
        for h in range(halves):
            rows_ref[p, h] = r[h].reshape(n2 * group, lanes)
    c2, s2 = dft_ref[0], dft_ref[1]
    parts = []
    for g in range(group):
        tc, ts = tw_ref[g, 0:1, :], tw_ref[g, 1:2, :]
        mc = c2 * tc - s2 * ts
        ms = s2 * tc + c2 * ts
        m = jnp.concatenate([jnp.concatenate([mc, ms], axis=1),
                             jnp.concatenate([-ms, mc], axis=1)], axis=0).astype(BF16)
        ys = jnp.concatenate(
            [jnp.concatenate([rows_ref[p, h, pl.ds(g, n2, stride=group), :] for h in range(halves)], axis=1)
             for p in range(2)], axis=0).astype(BF16)
        z = jnp.dot(m, ys, preferred_element_type=F32)
        parts.append(jnp.concatenate([z[:n2], z[n2:]], axis=1).astype(BF16))
    out = _fft_tail(jnp.concatenate(parts, axis=0), cs_ref, wf_ref, norm)
    for g in range(group):
        for h in range(halves):
            rows_ref[0, h, pl.ds(g, n2, stride=group), :] = out[g * n2:(g + 1) * n2, h * lanes:(h + 1) * lanes]
    for h in range(halves):
        o_ref[:, :, h * lanes:(h + 1) * lanes] = rows_ref[0, h].reshape(n2, group, lanes)


def _fft_direct_kernel(m_ref, u_ref, cs_ref, wf_ref, o_ref, *, norm):
    n = u_ref.shape[0]
    z = jnp.dot(m_ref[...].astype(BF16), u_ref[...].astype(BF16), preferred_element_type=F32)
    o_ref[...] = _fft_tail(jnp.concatenate([z[:n], z[n:]], axis=1).astype(BF16), cs_ref, wf_ref, norm)


def _dft_cos_sin(n):
    k = np.arange(n, dtype=np.int64)
    ang = 2.0 * np.pi * ((k[:, None] * k[None, :]) % n).astype(np.float64) / n
    return np.cos(ang), np.sin(ang)


def _channel_dft():
    c, s = _dft_cos_sin(FFT_DG)
    eye = np.eye(FFT_GROUPS)
    return jnp.asarray(np.concatenate([np.kron(eye, c), np.kron(eye, s)], axis=0), F32)


def _fft_latent_call(u, wf_bd, layer, *, batch):
    n = u.shape[0] // batch
    n1, n2 = n // FFT_N2, FFT_N2
    halves = W_FFT // LANES
    c1, s1 = _dft_cos_sin(n1)
    w1 = jnp.asarray(np.concatenate([c1, -s1], axis=0), F32)
    tw_ang = 2.0 * np.pi * (np.arange(n1, dtype=np.int64)[:, None] * np.arange(n2, dtype=np.int64)[None, :]) / n
    tw = jnp.asarray(np.stack([np.cos(tw_ang), np.sin(tw_ang)], axis=1), F32)
    dft2 = jnp.asarray(np.stack(_dft_cos_sin(n2), axis=0), F32)
    cs = _channel_dft()
    norm = float(1.0 / np.sqrt(n * FFT_DG))
    g1 = FFT_STAGE1_GROUP
    y = pl.pallas_call(
        _fft1_kernel,
        grid=(batch, n2 // g1),
        in_specs=[_const_spec(w1.shape), pl.BlockSpec((None, n1, g1, W_FFT), lambda b, j: (b, 0, j, 0))],
        out_specs=pl.BlockSpec((None, halves, g1, 2 * n1, LANES), lambda b, j: (b, 0, j, 0, 0)),
        out_shape=jax.ShapeDtypeStruct((batch, halves, n2, 2 * n1, LANES), F32),
        scratch_shapes=[pltpu.VMEM((halves, n1 * g1, LANES), F32)],
        compiler_params=_params("arbitrary", "arbitrary"),
        name="fft_stage1",
    )(w1, u.reshape(batch, n1, n2, W_FFT))
    g2 = min(n1, FFT_STAGE2_GROUP)
    nj = n1 // g2
    y_spec = lambda part: pl.BlockSpec((None, halves, n2, g2, LANES), lambda j, b: (b, 0, 0, part * nj + j, 0))
    out = pl.pallas_call(
        functools.partial(_fft2_kernel, norm=norm),
        grid=(nj, batch),
        in_specs=[pl.BlockSpec((g2, 2, n2), lambda j, b: (j, 0, 0)), _const_spec(dft2.shape), y_spec(0), y_spec(1),
                  _const_spec(cs.shape), _layer_spec(wf_bd, layer)],
        out_specs=pl.BlockSpec((None, n2, g2, W_FFT), lambda j, b: (b, 0, j, 0)),
        out_shape=jax.ShapeDtypeStruct((batch, n2, n1, W_FFT), F32),
        scratch_shapes=[pltpu.VMEM((2, halves, n2 * g2, LANES), F32)],
        compiler_params=_params("arbitrary", "arbitrary"),
        name="fft_stage2",
    )(tw, dft2, y, y, cs, wf_bd)
    return out.reshape(batch * n, W_FFT)


def _fft_direct_call(u, wf_bd, layer, *, batch):
    n = u.shape[0] // batch
    c, s = _dft_cos_sin(n)
    m = jnp.asarray(np.concatenate([c, -s], axis=0), F32)
    cs = _channel_dft()
    norm = float(1.0 / np.sqrt(n * FFT_DG))
    return pl.pallas_call(
        functools.partial(_fft_direct_kernel, norm=norm),
        grid=(batch,),
        in_specs=[_const_spec(m.shape), pl.BlockSpec((n, W_FFT), lambda b: (b, 0)),
                  _const_spec(cs.shape), _layer_spec(wf_bd, layer)],
        out_specs=pl.BlockSpec((n, W_FFT), lambda b: (b, 0)),
        out_shape=jax.ShapeDtypeStruct((batch * n, W_FFT), F32),
        compiler_params=_params("arbitrary"),
        name="fft_direct",
    )(m, u, cs, wf_bd)


def _out_kernel(x_ref, mod_ref, of_ref, ob_ref, gate_ref, fft_ref, ycp_ref, glag_ref, ones_ref, wout_ref,
                n2g_ref, wup_ref, cw_ref, cb_ref, wdn_ref, fing_ref, o_ref,
                x1_ref, h2_ref, au_ref, h_ref, acc_ref, *, row_len, final):
    _out_head(x_ref, mod_ref, of_ref, ob_ref, gate_ref, fft_ref, ycp_ref, glag_ref, ones_ref, wout_ref,
              n2g_ref, x1_ref, h2_ref)
    _out_ffn(mod_ref, wup_ref, cw_ref, cb_ref, wdn_ref, fing_ref, o_ref, x1_ref, h2_ref, au_ref, h_ref,
             acc_ref, row_len, final)


def _out_head(x_ref, mod_ref, of_ref, ob_ref, gate_ref, fft_ref, ycp_ref, glag_ref, ones_ref, wout_ref, n2g_ref,
              x1_ref, h2_ref):
    o = of_ref[...] + ob_ref[...]
    osq = o * o
    hi = osq.astype(BF16)
    lo = (osq - hi.astype(F32)).astype(BF16)
    ms = (jnp.dot(hi, ones_ref[...], preferred_element_type=F32)
          + jnp.dot(lo, ones_ref[...], preferred_element_type=F32)) * (1.0 / GLA_DV)
    gl = (o * lax.rsqrt(ms + EPS)) * glag_ref[...] * _silu(gate_ref[...])
    ymix = jnp.concatenate([gl.astype(BF16), fft_ref[...].astype(BF16), ycp_ref[...]], axis=1)
    x1 = x_ref[...] + mod_ref[2:3, :] * jnp.dot(ymix, wout_ref[...], preferred_element_type=F32)
    x1_ref[...] = x1
    h2_ref[...] = _rms_mod(x1, n2g_ref[...], mod_ref[4:5, :], mod_ref[3:4, :]).astype(BF16)


def _out_ffn(mod_ref, wup_ref, cw_ref, cb_ref, wdn_ref, fing_ref, o_ref, x1_ref, h2_ref, au_ref, h_ref, acc_ref,
             row_len, final):
    t_rows = o_ref.shape[0]
    d_ff = wdn_ref.shape[0]
    nchunk = d_ff // FFN_CHUNK
    acc_ref[...] = jnp.zeros(acc_ref.shape, F32)

    def up(c, slot):
        for half in range(2):
            cols = slice(half * d_ff + c * FFN_CHUNK, half * d_ff + (c + 1) * FFN_CHUNK)
            au_ref[slot, :, half * FFN_CHUNK:(half + 1) * FFN_CHUNK] = jnp.dot(
                h2_ref[...], wup_ref[:, cols], preferred_element_type=F32)

    def elem(c, slot):
        cols = slice(c * FFN_CHUNK, (c + 1) * FFN_CHUNK)
        pos = lax.broadcasted_iota(jnp.int32, (t_rows, FFN_CHUNK), 0) & (row_len - 1)
        a = au_ref[slot, :, :FFN_CHUNK]
        cw = cw_ref[:, cols]
        a = (jnp.where(pos == 0, 0.0, pltpu.roll(a, 1, 0)) * cw[0:1] + a * cw[1:2]
             + jnp.where(pos == row_len - 1, 0.0, pltpu.roll(a, t_rows - 1, 0)) * cw[2:3] + cb_ref[:, cols])
        h_ref[slot] = (_silu(a) * au_ref[slot, :, FFN_CHUNK:]).astype(BF16)

    def down(c, slot):
        rows = slice(c * FFN_CHUNK, (c + 1) * FFN_CHUNK)
        acc_ref[...] += jnp.dot(h_ref[slot], wdn_ref[rows, :], preferred_element_type=F32)

    def stage(c):
        up(c + 1, 0)
        elem(c, 1)
        down(c - 1, 0)
        up(c + 2, 1)
        elem(c + 1, 0)
        down(c, 1)

    assert nchunk % 2 == 1 and nchunk >= 3
    up(0, 0)
    up(1, 1)
    elem(0, 0)

    for i in range((nchunk - 3) // 2):
        stage(2 * i + 1)
    c = nchunk - 2
    up(c + 1, 0)
    elem(c, 1)
    down(c - 1, 0)
    elem(c + 1, 0)
    down(c, 1)
    down(c + 1, 0)
    x2 = x1_ref[...] + mod_ref[5:6, :] * acc_ref[...]
    if final:
        ms2 = jnp.mean(x2 * x2, axis=-1, keepdims=True)
        x2 = (x2 * lax.rsqrt(ms2 + EPS)) * fing_ref[...]
    o_ref[...] = x2


def _out_call(xt, mod4, layer, mod_base, groups, o_f, o_b, gate, yfft, ycp, params, fing, *, row_len, final):
    n, d = xt.shape
    t = OUT_TILE
    tiles_per_group = n // groups // t
    assert tiles_per_group * t * groups == n
    hd = np.arange(W_GLA) // GLA_DV
    ones = jnp.asarray((hd[:, None] == hd[None, :]).astype(np.float32), BF16)
    row = lambda i: (i, 0)
    acts = [xt, o_f, o_b, gate, yfft, ycp]
    glag, wout, n2g, wup, cw, cb, wdn = params
    consts = [glag, ones, wout, n2g, wup, cw, cb, wdn, fing]
    return pl.pallas_call(
        functools.partial(_out_kernel, row_len=row_len, final=final),
        grid=(n // t,),
        in_specs=[pl.BlockSpec((t, d), row), _mod_spec(mod4, layer, mod_base, tiles_per_group)]
                 + [pl.BlockSpec((t, a.shape[1]), row) for a in acts[1:]]
                 + [_const_spec(a.shape) if a is ones or a is fing else _layer_spec(a, layer) for a in consts],
        out_specs=pl.BlockSpec((t, d), row),
        out_shape=jax.ShapeDtypeStruct((n, d), F32),
        scratch_shapes=[pltpu.VMEM((t, d), F32), pltpu.VMEM((t, d), BF16),
                        pltpu.VMEM((2, t, 2 * FFN_CHUNK), F32), pltpu.VMEM((2, t, FFN_CHUNK), BF16),
                        pltpu.VMEM((t, d), F32)],
        compiler_params=_params("arbitrary"),
        name="out_ffn",
    )(xt, mod4, *acts[1:], *consts)


def _block_diag(w):
    dep, g, a, b = w.shape
    eye = jnp.eye(g, dtype=w.dtype)
    return (eye[None, :, None, :, None] * w[:, :, :, None, :]).reshape(dep, g * a, g * b)


def _w_in_layout_kernel(wt_ref, o_ref):
    a0, a1 = COL_STARTS[2], COL_STARTS[4]
    wt = wt_ref[...]
    pad = jnp.zeros((IN_COLS - wt.shape[0], wt.shape[1]), wt.dtype)
    o_ref[...] = jnp.concatenate([wt[:a0], wt[a1:], wt[a0:a1], pad], axis=0).T.astype(BF16)


def _w_in_layout_call(w_in):
    dep, d, cols = w_in.shape
    rows = TOKEN_TILE
    return pl.pallas_call(
        _w_in_layout_kernel,
        grid=(dep, d // rows),
        in_specs=[pl.BlockSpec((None, cols, rows), lambda i, j: (i, 0, j))],
        out_specs=pl.BlockSpec((None, rows, IN_COLS), lambda i, j: (i, j, 0)),
        out_shape=jax.ShapeDtypeStruct((dep, d, IN_COLS), BF16),
        compiler_params=_params("arbitrary", "arbitrary"),
        name="w_in_layout",
    )(jnp.swapaxes(w_in, 1, 2))


def _prep_params(norm1_g, norm2_g, w_in, gla_w_a2, gla_b_a2, gla_norm_g, fft_w, conv_w, conv_b, pool_w,
                 pool_scale, w_out, ffn_w_up, ffn_conv_w, ffn_conv_b, ffn_w_down):
    dep = w_in.shape[0]
    w_in_r = _w_in_layout_call(w_in)
    wa2 = jnp.zeros((dep, LANES, 2 * GLA_QK), F32)
    wa2 = wa2.at[:, 0:GLA_RANK, 0:GLA_QK].set(gla_w_a2[:, 0])
    wa2 = wa2.at[:, GLA_RANK:2 * GLA_RANK, GLA_QK:].set(gla_w_a2[:, 1])
    in_params = (norm1_g[:, None, :], w_in_r, wa2.astype(BF16), gla_b_a2.reshape(dep, 1, 2 * GLA_QK),
                 conv_w, conv_b[:, None, :], _block_diag(pool_w).astype(BF16), pool_scale[:, None, :])
    out_params = (jnp.tile(gla_norm_g, (1, GLA_HEADS))[:, None, :], w_out.astype(BF16), norm2_g[:, None, :],
                  ffn_w_up.astype(BF16), ffn_conv_w, ffn_conv_b[:, None, :], ffn_w_down.astype(BF16))
    return in_params, out_params, _block_diag(fft_w).astype(BF16)


def kernel(x, c, ctx, c_ctx, norm1_g, norm2_g, w_mod, b_mod, w_in, gla_w_a2, gla_b_a2, gla_norm_g,
           fft_w, conv_w, conv_b, pool_w, pool_scale, w_out, ffn_w_up, ffn_conv_w, ffn_conv_b,
           ffn_w_down, final_norm_g):
    batch, seq, d = x.shape
    ctx_len = ctx.shape[1]
    depth = w_mod.shape[0]
    assert seq % GLA_BLOCK == 0 and seq % FFT_N2 == 0 and ctx_len == TOKEN_TILE

    cc = jnp.concatenate([c, c_ctx[None, :], jnp.zeros((MOD_ROWS - batch - 1, d), F32)], axis=0)
    mod4 = _mod_call(cc, w_mod, b_mod).reshape(depth, MOD_ROWS, 6, d)
    in_params, out_params, wf_bd = _prep_params(
        norm1_g, norm2_g, w_in, gla_w_a2, gla_b_a2, gla_norm_g, fft_w, conv_w, conv_b, pool_w, pool_scale,
        w_out, ffn_w_up, ffn_conv_w, ffn_conv_b, ffn_w_down)

    xt = x.reshape(batch * seq, d)
    xc = ctx.reshape(batch * ctx_len, d)
    zero_state = jnp.zeros((batch, W_GLA, GLA_QK), F32)
    fing = final_norm_g.reshape(1, d)
    for i in range(depth):
        last = i == depth - 1
        kq, v, la, gate, ufft, ycp = _in_call(xc, mod4, i, batch, 1, in_params, row_len=ctx_len)
        o_f, o_b, s_f, s_b = _gla_call(kq, v, la, zero_state, zero_state, batch=batch)
        if not last:
            yfft = _fft_direct_call(ufft, wf_bd, i, batch=batch)
            xc = _out_call(xc, mod4, i, batch, 1, o_f, o_b, gate, yfft, ycp, out_params, fing,
                           row_len=ctx_len, final=False)

        kq, v, la, gate, ufft, ycp = _in_call(xt, mod4, i, 0, batch, in_params, row_len=GRID_W)
        o_f, o_b, _, _ = _gla_call(kq, v, la, s_f, s_b, batch=batch)
        yfft = _fft_latent_call(ufft, wf_bd, i, batch=batch)
        xt = _out_call(xt, mod4, i, 0, batch, o_f, o_b, gate, yfft, ycp, out_params, fing,
                       row_len=GRID_W, final=last)
    return xt.reshape(batch, seq, d)
```

```python
import functools

import numpy as np
import jax
import jax.numpy as jnp
from jax import lax
from jax.experimental import pallas as pl
from jax.experimental.pallas import tpu as pltpu

F32 = jnp.float32
BF16 = jnp.bfloat16

GRID_W = 64
EPS = 1e-6
GLA_HEADS = 4
GLA_DK = 32
GLA_DV = 64
GLA_QK = GLA_HEADS * GLA_DK
W_GLA = GLA_HEADS * GLA_DV
GLA_RANK = 16
GLA_TAU = 16.0
FFT_GROUPS = 4
FFT_DG = 64
W_FFT = FFT_GROUPS * FFT_DG
W_CONV = 256
POOL_WINDOWS = (2, 4, 8, 16)
POOL_DG = 64
W_POOL = len(POOL_WINDOWS) * POOL_DG
COL_SIZES = (GLA_QK, W_GLA, GLA_RANK, GLA_RANK, GLA_QK, W_GLA, W_FFT, W_CONV, W_CONV, W_CONV, W_POOL)
COL_STARTS = tuple(int(s) for s in np.cumsum((0,) + COL_SIZES)[:-1])

LANES = 128
VMEM_LIMIT_BYTES = 56 * 1024 * 1024

TOKEN_TILE = 256
IN_TILE = 1024
OUT_TILE = 512
GLA_CHUNK = 64
GLA_SUB = 8
GLA_SUBBLOCK = 256
GLA_BLOCK = 512
FFN_CHUNK = 256
FFT_N2 = 128
FFT_STAGE1_GROUP = 32
FFT_STAGE2_GROUP = 16
NEG_BIG = -1e30
LOG2_E = 1.4426950408889634

IN_K, IN_V, IN_Q, IN_G, IN_FFT, IN_H, IN_BG, IN_CG, IN_POOL, IN_A = (
    0, 128, 384, 512, 768, 1024, 1280, 1536, 1792, 2048)
IN_COLS = IN_A + LANES
MOD_ROWS = 8


def _const_spec(shape):
    nd = len(shape)
    return pl.BlockSpec(shape, lambda *_: (0,) * nd, pipeline_mode=pl.Buffered(1))


def _layer_spec(arr, layer):
    nd = arr.ndim
    return pl.BlockSpec((None,) + arr.shape[1:], lambda *_: (layer,) + (0,) * (nd - 1),
                        pipeline_mode=pl.Buffered(1))


def _mod_spec(mod4, layer, base, tiles_per_group):
    return pl.BlockSpec((None, None) + mod4.shape[2:], lambda i: (layer, base + i // tiles_per_group, 0, 0))


def _params(*sem):
    return pltpu.CompilerParams(dimension_semantics=sem, vmem_limit_bytes=VMEM_LIMIT_BYTES)


def _rms_mod(x, g, scale, shift):
    ms = jnp.mean(x * x, axis=-1, keepdims=True)
    return (x * lax.rsqrt(ms + EPS)) * g * (1.0 + scale) + shift


def _silu(a):
    return a * jax.nn.sigmoid(a)


def _mod_kernel(c_ref, w_ref, b_ref, o_ref):
    s = _silu(c_ref[...]).astype(BF16)
    o_ref[0] = jnp.dot(s, w_ref[0].astype(BF16), preferred_element_type=F32) + b_ref[0]


def _mod_call(cc, w_mod, b_mod):
    depth, d, n = w_mod.shape
    tn = 1536
    return pl.pallas_call(
        _mod_kernel,
        grid=(depth, n // tn),
        in_specs=[pl.BlockSpec(cc.shape, lambda i, j: (0, 0)),
                  pl.BlockSpec((1, d, tn), lambda i, j: (i, 0, j)),
                  pl.BlockSpec((1, 1, tn), lambda i, j: (i, 0, j))],
        out_specs=pl.BlockSpec((1, cc.shape[0], tn), lambda i, j: (i, 0, j)),
        out_shape=jax.ShapeDtypeStruct((depth, cc.shape[0], n), F32),
        compiler_params=_params("arbitrary", "arbitrary"),
        name="modulation",
    )(cc, w_mod, b_mod.reshape(depth, 1, n))


def _in_kernel(x_ref, mod_ref, g_ref, w_ref, wa2_ref, ba2_ref, cw_ref, cb_ref, cnt_ref, wpool_ref, pscale_ref,
               kq_ref, v_ref, la_ref, gate_ref, fft_ref, ycp_ref, *, row_len):
    nsub = x_ref.shape[0] // TOKEN_TILE

    def project(s):
        rows = slice(s * TOKEN_TILE, (s + 1) * TOKEN_TILE)
        h = _rms_mod(x_ref[rows, :], g_ref[...], mod_ref[1:2, :], mod_ref[0:1, :]).astype(BF16)
        return jnp.dot(h, w_ref[...], preferred_element_type=F32)

    p_next = project(0)
    for s in range(nsub):
        p, p_next = p_next, (project(s + 1) if s + 1 < nsub else None)
        _in_mixers(p, slice(s * TOKEN_TILE, (s + 1) * TOKEN_TILE), wa2_ref, ba2_ref, cw_ref, cb_ref, cnt_ref,
                   wpool_ref, pscale_ref, kq_ref, v_ref, la_ref, gate_ref, fft_ref, ycp_ref, row_len)


def _in_mixers(p, rows, wa2_ref, ba2_ref, cw_ref, cb_ref, cnt_ref, wpool_ref, pscale_ref,
               kq_ref, v_ref, la_ref, gate_ref, fft_ref, ycp_ref, row_len):
    t_rows = p.shape[0]
    kq_ref[rows, :GLA_QK] = p[:, IN_K:IN_K + GLA_QK]
    kq_ref[rows, GLA_QK:] = p[:, IN_Q:IN_Q + GLA_QK] * (GLA_DK ** -0.5)
    v_ref[rows, :] = p[:, IN_V:IN_V + W_GLA]
    gate_ref[rows, :] = p[:, IN_G:IN_G + W_GLA]
    fft_ref[rows, :] = p[:, IN_FFT:IN_FFT + W_FFT]
    z = jnp.dot(p[:, IN_A:IN_A + LANES].astype(BF16), wa2_ref[...], preferred_element_type=F32) + ba2_ref[...]
    la_ref[rows, :] = (jnp.minimum(z, 0.0) - jnp.log(1.0 + jnp.exp(-jnp.abs(z)))) * (1.0 / GLA_TAU)

    def pos(a):
        return lax.broadcasted_iota(jnp.int32, a.shape, 0) & (row_len - 1)

    def prev(a, s):
        return jnp.where(pos(a) >= s, pltpu.roll(a, s, 0), 0.0)

    def nxt(a, s):
        return jnp.where(pos(a) < row_len - s, pltpu.roll(a, t_rows - s, 0), 0.0)

    t = p[:, IN_CG:IN_CG + W_CONV] * p[:, IN_H:IN_H + W_CONV]
    cw = cw_ref[...]
    conv = prev(t, 1) * cw[0:1] + t * cw[1:2] + nxt(t, 1) * cw[2:3] + cb_ref[...]
    ycp_ref[rows, :W_CONV] = (p[:, IN_BG:IN_BG + W_CONV] * conv).astype(BF16)

    u = p[:, IN_POOL:IN_POOL + W_POOL]
    halves = []
    for side, steps in ((0, 1), (1, 3)):
        f = u[:, side * LANES:(side + 1) * LANES]
        g = prev(f, 1)
        sums = [g + f]
        for i in range(steps):
            f = f + nxt(f, 1 << i)
            g = g + prev(g, 1 << i)
            sums.append(g + f)
        lane = lax.broadcasted_iota(jnp.int32, f.shape, 1)
        halves.append(jnp.where(lane < POOL_DG, sums[-2], sums[-1]))
    tot = jnp.concatenate(halves, axis=1)
    pooled = tot / cnt_ref[...] - u
    yp = jnp.dot(pooled.astype(BF16), wpool_ref[...], preferred_element_type=F32) * pscale_ref[...]
    ycp_ref[rows, W_CONV:] = yp.astype(BF16)


def _pool_counts(t_rows, row_len):
    pos = np.arange(t_rows) % row_len
    cols = []
    for w in POOL_WINDOWS:
        lo = np.clip(pos - w // 2, 0, row_len - 1)
        hi = np.clip(pos + w // 2 - 1, 0, row_len - 1)
        cols.append(np.repeat((hi - lo + 1).astype(np.float32)[:, None], POOL_DG, axis=1))
    return np.concatenate(cols, axis=1)


def _in_call(xt, mod4, layer, mod_base, groups, params, *, row_len):
    n, d = xt.shape
    t = min(IN_TILE, n // groups)
    tiles_per_group = n // groups // t
    assert tiles_per_group * t * groups == n
    cnt = jnp.asarray(_pool_counts(TOKEN_TILE, row_len))
    row = lambda i: (i, 0)
    outs = [jax.ShapeDtypeStruct((n, 2 * GLA_QK), F32), jax.ShapeDtypeStruct((n, W_GLA), F32),
            jax.ShapeDtypeStruct((n, 2 * GLA_QK), F32), jax.ShapeDtypeStruct((n, W_GLA), F32),
            jax.ShapeDtypeStruct((n, W_FFT), F32), jax.ShapeDtypeStruct((n, W_CONV + W_POOL), BF16)]
    g, w, wa2, ba2, cw, cb, wpool, pscale = params
    return pl.pallas_call(
        functools.partial(_in_kernel, row_len=row_len),
        grid=(n // t,),
        in_specs=[pl.BlockSpec((t, d), row), _mod_spec(mod4, layer, mod_base, tiles_per_group),
                  _layer_spec(g, layer), _layer_spec(w, layer), _layer_spec(wa2, layer), _layer_spec(ba2, layer),
                  _layer_spec(cw, layer), _layer_spec(cb, layer), _const_spec(cnt.shape),
                  _layer_spec(wpool, layer), _layer_spec(pscale, layer)],
        out_specs=[pl.BlockSpec((t, o.shape[1]), row) for o in outs],
        out_shape=outs,
        compiler_params=_params("arbitrary"),
        name="in_proj",
    )(xt, mod4, g, w, wa2, ba2, cw, cb, cnt, wpool, pscale)


def _gla_block(kq_ref, v_ref, la_ref, s_ref, cum_ref, o_ref, consts, fwd):
    tri2, rexp, bdmask, hq, hv = consts
    k, q, v, la = kq_ref[:, :GLA_QK], kq_ref[:, GLA_QK:], v_ref[...], la_ref[...]
    t_rows = q.shape[0]
    c, sub = GLA_CHUNK, GLA_SUB
    nchunk = t_rows // c
    nt = (((1,), (1,)), ((), ()))
    tn = (((0,), (0,)), ((), ()))
    la_hi = la.astype(BF16)
    la_lo = (la - la_hi.astype(F32)).astype(BF16)
    ce = (jnp.dot(tri2, la_hi, preferred_element_type=F32)
          + jnp.dot(tri2, la_lo, preferred_element_type=F32)) * LOG2_E
    cum, end = ce[:t_rows], ce[t_rows:]
    cum_ref[...] = cum
    yield

    nblk = t_rows // sub

    def row_of_block(ref, width, j):
        cols = [jnp.concatenate([jnp.broadcast_to(ref[pl.ds(sub * blk + j, 1), c0:c0 + LANES], (sub, LANES))
                                 for blk in range(nblk)], axis=0) for c0 in range(0, width, LANES)]
        return cols[0] if len(cols) == 1 else jnp.concatenate(cols, axis=1)

    ii = lax.broadcasted_iota(jnp.int32, (t_rows, GLA_QK), 0) & (sub - 1)
    es = []
    for j in range(sub):
        keep = (ii >= j) if fwd else (ii <= j)
        w = jnp.exp2(jnp.where(keep, cum - row_of_block(cum_ref, GLA_QK, j), NEG_BIG))
        es.append((w * (q * row_of_block(kq_ref, GLA_QK, j))).astype(BF16))
    pr = jnp.dot(jnp.concatenate(es, axis=0), rexp, preferred_element_type=F32)
    yield

    qe = (q * jnp.exp2(cum)).astype(BF16)
    kd = (k * jnp.exp2(end - cum)).astype(BF16)
    v16 = v.astype(BF16)
    s = s_ref[0]
    o_parts = [None] * nchunk
    for ci in (range(nchunk) if fwd else reversed(range(nchunk))):
        rows = slice(ci * c, (ci + 1) * c)
        o_parts[ci] = lax.dot_general(qe[rows], s.astype(BF16), nt, preferred_element_type=F32)
        upd = lax.dot_general(v16[rows], kd[rows], tn, preferred_element_type=F32)
        s = s * jnp.exp2(end[ci * c:ci * c + 1]) + upd * bdmask
    s_ref[0] = s
    o = jnp.concatenate(o_parts, axis=0)

    half = t_rows // 2
    b = c // 2
    while b >= sub:
        npair = t_rows // (2 * b)
        refs = []
        for p in range(npair):
            r = 2 * b * p + (b - 1 if fwd else b)
            refs.append(jnp.broadcast_to(cum_ref[pl.ds(r, 1), :], (2 * b, GLA_QK)))
        e = jnp.exp2(-jnp.abs(cum - jnp.concatenate(refs, axis=0)))
        qfull, kfull = q * e, k * e
        first = [slice(2 * b * p, 2 * b * p + b) for p in range(npair)]
        second = [slice(2 * b * p + b, 2 * b * p + 2 * b) for p in range(npair)]
        qrows, krows = (second, first) if fwd else (first, second)
        qsel = jnp.concatenate([qfull[r] for r in qrows], axis=0)
        ksel = jnp.concatenate([kfull[r] for r in krows], axis=0).astype(BF16)
        vsel = jnp.concatenate([v[r] for r in krows], axis=0).astype(BF16)
        qst = jnp.concatenate([qsel * hq[h] for h in range(GLA_HEADS)], axis=0).astype(BF16)
        att = lax.dot_general(qst, ksel, nt, preferred_element_type=F32)
        ri = lax.broadcasted_iota(jnp.int32, att.shape, 0)
        ci = lax.broadcasted_iota(jnp.int32, att.shape, 1)
        att = jnp.where(((ri & (half - 1)) ^ ci) < b, att, 0.0).astype(BF16)
        att_k = jnp.concatenate([att[h * half:(h + 1) * half] for h in range(GLA_HEADS)], axis=1)
        v_k = jnp.concatenate([vsel * hv[h].astype(BF16) for h in range(GLA_HEADS)], axis=0)
        res = jnp.dot(att_k, v_k, preferred_element_type=F32)
        zero = jnp.zeros((b, W_GLA), F32)
        pieces = []
        for p in range(npair):
            piece = res[p * b:(p + 1) * b]
            pieces += [zero, piece] if fwd else [piece, zero]
        o = o + jnp.concatenate(pieces, axis=0)
        b //= 2
    yield

    for j in range(sub):
        o = o + pr[t_rows * j:t_rows * (j + 1)] * row_of_block(v_ref, W_GLA, j)
    o_ref[...] = o


def _gla_kernel(kqf_ref, vf_ref, laf_ref, kqb_ref, vb_ref, lab_ref, h0f_ref, h0b_ref,
                trif_ref, trib_ref, rexp_ref, bd_ref,
                of_ref, ob_ref, sf_ref, sb_ref, cum_ref):
    @pl.when(pl.program_id(1) == 0)
    def _():
        sf_ref[...] = h0f_ref[...]
        sb_ref[...] = h0b_ref[...]

    lq = lax.broadcasted_iota(jnp.int32, (1, GLA_QK), 1)
    lv = lax.broadcasted_iota(jnp.int32, (1, W_GLA), 1)
    hq = [jnp.where((lq >= h * GLA_DK) & (lq < (h + 1) * GLA_DK), 1.0, 0.0) for h in range(GLA_HEADS)]
    hv = [jnp.where((lv >= h * GLA_DV) & (lv < (h + 1) * GLA_DV), 1.0, 0.0) for h in range(GLA_HEADS)]
    rexp, bdmask = rexp_ref[...], bd_ref[...]
    nsub = kqf_ref.shape[0] // GLA_SUBBLOCK
    scans = []
    for r in range(nsub):
        lo, hi = pl.ds(r * GLA_SUBBLOCK, GLA_SUBBLOCK), pl.ds((nsub - 1 - r) * GLA_SUBBLOCK, GLA_SUBBLOCK)
        scans.append(_gla_block(kqf_ref.at[lo], vf_ref.at[lo], laf_ref.at[lo], sf_ref, cum_ref.at[2 * r],
                                of_ref.at[lo], (trif_ref[...], rexp, bdmask, hq, hv), True))
        scans.append(_gla_block(kqb_ref.at[hi], vb_ref.at[hi], lab_ref.at[hi], sb_ref, cum_ref.at[2 * r + 1],
                                ob_ref.at[hi], (trib_ref[...], rexp, bdmask, hq, hv), False))
    live = True
    while live:
        live = False
        for scan in scans:
            live = next(scan, "done") != "done" or live


def _gla_consts(t):
    c = GLA_CHUNK
    i = np.arange(t)
    same = (i[None, :] // c) == (i[:, None] // c)
    trif = np.concatenate([same & (i[None, :] <= i[:, None]), same], axis=0).astype(np.float32)
    trib = np.concatenate([same & (i[None, :] >= i[:, None]), same], axis=0).astype(np.float32)
    hk = np.arange(GLA_QK) // GLA_DK
    hd = np.arange(W_GLA) // GLA_DV
    rexp = (hk[:, None] == hd[None, :]).astype(np.float32)
    bd = (hd[:, None] == hk[None, :]).astype(np.float32)
    return (jnp.asarray(trif, BF16), jnp.asarray(trib, BF16), jnp.asarray(rexp, BF16), jnp.asarray(bd, F32))


def _gla_call(kq, v, la, h0f, h0b, *, batch):
    n = kq.shape[0]
    tg = min(GLA_BLOCK, n // batch)
    nb = n // batch // tg
    trif, trib, rexp, bd = _gla_consts(GLA_SUBBLOCK)
    fw = lambda b, i: (b * nb + i, 0)
    bw = lambda b, i: (b * nb + nb - 1 - i, 0)
    bw1 = lambda b, i: (b * nb + nb - 1 - i, 1)
    st = lambda b, i: (b, 0, 0)
    sshape = jax.ShapeDtypeStruct((batch, W_GLA, GLA_QK), F32)
    return pl.pallas_call(
        _gla_kernel,
        grid=(batch, nb),
        in_specs=[pl.BlockSpec((tg, 2 * GLA_QK), fw), pl.BlockSpec((tg, W_GLA), fw), pl.BlockSpec((tg, GLA_QK), fw),
                  pl.BlockSpec((tg, 2 * GLA_QK), bw), pl.BlockSpec((tg, W_GLA), bw), pl.BlockSpec((tg, GLA_QK), bw1),
                  pl.BlockSpec((1, W_GLA, GLA_QK), st), pl.BlockSpec((1, W_GLA, GLA_QK), st),
                  _const_spec(trif.shape), _const_spec(trib.shape), _const_spec(rexp.shape), _const_spec(bd.shape)],
        out_specs=[pl.BlockSpec((tg, W_GLA), fw), pl.BlockSpec((tg, W_GLA), bw),
                   pl.BlockSpec((1, W_GLA, GLA_QK), st), pl.BlockSpec((1, W_GLA, GLA_QK), st)],
        out_shape=[jax.ShapeDtypeStruct((n, W_GLA), F32), jax.ShapeDtypeStruct((n, W_GLA), F32), sshape, sshape],
        scratch_shapes=[pltpu.VMEM((2 * tg // GLA_SUBBLOCK, GLA_SUBBLOCK, GLA_QK), F32)],
        compiler_params=_params("arbitrary", "arbitrary"),
        name="gla",
    )(kq, v, la, kq, v, la, h0f, h0b, trif, trib, rexp, bd)


def _fft_tail(ab, cs_ref, wf_ref, norm):
    f = jnp.dot(ab, cs_ref[...].astype(BF16), preferred_element_type=F32) * norm
    return jnp.dot(f.astype(BF16), wf_ref[...], preferred_element_type=F32)


def _fft1_kernel(w_ref, u_ref, y_ref, rows_ref):
    n1, group, _ = u_ref.shape
    halves = rows_ref.shape[0]
    w = w_ref[...].astype(BF16)
    for h in range(halves):
        rows_ref[h] = u_ref[:, :, h * LANES:(h + 1) * LANES].reshape(n1 * group, LANES)
    for j in range(group):
        for h in range(halves):
            y_ref[h, j] = jnp.dot(w, rows_ref[h, pl.ds(j, n1, stride=group), :].astype(BF16),
                                  preferred_element_type=F32)


def _fft2_kernel(tw_ref, dft_ref, yre_ref, yim_ref, cs_ref, wf_ref, o_ref, rows_ref, *, norm):
    halves, n2, group, lanes = yre_ref.shape
    for p, r in enumerate((yre_ref, yim_ref)):
        for h in range(halves):
            rows_ref[p, h] = r[h].reshape(n2 * group, lanes)
    c2, s2 = dft_ref[0], dft_ref[1]
    parts = []
    for g in range(group):
        tc, ts = tw_ref[g, 0:1, :], tw_ref[g, 1:2, :]
        mc = c2 * tc - s2 * ts
        ms = s2 * tc + c2 * ts
        m = jnp.concatenate([jnp.concatenate([mc, ms], axis=1),
                             jnp.concatenate([-ms, mc], axis=1)], axis=0).astype(BF16)
        ys = jnp.concatenate(
            [jnp.concatenate([rows_ref[p, h, pl.ds(g, n2, stride=group), :] for h in range(halves)], axis=1)
             for p in range(2)], axis=0).astype(BF16)
        z = jnp.dot(m, ys, preferred_element_type=F32)
        parts.append(jnp.concatenate([z[:n2], z[n2:]], axis=1).astype(BF16))
    out = _fft_tail(jnp.concatenate(parts, axis=0), cs_ref, wf_ref, norm)
    for g in range(group):
        for h in range(halves):
            rows_ref[0, h, pl.ds(g, n2, stride=group), :] = out[g * n2:(g + 1) * n2, h * lanes:(h + 1) * lanes]
    for h in range(halves):
        o_ref[:, :, h * lanes:(h + 1) * lanes] = rows_ref[0, h].reshape(n2, group, lanes)


def _fft_direct_kernel(m_ref, u_ref, cs_ref, wf_ref, o_ref, *, norm):
    n = u_ref.shape[0]
    z = jnp.dot(m_ref[...].astype(BF16), u_ref[...].astype(BF16), preferred_element_type=F32)
    o_ref[...] = _fft_tail(jnp.concatenate([z[:n], z[n:]], axis=1).astype(BF16), cs_ref, wf_ref, norm)


def _dft_cos_sin(n):
    k = np.arange(n, dtype=np.int64)
    ang = 2.0 * np.pi * ((k[:, None] * k[None, :]) % n).astype(np.float64) / n
    return np.cos(ang), np.sin(ang)


def _channel_dft():
    c, s = _dft_cos_sin(FFT_DG)
    eye = np.eye(FFT_GROUPS)
    return jnp.asarray(np.concatenate([np.kron(eye, c), np.kron(eye, s)], axis=0), F32)


def _fft_latent_call(u, wf_bd, layer, *, batch):
    n = u.shape[0] // batch
    n1, n2 = n // FFT_N2, FFT_N2
    halves = W_FFT // LANES
    c1, s1 = _dft_cos_sin(n1)
    w1 = jnp.asarray(np.concatenate([c1, -s1], axis=0), F32)
    tw_ang = 2.0 * np.pi * (np.arange(n1, dtype=np.int64)[:, None] * np.arange(n2, dtype=np.int64)[None, :]) / n
    tw = jnp.asarray(np.stack([np.cos(tw_ang), np.sin(tw_ang)], axis=1), F32)
    dft2 = jnp.asarray(np.stack(_dft_cos_sin(n2), axis=0), F32)
    cs = _channel_dft()
    norm = float(1.0 / np.sqrt(n * FFT_DG))
    g1 = FFT_STAGE1_GROUP
    y = pl.pallas_call(
        _fft1_kernel,
        grid=(batch, n2 // g1),
        in_specs=[_const_spec(w1.shape), pl.BlockSpec((None, n1, g1, W_FFT), lambda b, j: (b, 0, j, 0))],
        out_specs=pl.BlockSpec((None, halves, g1, 2 * n1, LANES), lambda b, j: (b, 0, j, 0, 0)),
        out_shape=jax.ShapeDtypeStruct((batch, halves, n2, 2 * n1, LANES), F32),
        scratch_shapes=[pltpu.VMEM((halves, n1 * g1, LANES), F32)],
        compiler_params=_params("arbitrary", "arbitrary"),
        name="fft_stage1",
    )(w1, u.reshape(batch, n1, n2, W_FFT))
    g2 = min(n1, FFT_STAGE2_GROUP)
    nj = n1 // g2
    y_spec = lambda part: pl.BlockSpec((None, halves, n2, g2, LANES), lambda j, b: (b, 0, 0, part * nj + j, 0))
    out = pl.pallas_call(
        functools.partial(_fft2_kernel, norm=norm),
        grid=(nj, batch),
        in_specs=[pl.BlockSpec((g2, 2, n2), lambda j, b: (j, 0, 0)), _const_spec(dft2.shape), y_spec(0), y_spec(1),
                  _const_spec(cs.shape), _layer_spec(wf_bd, layer)],
        out_specs=pl.BlockSpec((None, n2, g2, W_FFT), lambda j, b: (b, 0, j, 0)),
        out_shape=jax.ShapeDtypeStruct((batch, n2, n1, W_FFT), F32),
        scratch_shapes=[pltpu.VMEM((2, halves, n2 * g2, LANES), F32)],
        compiler_params=_params("arbitrary", "arbitrary"),
        name="fft_stage2",
    )(tw, dft2, y, y, cs, wf_bd)
    return out.reshape(batch * n, W_FFT)


def _fft_direct_call(u, wf_bd, layer, *, batch):
    n = u.shape[0] // batch
    c, s = _dft_cos_sin(n)
    m = jnp.asarray(np.concatenate([c, -s], axis=0), F32)
    cs = _channel_dft()
    norm = float(1.0 / np.sqrt(n * FFT_DG))
    return pl.pallas_call(
        functools.partial(_fft_direct_kernel, norm=norm),
        grid=(batch,),
        in_specs=[_const_spec(m.shape), pl.BlockSpec((n, W_FFT), lambda b: (b, 0)),
                  _const_spec(cs.shape), _layer_spec(wf_bd, layer)],
        out_specs=pl.BlockSpec((n, W_FFT), lambda b: (b, 0)),
        out_shape=jax.ShapeDtypeStruct((batch * n, W_FFT), F32),
        compiler_params=_params("arbitrary"),
        name="fft_direct",
    )(m, u, cs, wf_bd)


def _out_kernel(x_ref, mod_ref, of_ref, ob_ref, gate_ref, fft_ref, ycp_ref, glag_ref, ones_ref, wout_ref,
                n2g_ref, wup_ref, cw_ref, cb_ref, wdn_ref, fing_ref, o_ref,
                x1_ref, h2_ref, au_ref, h_ref, acc_ref, *, row_len, final):
    _out_head(x_ref, mod_ref, of_ref, ob_ref, gate_ref, fft_ref, ycp_ref, glag_ref, ones_ref, wout_ref,
              n2g_ref, x1_ref, h2_ref)
    _out_ffn(mod_ref, wup_ref, cw_ref, cb_ref, wdn_ref, fing_ref, o_ref, x1_ref, h2_ref, au_ref, h_ref,
             acc_ref, row_len, final)


def _out_head(x_ref, mod_ref, of_ref, ob_ref, gate_ref, fft_ref, ycp_ref, glag_ref, ones_ref, wout_ref, n2g_ref,
              x1_ref, h2_ref):
    o = of_ref[...] + ob_ref[...]
    osq = o * o
    hi = osq.astype(BF16)
    lo = (osq - hi.astype(F32)).astype(BF16)
    ms = (jnp.dot(hi, ones_ref[...], preferred_element_type=F32)
          + jnp.dot(lo, ones_ref[...], preferred_element_type=F32)) * (1.0 / GLA_DV)
    gl = (o * lax.rsqrt(ms + EPS)) * glag_ref[...] * _silu(gate_ref[...])
    ymix = jnp.concatenate([gl.astype(BF16), fft_ref[...].astype(BF16), ycp_ref[...]], axis=1)
    x1 = x_ref[...] + mod_ref[2:3, :] * jnp.dot(ymix, wout_ref[...], preferred_element_type=F32)
    x1_ref[...] = x1
    h2_ref[...] = _rms_mod(x1, n2g_ref[...], mod_ref[4:5, :], mod_ref[3:4, :]).astype(BF16)


def _out_ffn(mod_ref, wup_ref, cw_ref, cb_ref, wdn_ref, fing_ref, o_ref, x1_ref, h2_ref, au_ref, h_ref, acc_ref,
             row_len, final):
    t_rows = o_ref.shape[0]
    d_ff = wdn_ref.shape[0]
    nchunk = d_ff // FFN_CHUNK
    acc_ref[...] = jnp.zeros(acc_ref.shape, F32)

    def up(c, slot):
        for half in range(2):
            cols = slice(half * d_ff + c * FFN_CHUNK, half * d_ff + (c + 1) * FFN_CHUNK)
            au_ref[slot, :, half * FFN_CHUNK:(half + 1) * FFN_CHUNK] = jnp.dot(
                h2_ref[...], wup_ref[:, cols], preferred_element_type=F32)

    def elem(c, slot):
        cols = slice(c * FFN_CHUNK, (c + 1) * FFN_CHUNK)
        pos = lax.broadcasted_iota(jnp.int32, (t_rows, FFN_CHUNK), 0) & (row_len - 1)
        a = au_ref[slot, :, :FFN_CHUNK]
        cw = cw_ref[:, cols]
        a = (jnp.where(pos == 0, 0.0, pltpu.roll(a, 1, 0)) * cw[0:1] + a * cw[1:2]
             + jnp.where(pos == row_len - 1, 0.0, pltpu.roll(a, t_rows - 1, 0)) * cw[2:3] + cb_ref[:, cols])
        h_ref[slot] = (_silu(a) * au_ref[slot, :, FFN_CHUNK:]).astype(BF16)

    def down(c, slot):
        rows = slice(c * FFN_CHUNK, (c + 1) * FFN_CHUNK)
        acc_ref[...] += jnp.dot(h_ref[slot], wdn_ref[rows, :], preferred_element_type=F32)

    def stage(c):
        up(c + 1, 0)
        elem(c, 1)
        down(c - 1, 0)
        up(c + 2, 1)
        elem(c + 1, 0)
        down(c, 1)

    assert nchunk % 2 == 1 and nchunk >= 3
    up(0, 0)
    up(1, 1)
    elem(0, 0)

    for i in range((nchunk - 3) // 2):
        stage(2 * i + 1)
    c = nchunk - 2
    up(c + 1, 0)
    elem(c, 1)
    down(c - 1, 0)
    elem(c + 1, 0)
    down(c, 1)
    down(c + 1, 0)
    x2 = x1_ref[...] + mod_ref[5:6, :] * acc_ref[...]
    if final:
        ms2 = jnp.mean(x2 * x2, axis=-1, keepdims=True)
        x2 = (x2 * lax.rsqrt(ms2 + EPS)) * fing_ref[...]
    o_ref[...] = x2


def _out_call(xt, mod4, layer, mod_base, groups, o_f, o_b, gate, yfft, ycp, params, fing, *, row_len, final):
    n, d = xt.shape
    t = OUT_TILE
    tiles_per_group = n // groups // t
    assert tiles_per_group * t * groups == n
    hd = np.arange(W_GLA) // GLA_DV
    ones = jnp.asarray((hd[:, None] == hd[None, :]).astype(np.float32), BF16)
    row = lambda i: (i, 0)
    acts = [xt, o_f, o_b, gate, yfft, ycp]
    glag, wout, n2g, wup, cw, cb, wdn = params
    consts = [glag, ones, wout, n2g, wup, cw, cb, wdn, fing]
    return pl.pallas_call(
        functools.partial(_out_kernel, row_len=row_len, final=final),
        grid=(n // t,),
        in_specs=[pl.BlockSpec((t, d), row), _mod_spec(mod4, layer, mod_base, tiles_per_group)]
                 + [pl.BlockSpec((t, a.shape[1]), row) for a in acts[1:]]
                 + [_const_spec(a.shape) if a is ones or a is fing else _layer_spec(a, layer) for a in consts],
        out_specs=pl.BlockSpec((t, d), row),
        out_shape=jax.ShapeDtypeStruct((n, d), F32),
        scratch_shapes=[pltpu.VMEM((t, d), F32), pltpu.VMEM((t, d), BF16),
                        pltpu.VMEM((2, t, 2 * FFN_CHUNK), F32), pltpu.VMEM((2, t, FFN_CHUNK), BF16),
                        pltpu.VMEM((t, d), F32)],
        compiler_params=_params("arbitrary"),
        name="out_ffn",
    )(xt, mod4, *acts[1:], *consts)


def _block_diag(w):
    dep, g, a, b = w.shape
    eye = jnp.eye(g, dtype=w.dtype)
    return (eye[None, :, None, :, None] * w[:, :, :, None, :]).reshape(dep, g * a, g * b)


def _w_in_layout_kernel(wt_ref, o_ref):
    a0, a1 = COL_STARTS[2], COL_STARTS[4]
    wt = wt_ref[...]
    pad = jnp.zeros((IN_COLS - wt.shape[0], wt.shape[1]), wt.dtype)
    o_ref[...] = jnp.concatenate([wt[:a0], wt[a1:], wt[a0:a1], pad], axis=0).T.astype(BF16)


def _w_in_layout_call(w_in):
    dep, d, cols = w_in.shape
    rows = TOKEN_TILE
    return pl.pallas_call(
        _w_in_layout_kernel,
        grid=(dep, d // rows),
        in_specs=[pl.BlockSpec((None, cols, rows), lambda i, j: (i, 0, j))],
        out_specs=pl.BlockSpec((None, rows, IN_COLS), lambda i, j: (i, j, 0)),
        out_shape=jax.ShapeDtypeStruct((dep, d, IN_COLS), BF16),
        compiler_params=_params("arbitrary", "arbitrary"),
        name="w_in_layout",
    )(jnp.swapaxes(w_in, 1, 2))


def _prep_params(norm1_g, norm2_g, w_in, gla_w_a2, gla_b_a2, gla_norm_g, fft_w, conv_w, conv_b, pool_w,
                 pool_scale, w_out, ffn_w_up, ffn_conv_w, ffn_conv_b, ffn_w_down):
    dep = w_in.shape[0]
    w_in_r = _w_in_layout_call(w_in)
    wa2 = jnp.zeros((dep, LANES, 2 * GLA_QK), F32)
    wa2 = wa2.at[:, 0:GLA_RANK, 0:GLA_QK].set(gla_w_a2[:, 0])
    wa2 = wa2.at[:, GLA_RANK:2 * GLA_RANK, GLA_QK:].set(gla_w_a2[:, 1])
    in_params = (norm1_g[:, None, :], w_in_r, wa2.astype(BF16), gla_b_a2.reshape(dep, 1, 2 * GLA_QK),
                 conv_w, conv_b[:, None, :], _block_diag(pool_w).astype(BF16), pool_scale[:, None, :])
    out_params = (jnp.tile(gla_norm_g, (1, GLA_HEADS))[:, None, :], w_out.astype(BF16), norm2_g[:, None, :],
                  ffn_w_up.astype(BF16), ffn_conv_w, ffn_conv_b[:, None, :], ffn_w_down.astype(BF16))
    return in_params, out_params, _block_diag(fft_w).astype(BF16)


def kernel(x, c, ctx, c_ctx, norm1_g, norm2_g, w_mod, b_mod, w_in, gla_w_a2, gla_b_a2, gla_norm_g,
           fft_w, conv_w, conv_b, pool_w, pool_scale, w_out, ffn_w_up, ffn_conv_w, ffn_conv_b,
           ffn_w_down, final_norm_g):
    batch, seq, d = x.shape
    ctx_len = ctx.shape[1]
    depth = w_mod.shape[0]
    assert seq % GLA_BLOCK == 0 and seq % FFT_N2 == 0 and ctx_len == TOKEN_TILE

    cc = jnp.concatenate([c, c_ctx[None, :], jnp.zeros((MOD_ROWS - batch - 1, d), F32)], axis=0)
    mod4 = _mod_call(cc, w_mod, b_mod).reshape(depth, MOD_ROWS, 6, d)
    in_params, out_params, wf_bd = _prep_params(
        norm1_g, norm2_g, w_in, gla_w_a2, gla_b_a2, gla_norm_g, fft_w, conv_w, conv_b, pool_w, pool_scale,
        w_out, ffn_w_up, ffn_conv_w, ffn_conv_b, ffn_w_down)

    xt = x.reshape(batch * seq, d)
    xc = ctx.reshape(batch * ctx_len, d)
    zero_state = jnp.zeros((batch, W_GLA, GLA_QK), F32)
    fing = final_norm_g.reshape(1, d)
    for i in range(depth):
        last = i == depth - 1
        kq, v, la, gate, ufft, ycp = _in_call(xc, mod4, i, batch, 1, in_params, row_len=ctx_len)
        o_f, o_b, s_f, s_b = _gla_call(kq, v, la, zero_state, zero_state, batch=batch)
        if not last:
            yfft = _fft_direct_call(ufft, wf_bd, i, batch=batch)
            xc = _out_call(xc, mod4, i, batch, 1, o_f, o_b, gate, yfft, ycp, out_params, fing,
                           row_len=ctx_len, final=False)

        kq, v, la, gate, ufft, ycp = _in_call(xt, mod4, i, 0, batch, in_params, row_len=GRID_W)
        o_f, o_b, _, _ = _gla_call(kq, v, la, s_f, s_b, batch=batch)
        yfft = _fft_latent_call(ufft, wf_bd, i, batch=batch)
        xt = _out_call(xt, mod4, i, 0, batch, o_f, o_b, gate, yfft, ycp, out_params, fing,
                       row_len=GRID_W, final=last)
    return xt.reshape(batch, seq, d)
```

```python
import functools

import numpy as np
import jax
import jax.numpy as jnp
from jax import lax
from jax.experimental import pallas as pl
from jax.experimental.pallas import tpu as pltpu

F32 = jnp.float32
BF16 = jnp.bfloat16

GRID_W = 64
EPS = 1e-6
GLA_HEADS = 4
GLA_DK = 32
GLA_DV = 64
GLA_QK = GLA_HEADS * GLA_DK
W_GLA = GLA_HEADS * GLA_DV
GLA_RANK = 16
GLA_TAU = 16.0
FFT_GROUPS = 4
FFT_DG = 64
W_FFT = FFT_GROUPS * FFT_DG
W_CONV = 256
POOL_WINDOWS = (2, 4, 8, 16)
POOL_DG = 64
W_POOL = len(POOL_WINDOWS) * POOL_DG
COL_SIZES = (GLA_QK, W_GLA, GLA_RANK, GLA_RANK, GLA_QK, W_GLA, W_FFT, W_CONV, W_CONV, W_CONV, W_POOL)
COL_STARTS = tuple(int(s) for s in np.cumsum((0,) + COL_SIZES)[:-1])

LANES = 128
VMEM_LIMIT_BYTES = 56 * 1024 * 1024

TOKEN_TILE = 256
IN_TILE = 1024
OUT_TILE = 512
GLA_CHUNK = 64
GLA_SUB = 8
GLA_SUBBLOCK = 256
GLA_BLOCK = 512
FFN_CHUNK = 256
FFT_N2 = 128
FFT_STAGE1_GROUP = 32
FFT_STAGE2_GROUP = 16
NEG_BIG = -1e30
LOG2_E = 1.4426950408889634

IN_K, IN_V, IN_Q, IN_G, IN_FFT, IN_H, IN_BG, IN_CG, IN_POOL, IN_A = (
    0, 128, 384, 512, 768, 1024, 1280, 1536, 1792, 2048)
IN_COLS = IN_A + LANES
MOD_ROWS = 8


def _const_spec(shape):
    nd = len(shape)
    return pl.BlockSpec(shape, lambda *_: (0,) * nd, pipeline_mode=pl.Buffered(1))


def _layer_spec(arr, layer):
    nd = arr.ndim
    return pl.BlockSpec((None,) + arr.shape[1:], lambda *_: (layer,) + (0,) * (nd - 1),
                        pipeline_mode=pl.Buffered(1))


def _mod_spec(mod4, layer, base, tiles_per_group):
    return pl.BlockSpec((None, None) + mod4.shape[2:], lambda i: (layer, base + i // tiles_per_group, 0, 0))


def _params(*sem):
    return pltpu.CompilerParams(dimension_semantics=sem, vmem_limit_bytes=VMEM_LIMIT_BYTES)


def _rms_mod(x, g, scale, shift):
    ms = jnp.mean(x * x, axis=-1, keepdims=True)
    return (x * lax.rsqrt(ms + EPS)) * g * (1.0 + scale) + shift


def _silu(a):
    return a * jax.nn.sigmoid(a)


def _mod_kernel(c_ref, w_ref, b_ref, o_ref):
    s = _silu(c_ref[...]).astype(BF16)
    o_ref[0] = jnp.dot(s, w_ref[0].astype(BF16), preferred_element_type=F32) + b_ref[0]


def _mod_call(cc, w_mod, b_mod):
    depth, d, n = w_mod.shape
    tn = 1536
    return pl.pallas_call(
        _mod_kernel,
        grid=(depth, n // tn),
        in_specs=[pl.BlockSpec(cc.shape, lambda i, j: (0, 0)),
                  pl.BlockSpec((1, d, tn), lambda i, j: (i, 0, j)),
                  pl.BlockSpec((1, 1, tn), lambda i, j: (i, 0, j))],
        out_specs=pl.BlockSpec((1, cc.shape[0], tn), lambda i, j: (i, 0, j)),
        out_shape=jax.ShapeDtypeStruct((depth, cc.shape[0], n), F32),
        compiler_params=_params("arbitrary", "arbitrary"),
        name="modulation",
    )(cc, w_mod, b_mod.reshape(depth, 1, n))


def _in_kernel(x_ref, mod_ref, g_ref, w_ref, wa2_ref, ba2_ref, cw_ref, cb_ref, cnt_ref, wpool_ref, pscale_ref,
               kq_ref, v_ref, la_ref, gate_ref, fft_ref, ycp_ref, *, row_len):
    nsub = x_ref.shape[0] // TOKEN_TILE

    def project(s):
        rows = slice(s * TOKEN_TILE, (s + 1) * TOKEN_TILE)
        h = _rms_mod(x_ref[rows, :], g_ref[...], mod_ref[1:2, :], mod_ref[0:1, :]).astype(BF16)
        return jnp.dot(h, w_ref[...], preferred_element_type=F32)

    p_next = project(0)
    for s in range(nsub):
        p, p_next = p_next, (project(s + 1) if s + 1 < nsub else None)
        _in_mixers(p, slice(s * TOKEN_TILE, (s + 1) * TOKEN_TILE), wa2_ref, ba2_ref, cw_ref, cb_ref, cnt_ref,
                   wpool_ref, pscale_ref, kq_ref, v_ref, la_ref, gate_ref, fft_ref, ycp_ref, row_len)


def _in_mixers(p, rows, wa2_ref, ba2_ref, cw_ref, cb_ref, cnt_ref, wpool_ref, pscale_ref,
               kq_ref, v_ref, la_ref, gate_ref, fft_ref, ycp_ref, row_len):
    t_rows = p.shape[0]
    kq_ref[rows, :GLA_QK] = p[:, IN_K:IN_K + GLA_QK]
    kq_ref[rows, GLA_QK:] = p[:, IN_Q:IN_Q + GLA_QK] * (GLA_DK ** -0.5)
    v_ref[rows, :] = p[:, IN_V:IN_V + W_GLA]
    gate_ref[rows, :] = p[:, IN_G:IN_G + W_GLA]
    fft_ref[rows, :] = p[:, IN_FFT:IN_FFT + W_FFT]
    z = jnp.dot(p[:, IN_A:IN_A + LANES].astype(BF16), wa2_ref[...], preferred_element_type=F32) + ba2_ref[...]
    la_ref[rows, :] = (jnp.minimum(z, 0.0) - jnp.log(1.0 + jnp.exp(-jnp.abs(z)))) * (1.0 / GLA_TAU)

    def pos(a):
        return lax.broadcasted_iota(jnp.int32, a.shape, 0) & (row_len - 1)

    def prev(a, s):
        return jnp.where(pos(a) >= s, pltpu.roll(a, s, 0), 0.0)

    def nxt(a, s):
        return jnp.where(pos(a) < row_len - s, pltpu.roll(a, t_rows - s, 0), 0.0)

    t = p[:, IN_CG:IN_CG + W_CONV] * p[:, IN_H:IN_H + W_CONV]
    cw = cw_ref[...]
    conv = prev(t, 1) * cw[0:1] + t * cw[1:2] + nxt(t, 1) * cw[2:3] + cb_ref[...]
    ycp_ref[rows, :W_CONV] = (p[:, IN_BG:IN_BG + W_CONV] * conv).astype(BF16)

    u = p[:, IN_POOL:IN_POOL + W_POOL]
    halves = []
    for side, steps in ((0, 1), (1, 3)):
        f = u[:, side * LANES:(side + 1) * LANES]
        g = prev(f, 1)
        sums = [g + f]
        for i in range(steps):
            f = f + nxt(f, 1 << i)
            g = g + prev(g, 1 << i)
            sums.append(g + f)
        lane = lax.broadcasted_iota(jnp.int32, f.shape, 1)
        halves.append(jnp.where(lane < POOL_DG, sums[-2], sums[-1]))
    tot = jnp.concatenate(halves, axis=1)
    pooled = tot / cnt_ref[...] - u
    yp = jnp.dot(pooled.astype(BF16), wpool_ref[...], preferred_element_type=F32) * pscale_ref[...]
    ycp_ref[rows, W_CONV:] = yp.astype(BF16)


def _pool_counts(t_rows, row_len):
    pos = np.arange(t_rows) % row_len
    cols = []
    for w in POOL_WINDOWS:
        lo = np.clip(pos - w // 2, 0, row_len - 1)
        hi = np.clip(pos + w // 2 - 1, 0, row_len - 1)
        cols.append(np.repeat((hi - lo + 1).astype(np.float32)[:, None], POOL_DG, axis=1))
    return np.concatenate(cols, axis=1)


def _in_call(xt, mod4, layer, mod_base, groups, params, *, row_len):
    n, d = xt.shape
    t = min(IN_TILE, n // groups)
    tiles_per_group = n // groups // t
    assert tiles_per_group * t * groups == n
    cnt = jnp.asarray(_pool_counts(TOKEN_TILE, row_len))
    row = lambda i: (i, 0)
    outs = [jax.ShapeDtypeStruct((n, 2 * GLA_QK), F32), jax.ShapeDtypeStruct((n, W_GLA), F32),
            jax.ShapeDtypeStruct((n, 2 * GLA_QK), F32), jax.ShapeDtypeStruct((n, W_GLA), F32),
            jax.ShapeDtypeStruct((n, W_FFT), F32), jax.ShapeDtypeStruct((n, W_CONV + W_POOL), BF16)]
    g, w, wa2, ba2, cw, cb, wpool, pscale = params
    return pl.pallas_call(
        functools.partial(_in_kernel, row_len=row_len),
        grid=(n // t,),
        in_specs=[pl.BlockSpec((t, d), row), _mod_spec(mod4, layer, mod_base, tiles_per_group),
                  _layer_spec(g, layer), _layer_spec(w, layer), _layer_spec(wa2, layer), _layer_spec(ba2, layer),
                  _layer_spec(cw, layer), _layer_spec(cb, layer), _const_spec(cnt.shape),
                  _layer_spec(wpool, layer), _layer_spec(pscale, layer)],
        out_specs=[pl.BlockSpec((t, o.shape[1]), row) for o in outs],
        out_shape=outs,
        compiler_params=_params("arbitrary"),
        name="in_proj",
    )(xt, mod4, g, w, wa2, ba2, cw, cb, cnt, wpool, pscale)


def _gla_block(kq_ref, v_ref, la_ref, s_ref, cum_ref, o_ref, consts, fwd):
    tri2, rexp, bdmask, hq, hv = consts
    k, q, v, la = kq_ref[:, :GLA_QK], kq_ref[:, GLA_QK:], v_ref[...], la_ref[...]
    t_rows = q.shape[0]
    c, sub = GLA_CHUNK, GLA_SUB
    nchunk = t_rows // c
    nt = (((1,), (1,)), ((), ()))
    tn = (((0,), (0,)), ((), ()))
    la_hi = la.astype(BF16)
    la_lo = (la - la_hi.astype(F32)).astype(BF16)
    cums = jnp.dot(tri2, jnp.concatenate([la_hi, la_lo], axis=1), preferred_element_type=F32)
    cum = (cums[:, :GLA_QK] + cums[:, GLA_QK:]) * LOG2_E
    cum_ref[...] = cum
    end = jnp.concatenate([jnp.broadcast_to(cum_ref[pl.ds(ci * c + (c - 1 if fwd else 0), 1), :], (c, GLA_QK))
                           for ci in range(nchunk)], axis=0)
    yield

    nblk = t_rows // sub

    def row_of_block(ref, width, j):
        cols = [jnp.concatenate([jnp.broadcast_to(ref[pl.ds(sub * blk + j, 1), c0:c0 + LANES], (sub, LANES))
                                 for blk in range(nblk)], axis=0) for c0 in range(0, width, LANES)]
        return cols[0] if len(cols) == 1 else jnp.concatenate(cols, axis=1)

    ii = lax.broadcasted_iota(jnp.int32, (t_rows, GLA_QK), 0) & (sub - 1)
    es = []
    for j in range(sub):
        keep = (ii >= j) if fwd else (ii <= j)
        w = jnp.exp2(jnp.where(keep, cum - row_of_block(cum_ref, GLA_QK, j), NEG_BIG))
        es.append((w * (q * row_of_block(kq_ref, GLA_QK, j))).astype(BF16))
    pr = jnp.dot(jnp.concatenate(es, axis=0), rexp, preferred_element_type=F32)
    yield

    qe = (q * jnp.exp2(cum)).astype(BF16)
    kd = (k * jnp.exp2(end - cum)).astype(BF16)
    v16 = v.astype(BF16)
    s = s_ref[0]
    o_parts = [None] * nchunk
    for ci in (range(nchunk) if fwd else reversed(range(nchunk))):
        rows = slice(ci * c, (ci + 1) * c)
        o_parts[ci] = lax.dot_general(qe[rows], s.astype(BF16), nt, preferred_element_type=F32)
        upd = lax.dot_general(v16[rows], kd[rows], tn, preferred_element_type=F32)
        s = s * jnp.exp2(end[ci * c:ci * c + 1]) + upd * bdmask
    s_ref[0] = s
    o = jnp.concatenate(o_parts, axis=0)

    half = t_rows // 2
    b = c // 2
    while b >= sub:
        npair = t_rows // (2 * b)
        refs = []
        for p in range(npair):
            r = 2 * b * p + (b - 1 if fwd else b)
            refs.append(jnp.broadcast_to(cum_ref[pl.ds(r, 1), :], (2 * b, GLA_QK)))
        e = jnp.exp2(-jnp.abs(cum - jnp.concatenate(refs, axis=0)))
        qfull, kfull = q * e, k * e
        first = [slice(2 * b * p, 2 * b * p + b) for p in range(npair)]
        second = [slice(2 * b * p + b, 2 * b * p + 2 * b) for p in range(npair)]
        qrows, krows = (second, first) if fwd else (first, second)
        qsel = jnp.concatenate([qfull[r] for r in qrows], axis=0)
        ksel = jnp.concatenate([kfull[r] for r in krows], axis=0).astype(BF16)
        vsel = jnp.concatenate([v[r] for r in krows], axis=0).astype(BF16)
        qst = jnp.concatenate([qsel * hq[h] for h in range(GLA_HEADS)], axis=0).astype(BF16)
        att = lax.dot_general(qst, ksel, nt, preferred_element_type=F32)
        ri = lax.broadcasted_iota(jnp.int32, att.shape, 0)
        ci = lax.broadcasted_iota(jnp.int32, att.shape, 1)
        att = jnp.where(((ri & (half - 1)) ^ ci) < b, att, 0.0).astype(BF16)
        att_k = jnp.concatenate([att[h * half:(h + 1) * half] for h in range(GLA_HEADS)], axis=1)
        v_k = jnp.concatenate([vsel * hv[h].astype(BF16) for h in range(GLA_HEADS)], axis=0)
        res = jnp.dot(att_k, v_k, preferred_element_type=F32)
        zero = jnp.zeros((b, W_GLA), F32)
        pieces = []
        for p in range(npair):
            piece = res[p * b:(p + 1) * b]
            pieces += [zero, piece] if fwd else [piece, zero]
        o = o + jnp.concatenate(pieces, axis=0)
        b //= 2
    yield

    for j in range(sub):
        o = o + pr[t_rows * j:t_rows * (j + 1)] * row_of_block(v_ref, W_GLA, j)
    o_ref[...] = o


def _gla_kernel(kqf_ref, vf_ref, laf_ref, kqb_ref, vb_ref, lab_ref, h0f_ref, h0b_ref,
                trif_ref, trib_ref, rexp_ref, bd_ref,
                of_ref, ob_ref, sf_ref, sb_ref, cum_ref):
    @pl.when(pl.program_id(1) == 0)
    def _():
        sf_ref[...] = h0f_ref[...]
        sb_ref[...] = h0b_ref[...]

    lq = lax.broadcasted_iota(jnp.int32, (1, GLA_QK), 1)
    lv = lax.broadcasted_iota(jnp.int32, (1, W_GLA), 1)
    hq = [jnp.where((lq >= h * GLA_DK) & (lq < (h + 1) * GLA_DK), 1.0, 0.0) for h in range(GLA_HEADS)]
    hv = [jnp.where((lv >= h * GLA_DV) & (lv < (h + 1) * GLA_DV), 1.0, 0.0) for h in range(GLA_HEADS)]
    rexp, bdmask = rexp_ref[...], bd_ref[...]
    nsub = kqf_ref.shape[0] // GLA_SUBBLOCK
    scans = []
    for r in range(nsub):
        lo, hi = pl.ds(r * GLA_SUBBLOCK, GLA_SUBBLOCK), pl.ds((nsub - 1 - r) * GLA_SUBBLOCK, GLA_SUBBLOCK)
        scans.append(_gla_block(kqf_ref.at[lo], vf_ref.at[lo], laf_ref.at[lo], sf_ref, cum_ref.at[2 * r],
                                of_ref.at[lo], (trif_ref[...], rexp, bdmask, hq, hv), True))
        scans.append(_gla_block(kqb_ref.at[hi], vb_ref.at[hi], lab_ref.at[hi], sb_ref, cum_ref.at[2 * r + 1],
                                ob_ref.at[hi], (trib_ref[...], rexp, bdmask, hq, hv), False))
    live = True
    while live:
        live = False
        for scan in scans:
            live = next(scan, "done") != "done" or live


def _gla_consts(t):
    c = GLA_CHUNK
    i = np.arange(t)
    same = (i[None, :] // c) == (i[:, None] // c)
    trif = (same & (i[None, :] <= i[:, None])).astype(np.float32)
    trib = (same & (i[None, :] >= i[:, None])).astype(np.float32)
    hk = np.arange(GLA_QK) // GLA_DK
    hd = np.arange(W_GLA) // GLA_DV
    rexp = (hk[:, None] == hd[None, :]).astype(np.float32)
    bd = (hd[:, None] == hk[None, :]).astype(np.float32)
    return (jnp.asarray(trif, BF16), jnp.asarray(trib, BF16), jnp.asarray(rexp, BF16), jnp.asarray(bd, F32))


def _gla_call(kq, v, la, h0f, h0b, *, batch):
    n = kq.shape[0]
    tg = min(GLA_BLOCK, n // batch)
    nb = n // batch // tg
    trif, trib, rexp, bd = _gla_consts(GLA_SUBBLOCK)
    fw = lambda b, i: (b * nb + i, 0)
    bw = lambda b, i: (b * nb + nb - 1 - i, 0)
    bw1 = lambda b, i: (b * nb + nb - 1 - i, 1)
    st = lambda b, i: (b, 0, 0)
    sshape = jax.ShapeDtypeStruct((batch, W_GLA, GLA_QK), F32)
    return pl.pallas_call(
        _gla_kernel,
        grid=(batch, nb),
        in_specs=[pl.BlockSpec((tg, 2 * GLA_QK), fw), pl.BlockSpec((tg, W_GLA), fw), pl.BlockSpec((tg, GLA_QK), fw),
                  pl.BlockSpec((tg, 2 * GLA_QK), bw), pl.BlockSpec((tg, W_GLA), bw), pl.BlockSpec((tg, GLA_QK), bw1),
                  pl.BlockSpec((1, W_GLA, GLA_QK), st), pl.BlockSpec((1, W_GLA, GLA_QK), st),
                  _const_spec(trif.shape), _const_spec(trib.shape), _const_spec(rexp.shape), _const_spec(bd.shape)],
        out_specs=[pl.BlockSpec((tg, W_GLA), fw), pl.BlockSpec((tg, W_GLA), bw),
                   pl.BlockSpec((1, W_GLA, GLA_QK), st), pl.BlockSpec((1, W_GLA, GLA_QK), st)],
        out_shape=[jax.ShapeDtypeStruct((n, W_GLA), F32), jax.ShapeDtypeStruct((n, W_GLA), F32), sshape, sshape],
        scratch_shapes=[pltpu.VMEM((2 * tg // GLA_SUBBLOCK, GLA_SUBBLOCK, GLA_QK), F32)],
        compiler_params=_params("arbitrary", "arbitrary"),
        name="gla",
    )(kq, v, la, kq, v, la, h0f, h0b, trif, trib, rexp, bd)


def _fft_tail(ab, cs_ref, wf_ref, norm):
    f = jnp.dot(ab, cs_ref[...].astype(BF16), preferred_element_type=F32) * norm
    return jnp.dot(f.astype(BF16), wf_ref[...], preferred_element_type=F32)


def _fft1_kernel(w_ref, u_ref, y_ref, rows_ref):
    n1, group, _ = u_ref.shape
    halves = rows_ref.shape[0]
    w = w_ref[...].astype(BF16)
    for h in range(halves):
        rows_ref[h] = u_ref[:, :, h * LANES:(h + 1) * LANES].reshape(n1 * group, LANES)
    for j in range(group):
        for h in range(halves):
            y_ref[h, j] = jnp.dot(w, rows_ref[h, pl.ds(j, n1, stride=group), :].astype(BF16),
                                  preferred_element_type=F32)


def _fft2_kernel(tw_ref, dft_ref, yre_ref, yim_ref, cs_ref, wf_ref, o_ref, rows_ref, *, norm):
    halves, n2, group, lanes = yre_ref.shape
    for p, r in enumerate((yre_ref, yim_ref)):
        for h in range(halves):
            rows_ref[p, h] = r[h].reshape(n2 * group, lanes)
    c2, s2 = dft_ref[0], dft_ref[1]
    parts = []
    for g in range(group):
        tc, ts = tw_ref[g, 0:1, :], tw_ref[g, 1:2, :]
        mc = c2 * tc - s2 * ts
        ms = s2 * tc + c2 * ts
        m = jnp.concatenate([jnp.concatenate([mc, ms], axis=1),
                             jnp.concatenate([-ms, mc], axis=1)], axis=0).astype(BF16)
        ys = jnp.concatenate(
            [jnp.concatenate([rows_ref[p, h, pl.ds(g, n2, stride=group), :] for h in range(halves)], axis=1)
             for p in range(2)], axis=0).astype(BF16)
        z = jnp.dot(m, ys, preferred_element_type=F32)
        parts.append(jnp.concatenate([z[:n2], z[n2:]], axis=1).astype(BF16))
    out = _fft_tail(jnp.concatenate(parts, axis=0), cs_ref, wf_ref, norm)
    for g in range(group):
        for h in range(halves):
            rows_ref[0, h, pl.ds(g, n2, stride=group), :] = out[g * n2:(g + 1) * n2, h * lanes:(h + 1) * lanes]
    for h in range(halves):
        o_ref[:, :, h * lanes:(h + 1) * lanes] = rows_ref[0, h].reshape(n2, group, lanes)


def _fft_direct_kernel(m_ref, u_ref, cs_ref, wf_ref, o_ref, *, norm):
    n = u_ref.shape[0]
    z = jnp.dot(m_ref[...].astype(BF16), u_ref[...].astype(BF16), preferred_element_type=F32)
    o_ref[...] = _fft_tail(jnp.concatenate([z[:n], z[n:]], axis=1).astype(BF16), cs_ref, wf_ref, norm)


def _dft_cos_sin(n):
    k = np.arange(n, dtype=np.int64)
    ang = 2.0 * np.pi * ((k[:, None] * k[None, :]) % n).astype(np.float64) / n
    return np.cos(ang), np.sin(ang)


def _channel_dft():
    c, s = _dft_cos_sin(FFT_DG)
    eye = np.eye(FFT_GROUPS)
    return jnp.asarray(np.concatenate([np.kron(eye, c), np.kron(eye, s)], axis=0), F32)


def _fft_latent_call(u, wf_bd, layer, *, batch):
    n = u.shape[0] // batch
    n1, n2 = n // FFT_N2, FFT_N2
    halves = W_FFT // LANES
    c1, s1 = _dft_cos_sin(n1)
    w1 = jnp.asarray(np.concatenate([c1, -s1], axis=0), F32)
    tw_ang = 2.0 * np.pi * (np.arange(n1, dtype=np.int64)[:, None] * np.arange(n2, dtype=np.int64)[None, :]) / n
    tw = jnp.asarray(np.stack([np.cos(tw_ang), np.sin(tw_ang)], axis=1), F32)
    dft2 = jnp.asarray(np.stack(_dft_cos_sin(n2), axis=0), F32)
    cs = _channel_dft()
    norm = float(1.0 / np.sqrt(n * FFT_DG))
    g1 = FFT_STAGE1_GROUP
    y = pl.pallas_call(
        _fft1_kernel,
        grid=(batch, n2 // g1),
        in_specs=[_const_spec(w1.shape), pl.BlockSpec((None, n1, g1, W_FFT), lambda b, j: (b, 0, j, 0))],
        out_specs=pl.BlockSpec((None, halves, g1, 2 * n1, LANES), lambda b, j: (b, 0, j, 0, 0)),
        out_shape=jax.ShapeDtypeStruct((batch, halves, n2, 2 * n1, LANES), F32),
        scratch_shapes=[pltpu.VMEM((halves, n1 * g1, LANES), F32)],
        compiler_params=_params("arbitrary", "arbitrary"),
        name="fft_stage1",
    )(w1, u.reshape(batch, n1, n2, W_FFT))
    g2 = min(n1, FFT_STAGE2_GROUP)
    nj = n1 // g2
    y_spec = lambda part: pl.BlockSpec((None, halves, n2, g2, LANES), lambda j, b: (b, 0, 0, part * nj + j, 0))
    out = pl.pallas_call(
        functools.partial(_fft2_kernel, norm=norm),
        grid=(nj, batch),
        in_specs=[pl.BlockSpec((g2, 2, n2), lambda j, b: (j, 0, 0)), _const_spec(dft2.shape), y_spec(0), y_spec(1),
                  _const_spec(cs.shape), _layer_spec(wf_bd, layer)],
        out_specs=pl.BlockSpec((None, n2, g2, W_FFT), lambda j, b: (b, 0, j, 0)),
        out_shape=jax.ShapeDtypeStruct((batch, n2, n1, W_FFT), F32),
        scratch_shapes=[pltpu.VMEM((2, halves, n2 * g2, LANES), F32)],
        compiler_params=_params("arbitrary", "arbitrary"),
        name="fft_stage2",
    )(tw, dft2, y, y, cs, wf_bd)
    return out.reshape(batch * n, W_FFT)


def _fft_direct_call(u, wf_bd, layer, *, batch):
    n = u.shape[0] // batch
    c, s = _dft_cos_sin(n)
    m = jnp.asarray(np.concatenate([c, -s], axis=0), F32)
    cs = _channel_dft()
    norm = float(1.0 / np.sqrt(n * FFT_DG))
    return pl.pallas_call(
        functools.partial(_fft_direct_kernel, norm=norm),
        grid=(batch,),
        in_specs=[_const_spec(m.shape), pl.BlockSpec((n, W_FFT), lambda b: (b, 0)),
                  _const_spec(cs.shape), _layer_spec(wf_bd, layer)],
        out_specs=pl.BlockSpec((n, W_FFT), lambda b: (b, 0)),
        out_shape=jax.ShapeDtypeStruct((batch * n, W_FFT), F32),
        compiler_params=_params("arbitrary"),
        name="fft_direct",
    )(m, u, cs, wf_bd)


def _out_kernel(x_ref, mod_ref, of_ref, ob_ref, gate_ref, fft_ref, ycp_ref, glag_ref, ones_ref, wout_ref,
                n2g_ref, wup_ref, cw_ref, cb_ref, wdn_ref, fing_ref, o_ref,
                x1_ref, h2_ref, au_ref, h_ref, acc_ref, *, row_len, final):
    _out_head(x_ref, mod_ref, of_ref, ob_ref, gate_ref, fft_ref, ycp_ref, glag_ref, ones_ref, wout_ref,
              n2g_ref, x1_ref, h2_ref)
    _out_ffn(mod_ref, wup_ref, cw_ref, cb_ref, wdn_ref, fing_ref, o_ref, x1_ref, h2_ref, au_ref, h_ref,
             acc_ref, row_len, final)


def _out_head(x_ref, mod_ref, of_ref, ob_ref, gate_ref, fft_ref, ycp_ref, glag_ref, ones_ref, wout_ref, n2g_ref,
              x1_ref, h2_ref):
    for r0 in range(0, x_ref.shape[0], TOKEN_TILE):
        rows = slice(r0, r0 + TOKEN_TILE)
        o = of_ref[rows, :] + ob_ref[rows, :]
        osq = o * o
        hi = osq.astype(BF16)
        lo = (osq - hi.astype(F32)).astype(BF16)
        ms = (jnp.dot(hi, ones_ref[...], preferred_element_type=F32)
              + jnp.dot(lo, ones_ref[...], preferred_element_type=F32)) * (1.0 / GLA_DV)
        gl = (o * lax.rsqrt(ms + EPS)) * glag_ref[...] * _silu(gate_ref[rows, :])
        ymix = jnp.concatenate([gl.astype(BF16), fft_ref[rows, :].astype(BF16), ycp_ref[rows, :]], axis=1)
        x1 = x_ref[rows, :] + mod_ref[2:3, :] * jnp.dot(ymix, wout_ref[...], preferred_element_type=F32)
        x1_ref[rows, :] = x1
        h2_ref[rows, :] = _rms_mod(x1, n2g_ref[...], mod_ref[4:5, :], mod_ref[3:4, :]).astype(BF16)


def _out_ffn(mod_ref, wup_ref, cw_ref, cb_ref, wdn_ref, fing_ref, o_ref, x1_ref, h2_ref, au_ref, h_ref, acc_ref,
             row_len, final):
    t_rows = o_ref.shape[0]
    d_ff = wdn_ref.shape[0]
    nchunk = d_ff // FFN_CHUNK
    acc_ref[...] = jnp.zeros(acc_ref.shape, F32)

    def up(c, slot, row_step=None):
        row_step = row_step or t_rows
        for r0 in range(0, t_rows, row_step):
            rows = slice(r0, r0 + row_step)
            for half in range(2):
                cols = slice(half * d_ff + c * FFN_CHUNK, half * d_ff + (c + 1) * FFN_CHUNK)
                au_ref[slot, rows, half * FFN_CHUNK:(half + 1) * FFN_CHUNK] = jnp.dot(
                    h2_ref[rows, :], wup_ref[:, cols], preferred_element_type=F32)

    def elem(c, slot):
        cols = slice(c * FFN_CHUNK, (c + 1) * FFN_CHUNK)
        pos = lax.broadcasted_iota(jnp.int32, (t_rows, FFN_CHUNK), 0) & (row_len - 1)
        a = au_ref[slot, :, :FFN_CHUNK]
        cw = cw_ref[:, cols]
        a = (jnp.where(pos == 0, 0.0, pltpu.roll(a, 1, 0)) * cw[0:1] + a * cw[1:2]
             + jnp.where(pos == row_len - 1, 0.0, pltpu.roll(a, t_rows - 1, 0)) * cw[2:3] + cb_ref[:, cols])
        h_ref[slot] = (_silu(a) * au_ref[slot, :, FFN_CHUNK:]).astype(BF16)

    def down(c, slot):
        rows = slice(c * FFN_CHUNK, (c + 1) * FFN_CHUNK)
        acc_ref[...] += jnp.dot(h_ref[slot], wdn_ref[rows, :], preferred_element_type=F32)

    def stage(c):
        up(c + 1, 0)
        elem(c, 1)
        down(c - 1, 0)
        up(c + 2, 1)
        elem(c + 1, 0)
        down(c, 1)

    assert nchunk % 2 == 1 and nchunk >= 3
    up(0, 0, TOKEN_TILE)
    up(1, 1, TOKEN_TILE)
    elem(0, 0)

    for i in range((nchunk - 3) // 2):
        stage(2 * i + 1)
    c = nchunk - 2
    up(c + 1, 0)
    elem(c, 1)
    down(c - 1, 0)
    elem(c + 1, 0)
    down(c, 1)
    down(c + 1, 0)
    x2 = x1_ref[...] + mod_ref[5:6, :] * acc_ref[...]
    if final:
        ms2 = jnp.mean(x2 * x2, axis=-1, keepdims=True)
        x2 = (x2 * lax.rsqrt(ms2 + EPS)) * fing_ref[...]
    o_ref[...] = x2


def _out_call(xt, mod4, layer, mod_base, groups, o_f, o_b, gate, yfft, ycp, params, fing, *, row_len, final):
    n, d = xt.shape
    t = OUT_TILE
    tiles_per_group = n // groups // t
    assert tiles_per_group * t * groups == n
    hd = np.arange(W_GLA) // GLA_DV
    ones = jnp.asarray((hd[:, None] == hd[None, :]).astype(np.float32), BF16)
    row = lambda i: (i, 0)
    acts = [xt, o_f, o_b, gate, yfft, ycp]
    glag, wout, n2g, wup, cw, cb, wdn = params
    consts = [glag, ones, wout, n2g, wup, cw, cb, wdn, fing]
    return pl.pallas_call(
        functools.partial(_out_kernel, row_len=row_len, final=final),
        grid=(n // t,),
        in_specs=[pl.BlockSpec((t, d), row), _mod_spec(mod4, layer, mod_base, tiles_per_group)]
                 + [pl.BlockSpec((t, a.shape[1]), row) for a in acts[1:]]
                 + [_const_spec(a.shape) if a is ones or a is fing else _layer_spec(a, layer) for a in consts],
        out_specs=pl.BlockSpec((t, d), row),
        out_shape=jax.ShapeDtypeStruct((n, d), F32),
        scratch_shapes=[pltpu.VMEM((t, d), F32), pltpu.VMEM((t, d), BF16),
                        pltpu.VMEM((2, t, 2 * FFN_CHUNK), F32), pltpu.VMEM((2, t, FFN_CHUNK), BF16),
                        pltpu.VMEM((t, d), F32)],
        compiler_params=_params("arbitrary"),
        name="out_ffn",
    )(xt, mod4, *acts[1:], *consts)


def _block_diag(w):
    dep, g, a, b = w.shape
    eye = jnp.eye(g, dtype=w.dtype)
    return (eye[None, :, None, :, None] * w[:, :, :, None, :]).reshape(dep, g * a, g * b)


def _w_in_layout_kernel(wt_ref, o_ref):
    a0, a1 = COL_STARTS[2], COL_STARTS[4]
    wt = wt_ref[...]
    pad = jnp.zeros((IN_COLS - wt.shape[0], wt.shape[1]), wt.dtype)
    o_ref[...] = jnp.concatenate([wt[:a0], wt[a1:], wt[a0:a1], pad], axis=0).T.astype(BF16)


def _w_in_layout_call(w_in):
    dep, d, cols = w_in.shape
    rows = TOKEN_TILE
    return pl.pallas_call(
        _w_in_layout_kernel,
        grid=(dep, d // rows),
        in_specs=[pl.BlockSpec((None, cols, rows), lambda i, j: (i, 0, j))],
        out_specs=pl.BlockSpec((None, rows, IN_COLS), lambda i, j: (i, j, 0)),
        out_shape=jax.ShapeDtypeStruct((dep, d, IN_COLS), BF16),
        compiler_params=_params("arbitrary", "arbitrary"),
        name="w_in_layout",
    )(jnp.swapaxes(w_in, 1, 2))


def _prep_params(norm1_g, norm2_g, w_in, gla_w_a2, gla_b_a2, gla_norm_g, fft_w, conv_w, conv_b, pool_w,
                 pool_scale, w_out, ffn_w_up, ffn_conv_w, ffn_conv_b, ffn_w_down):
    dep = w_in.shape[0]
    w_in_r = _w_in_layout_call(w_in)
    wa2 = jnp.zeros((dep, LANES, 2 * GLA_QK), F32)
    wa2 = wa2.at[:, 0:GLA_RANK, 0:GLA_QK].set(gla_w_a2[:, 0])
    wa2 = wa2.at[:, GLA_RANK:2 * GLA_RANK, GLA_QK:].set(gla_w_a2[:, 1])
    in_params = (norm1_g[:, None, :], w_in_r, wa2.astype(BF16), gla_b_a2.reshape(dep, 1, 2 * GLA_QK),
                 conv_w, conv_b[:, None, :], _block_diag(pool_w).astype(BF16), pool_scale[:, None, :])
    out_params = (jnp.tile(gla_norm_g, (1, GLA_HEADS))[:, None, :], w_out.astype(BF16), norm2_g[:, None, :],
                  ffn_w_up.astype(BF16), ffn_conv_w, ffn_conv_b[:, None, :], ffn_w_down.astype(BF16))
    return in_params, out_params, _block_diag(fft_w).astype(BF16)


def kernel(x, c, ctx, c_ctx, norm1_g, norm2_g, w_mod, b_mod, w_in, gla_w_a2, gla_b_a2, gla_norm_g,
           fft_w, conv_w, conv_b, pool_w, pool_scale, w_out, ffn_w_up, ffn_conv_w, ffn_conv_b,
           ffn_w_down, final_norm_g):
    batch, seq, d = x.shape
    ctx_len = ctx.shape[1]
    depth = w_mod.shape[0]
    assert seq % GLA_BLOCK == 0 and seq % FFT_N2 == 0 and ctx_len == TOKEN_TILE

    cc = jnp.concatenate([c, c_ctx[None, :], jnp.zeros((MOD_ROWS - batch - 1, d), F32)], axis=0)
    mod4 = _mod_call(cc, w_mod, b_mod).reshape(depth, MOD_ROWS, 6, d)
    in_params, out_params, wf_bd = _prep_params(
        norm1_g, norm2_g, w_in, gla_w_a2, gla_b_a2, gla_norm_g, fft_w, conv_w, conv_b, pool_w, pool_scale,
        w_out, ffn_w_up, ffn_conv_w, ffn_conv_b, ffn_w_down)

    xt = x.reshape(batch * seq, d)
    xc = ctx.reshape(batch * ctx_len, d)
    zero_state = jnp.zeros((batch, W_GLA, GLA_QK), F32)
    fing = final_norm_g.reshape(1, d)
    for i in range(depth):
        last = i == depth - 1
        kq, v, la, gate, ufft, ycp = _in_call(xc, mod4, i, batch, 1, in_params, row_len=ctx_len)
        o_f, o_b, s_f, s_b = _gla_call(kq, v, la, zero_state, zero_state, batch=batch)
        if not last:
            yfft = _fft_direct_call(ufft, wf_bd, i, batch=batch)
            xc = _out_call(xc, mod4, i, batch, 1, o_f, o_b, gate, yfft, ycp, out_params, fing,
                           row_len=ctx_len, final=False)

        kq, v, la, gate, ufft, ycp = _in_call(xt, mod4, i, 0, batch, in_params, row_len=GRID_W)
        o_f, o_b, _, _ = _gla_call(kq, v, la, s_f, s_b, batch=batch)
        yfft = _fft_latent_call(ufft, wf_bd, i, batch=batch)
        xt = _out_call(xt, mod4, i, 0, batch, o_f, o_b, gate, yfft, ycp, out_params, fing,
                       row_len=GRID_W, final=last)
    return xt.reshape(batch, seq, d)
```

```python
import functools

import numpy as np
import jax
import jax.numpy as jnp
from jax import lax
from jax.experimental import pallas as pl
from jax.experimental.pallas import tpu as pltpu

F32 = jnp.float32
BF16 = jnp.bfloat16

GRID_W = 64
EPS = 1e-6
GLA_HEADS = 4
GLA_DK = 32
GLA_DV = 64
GLA_QK = GLA_HEADS * GLA_DK
W_GLA = GLA_HEADS * GLA_DV
GLA_RANK = 16
GLA_TAU = 16.0
FFT_GROUPS = 4
FFT_DG = 64
W_FFT = FFT_GROUPS * FFT_DG
W_CONV = 256
POOL_WINDOWS = (2, 4, 8, 16)
POOL_DG = 64
W_POOL = len(POOL_WINDOWS) * POOL_DG
COL_SIZES = (GLA_QK, W_GLA, GLA_RANK, GLA_RANK, GLA_QK, W_GLA, W_FFT, W_CONV, W_CONV, W_CONV, W_POOL)
COL_STARTS = tuple(int(s) for s in np.cumsum((0,) + COL_SIZES)[:-1])

LANES = 128
VMEM_LIMIT_BYTES = 56 * 1024 * 1024

TOKEN_TILE = 256
IN_TILE = 1024
OUT_TILE = 512
GLA_CHUNK = 64
GLA_SUB = 8
GLA_SUBBLOCK = 256
GLA_BLOCK = 512
FFN_CHUNK = 256
FFT_N2 = 128
FFT_GROUP = 8
NEG_BIG = -1e30
LOG2_E = 1.4426950408889634

IN_K, IN_V, IN_Q, IN_G, IN_FFT, IN_H, IN_BG, IN_CG, IN_POOL, IN_A = (
    0, 128, 384, 512, 768, 1024, 1280, 1536, 1792, 2048)
IN_COLS = IN_A + LANES
MOD_ROWS = 8


def _const_spec(shape):
    nd = len(shape)
    return pl.BlockSpec(shape, lambda *_: (0,) * nd, pipeline_mode=pl.Buffered(1))


def _layer_spec(arr, layer):
    nd = arr.ndim
    return pl.BlockSpec((None,) + arr.shape[1:], lambda *_: (layer,) + (0,) * (nd - 1),
                        pipeline_mode=pl.Buffered(1))


def _mod_spec(mod4, layer, base, tiles_per_group):
    return pl.BlockSpec((None, None) + mod4.shape[2:], lambda i: (layer, base + i // tiles_per_group, 0, 0))


def _params(*sem):
    return pltpu.CompilerParams(dimension_semantics=sem, vmem_limit_bytes=VMEM_LIMIT_BYTES)


def _rms_mod(x, g, scale, shift):
    ms = jnp.mean(x * x, axis=-1, keepdims=True)
    return (x * lax.rsqrt(ms + EPS)) * g * (1.0 + scale) + shift


def _silu(a):
    return a * jax.nn.sigmoid(a)


def _mod_kernel(c_ref, w_ref, b_ref, o_ref):
    s = _silu(c_ref[...]).astype(BF16)
    o_ref[0] = jnp.dot(s, w_ref[0].astype(BF16), preferred_element_type=F32) + b_ref[0]


def _mod_call(cc, w_mod, b_mod):
    depth, d, n = w_mod.shape
    tn = 1536
    return pl.pallas_call(
        _mod_kernel,
        grid=(depth, n // tn),
        in_specs=[pl.BlockSpec(cc.shape, lambda i, j: (0, 0)),
                  pl.BlockSpec((1, d, tn), lambda i, j: (i, 0, j)),
                  pl.BlockSpec((1, 1, tn), lambda i, j: (i, 0, j))],
        out_specs=pl.BlockSpec((1, cc.shape[0], tn), lambda i, j: (i, 0, j)),
        out_shape=jax.ShapeDtypeStruct((depth, cc.shape[0], n), F32),
        compiler_params=_params("arbitrary", "arbitrary"),
        name="modulation",
    )(cc, w_mod, b_mod.reshape(depth, 1, n))


def _in_kernel(x_ref, mod_ref, g_ref, w_ref, wa2_ref, ba2_ref, cw_ref, cb_ref, cnt_ref, wpool_ref, pscale_ref,
               kq_ref, v_ref, la_ref, gate_ref, fft_ref, ycp_ref, *, row_len):
    nsub = x_ref.shape[0] // TOKEN_TILE

    def project(s):
        rows = slice(s * TOKEN_TILE, (s + 1) * TOKEN_TILE)
        h = _rms_mod(x_ref[rows, :], g_ref[...], mod_ref[1:2, :], mod_ref[0:1, :]).astype(BF16)
        return jnp.dot(h, w_ref[...], preferred_element_type=F32)

    p_next = project(0)
    for s in range(nsub):
        p, p_next = p_next, (project(s + 1) if s + 1 < nsub else None)
        _in_mixers(p, slice(s * TOKEN_TILE, (s + 1) * TOKEN_TILE), wa2_ref, ba2_ref, cw_ref, cb_ref, cnt_ref,
                   wpool_ref, pscale_ref, kq_ref, v_ref, la_ref, gate_ref, fft_ref, ycp_ref, row_len)


def _in_mixers(p, rows, wa2_ref, ba2_ref, cw_ref, cb_ref, cnt_ref, wpool_ref, pscale_ref,
               kq_ref, v_ref, la_ref, gate_ref, fft_ref, ycp_ref, row_len):
    t_rows = p.shape[0]
    kq_ref[rows, :GLA_QK] = p[:, IN_K:IN_K + GLA_QK]
    kq_ref[rows, GLA_QK:] = p[:, IN_Q:IN_Q + GLA_QK] * (GLA_DK ** -0.5)
    v_ref[rows, :] = p[:, IN_V:IN_V + W_GLA]
    gate_ref[rows, :] = p[:, IN_G:IN_G + W_GLA]
    fft_ref[rows, :] = p[:, IN_FFT:IN_FFT + W_FFT]
    z = jnp.dot(p[:, IN_A:IN_A + LANES].astype(BF16), wa2_ref[...], preferred_element_type=F32) + ba2_ref[...]
    la_ref[rows, :] = (jnp.minimum(z, 0.0) - jnp.log(1.0 + jnp.exp(-jnp.abs(z)))) * (1.0 / GLA_TAU)

    def pos(a):
        return lax.broadcasted_iota(jnp.int32, a.shape, 0) & (row_len - 1)

    def prev(a, s):
        return jnp.where(pos(a) >= s, pltpu.roll(a, s, 0), 0.0)

    def nxt(a, s):
        return jnp.where(pos(a) < row_len - s, pltpu.roll(a, t_rows - s, 0), 0.0)

    t = p[:, IN_CG:IN_CG + W_CONV] * p[:, IN_H:IN_H + W_CONV]
    cw = cw_ref[...]
    conv = prev(t, 1) * cw[0:1] + t * cw[1:2] + nxt(t, 1) * cw[2:3] + cb_ref[...]
    ycp_ref[rows, :W_CONV] = (p[:, IN_BG:IN_BG + W_CONV] * conv).astype(BF16)

    u = p[:, IN_POOL:IN_POOL + W_POOL]
    halves = []
    for side, steps in ((0, 1), (1, 3)):
        f = u[:, side * LANES:(side + 1) * LANES]
        g = prev(f, 1)
        sums = [g + f]
        for i in range(steps):
            f = f + nxt(f, 1 << i)
            g = g + prev(g, 1 << i)
            sums.append(g + f)
        lane = lax.broadcasted_iota(jnp.int32, f.shape, 1)
        halves.append(jnp.where(lane < POOL_DG, sums[-2], sums[-1]))
    tot = jnp.concatenate(halves, axis=1)
    pooled = tot / cnt_ref[...] - u
    yp = jnp.dot(pooled.astype(BF16), wpool_ref[...], preferred_element_type=F32) * pscale_ref[...]
    ycp_ref[rows, W_CONV:] = yp.astype(BF16)


def _pool_counts(t_rows, row_len):
    pos = np.arange(t_rows) % row_len
    cols = []
    for w in POOL_WINDOWS:
        lo = np.clip(pos - w // 2, 0, row_len - 1)
        hi = np.clip(pos + w // 2 - 1, 0, row_len - 1)
        cols.append(np.repeat((hi - lo + 1).astype(np.float32)[:, None], POOL_DG, axis=1))
    return np.concatenate(cols, axis=1)


def _in_call(xt, mod4, layer, mod_base, groups, params, *, row_len):
    n, d = xt.shape
    t = min(IN_TILE, n // groups)
    tiles_per_group = n // groups // t
    assert tiles_per_group * t * groups == n
    cnt = jnp.asarray(_pool_counts(TOKEN_TILE, row_len))
    row = lambda i: (i, 0)
    outs = [jax.ShapeDtypeStruct((n, 2 * GLA_QK), F32), jax.ShapeDtypeStruct((n, W_GLA), F32),
            jax.ShapeDtypeStruct((n, 2 * GLA_QK), F32), jax.ShapeDtypeStruct((n, W_GLA), F32),
            jax.ShapeDtypeStruct((n, W_FFT), F32), jax.ShapeDtypeStruct((n, W_CONV + W_POOL), BF16)]
    g, w, wa2, ba2, cw, cb, wpool, pscale = params
    return pl.pallas_call(
        functools.partial(_in_kernel, row_len=row_len),
        grid=(n // t,),
        in_specs=[pl.BlockSpec((t, d), row), _mod_spec(mod4, layer, mod_base, tiles_per_group),
                  _layer_spec(g, layer), _layer_spec(w, layer), _layer_spec(wa2, layer), _layer_spec(ba2, layer),
                  _layer_spec(cw, layer), _layer_spec(cb, layer), _const_spec(cnt.shape),
                  _layer_spec(wpool, layer), _layer_spec(pscale, layer)],
        out_specs=[pl.BlockSpec((t, o.shape[1]), row) for o in outs],
        out_shape=outs,
        compiler_params=_params("arbitrary"),
        name="in_proj",
    )(xt, mod4, g, w, wa2, ba2, cw, cb, cnt, wpool, pscale)


def _gla_block(kq_ref, v_ref, la_ref, s_ref, cum_ref, o_ref, consts, fwd):
    tri2, rexp, bdmask, hq, hv = consts
    k, q, v, la = kq_ref[:, :GLA_QK], kq_ref[:, GLA_QK:], v_ref[...], la_ref[...]
    t_rows = q.shape[0]
    c, sub = GLA_CHUNK, GLA_SUB
    nchunk = t_rows // c
    nt = (((1,), (1,)), ((), ()))
    tn = (((0,), (0,)), ((), ()))
    la_hi = la.astype(BF16)
    la_lo = (la - la_hi.astype(F32)).astype(BF16)
    cums = jnp.dot(tri2, jnp.concatenate([la_hi, la_lo], axis=1), preferred_element_type=F32)
    cum = (cums[:, :GLA_QK] + cums[:, GLA_QK:]) * LOG2_E
    cum_ref[...] = cum
    end = jnp.concatenate([jnp.broadcast_to(cum_ref[pl.ds(ci * c + (c - 1 if fwd else 0), 1), :], (c, GLA_QK))
                           for ci in range(nchunk)], axis=0)
    yield

    nblk = t_rows // sub

    def row_of_block(ref, width, j):
        cols = [jnp.concatenate([jnp.broadcast_to(ref[pl.ds(sub * blk + j, 1), c0:c0 + LANES], (sub, LANES))
                                 for blk in range(nblk)], axis=0) for c0 in range(0, width, LANES)]
        return cols[0] if len(cols) == 1 else jnp.concatenate(cols, axis=1)

    ii = lax.broadcasted_iota(jnp.int32, (t_rows, GLA_QK), 0) & (sub - 1)
    es = []
    for j in range(sub):
        keep = (ii >= j) if fwd else (ii <= j)
        w = jnp.exp2(jnp.where(keep, cum - row_of_block(cum_ref, GLA_QK, j), NEG_BIG))
        es.append((w * (q * row_of_block(kq_ref, GLA_QK, j))).astype(BF16))
    pr = jnp.dot(jnp.concatenate(es, axis=0), rexp, preferred_element_type=F32)
    yield

    qe = (q * jnp.exp2(cum)).astype(BF16)
    kd = (k * jnp.exp2(end - cum)).astype(BF16)
    v16 = v.astype(BF16)
    s = s_ref[0]
    o_parts = [None] * nchunk
    for ci in (range(nchunk) if fwd else reversed(range(nchunk))):
        rows = slice(ci * c, (ci + 1) * c)
        o_parts[ci] = lax.dot_general(qe[rows], s.astype(BF16), nt, preferred_element_type=F32)
        upd = lax.dot_general(v16[rows], kd[rows], tn, preferred_element_type=F32)
        s = s * jnp.exp2(end[ci * c:ci * c + 1]) + upd * bdmask
    s_ref[0] = s
    o = jnp.concatenate(o_parts, axis=0)

    half = t_rows // 2
    b = c // 2
    while b >= sub:
        npair = t_rows // (2 * b)
        refs = []
        for p in range(npair):
            r = 2 * b * p + (b - 1 if fwd else b)
            refs.append(jnp.broadcast_to(cum_ref[pl.ds(r, 1), :], (2 * b, GLA_QK)))
        e = jnp.exp2(-jnp.abs(cum - jnp.concatenate(refs, axis=0)))
        qfull, kfull = q * e, k * e
        first = [slice(2 * b * p, 2 * b * p + b) for p in range(npair)]
        second = [slice(2 * b * p + b, 2 * b * p + 2 * b) for p in range(npair)]
        qrows, krows = (second, first) if fwd else (first, second)
        qsel = jnp.concatenate([qfull[r] for r in qrows], axis=0)
        ksel = jnp.concatenate([kfull[r] for r in krows], axis=0).astype(BF16)
        vsel = jnp.concatenate([v[r] for r in krows], axis=0).astype(BF16)
        qst = jnp.concatenate([qsel * hq[h] for h in range(GLA_HEADS)], axis=0).astype(BF16)
        att = lax.dot_general(qst, ksel, nt, preferred_element_type=F32)
        ri = lax.broadcasted_iota(jnp.int32, att.shape, 0)
        ci = lax.broadcasted_iota(jnp.int32, att.shape, 1)
        att = jnp.where(((ri & (half - 1)) ^ ci) < b, att, 0.0).astype(BF16)
        att_k = jnp.concatenate([att[h * half:(h + 1) * half] for h in range(GLA_HEADS)], axis=1)
        v_k = jnp.concatenate([vsel * hv[h].astype(BF16) for h in range(GLA_HEADS)], axis=0)
        res = jnp.dot(att_k, v_k, preferred_element_type=F32)
        zero = jnp.zeros((b, W_GLA), F32)
        pieces = []
        for p in range(npair):
            piece = res[p * b:(p + 1) * b]
            pieces += [zero, piece] if fwd else [piece, zero]
        o = o + jnp.concatenate(pieces, axis=0)
        b //= 2
    yield

    for j in range(sub):
        o = o + pr[t_rows * j:t_rows * (j + 1)] * row_of_block(v_ref, W_GLA, j)
    o_ref[...] = o


def _gla_kernel(kqf_ref, vf_ref, laf_ref, kqb_ref, vb_ref, lab_ref, h0f_ref, h0b_ref,
                trif_ref, trib_ref, rexp_ref, bd_ref,
                of_ref, ob_ref, sf_ref, sb_ref, cum_ref):
    @pl.when(pl.program_id(1) == 0)
    def _():
        sf_ref[...] = h0f_ref[...]
        sb_ref[...] = h0b_ref[...]

    lq = lax.broadcasted_iota(jnp.int32, (1, GLA_QK), 1)
    lv = lax.broadcasted_iota(jnp.int32, (1, W_GLA), 1)
    hq = [jnp.where((lq >= h * GLA_DK) & (lq < (h + 1) * GLA_DK), 1.0, 0.0) for h in range(GLA_HEADS)]
    hv = [jnp.where((lv >= h * GLA_DV) & (lv < (h + 1) * GLA_DV), 1.0, 0.0) for h in range(GLA_HEADS)]
    rexp, bdmask = rexp_ref[...], bd_ref[...]
    nsub = kqf_ref.shape[0] // GLA_SUBBLOCK
    scans = []
    for r in range(nsub):
        lo, hi = pl.ds(r * GLA_SUBBLOCK, GLA_SUBBLOCK), pl.ds((nsub - 1 - r) * GLA_SUBBLOCK, GLA_SUBBLOCK)
        scans.append(_gla_block(kqf_ref.at[lo], vf_ref.at[lo], laf_ref.at[lo], sf_ref, cum_ref.at[2 * r],
                                of_ref.at[lo], (trif_ref[...], rexp, bdmask, hq, hv), True))
        scans.append(_gla_block(kqb_ref.at[hi], vb_ref.at[hi], lab_ref.at[hi], sb_ref, cum_ref.at[2 * r + 1],
                                ob_ref.at[hi], (trib_ref[...], rexp, bdmask, hq, hv), False))
    live = True
    while live:
        live = False
        for scan in scans:
            live = next(scan, "done") != "done" or live


def _gla_consts(t):
    c = GLA_CHUNK
    i = np.arange(t)
    same = (i[None, :] // c) == (i[:, None] // c)
    trif = (same & (i[None, :] <= i[:, None])).astype(np.float32)
    trib = (same & (i[None, :] >= i[:, None])).astype(np.float32)
    hk = np.arange(GLA_QK) // GLA_DK
    hd = np.arange(W_GLA) // GLA_DV
    rexp = (hk[:, None] == hd[None, :]).astype(np.float32)
    bd = (hd[:, None] == hk[None, :]).astype(np.float32)
    return (jnp.asarray(trif, BF16), jnp.asarray(trib, BF16), jnp.asarray(rexp, BF16), jnp.asarray(bd, F32))


def _gla_call(kq, v, la, h0f, h0b, *, batch):
    n = kq.shape[0]
    tg = min(GLA_BLOCK, n // batch)
    nb = n // batch // tg
    trif, trib, rexp, bd = _gla_consts(GLA_SUBBLOCK)
    fw = lambda b, i: (b * nb + i, 0)
    bw = lambda b, i: (b * nb + nb - 1 - i, 0)
    bw1 = lambda b, i: (b * nb + nb - 1 - i, 1)
    st = lambda b, i: (b, 0, 0)
    sshape = jax.ShapeDtypeStruct((batch, W_GLA, GLA_QK), F32)
    return pl.pallas_call(
        _gla_kernel,
        grid=(batch, nb),
        in_specs=[pl.BlockSpec((tg, 2 * GLA_QK), fw), pl.BlockSpec((tg, W_GLA), fw), pl.BlockSpec((tg, GLA_QK), fw),
                  pl.BlockSpec((tg, 2 * GLA_QK), bw), pl.BlockSpec((tg, W_GLA), bw), pl.BlockSpec((tg, GLA_QK), bw1),
                  pl.BlockSpec((1, W_GLA, GLA_QK), st), pl.BlockSpec((1, W_GLA, GLA_QK), st),
                  _const_spec(trif.shape), _const_spec(trib.shape), _const_spec(rexp.shape), _const_spec(bd.shape)],
        out_specs=[pl.BlockSpec((tg, W_GLA), fw), pl.BlockSpec((tg, W_GLA), bw),
                   pl.BlockSpec((1, W_GLA, GLA_QK), st), pl.BlockSpec((1, W_GLA, GLA_QK), st)],
        out_shape=[jax.ShapeDtypeStruct((n, W_GLA), F32), jax.ShapeDtypeStruct((n, W_GLA), F32), sshape, sshape],
        scratch_shapes=[pltpu.VMEM((2 * tg // GLA_SUBBLOCK, GLA_SUBBLOCK, GLA_QK), F32)],
        compiler_params=_params("arbitrary", "arbitrary"),
        name="gla",
    )(kq, v, la, kq, v, la, h0f, h0b, trif, trib, rexp, bd)


def _fft_tail(ab, cs_ref, wf_ref, norm):
    f = jnp.dot(ab, cs_ref[...].astype(BF16), preferred_element_type=F32) * norm
    return jnp.dot(f.astype(BF16), wf_ref[...], preferred_element_type=F32)


def _fft_kernel(kron_ref, tw_ref, dft_ref, cs_ref, wf_ref, u_ref, o_ref, y_ref, rows_ref, *, norm):
    n1, n2, width = u_ref.shape
    halves = rows_ref.shape[0]
    g = FFT_GROUP
    kron = kron_ref[...].astype(BF16)

    def stage1(blk, carry):
        cols = pl.ds(pl.multiple_of(blk * g, g), g)
        ub = u_ref[:, cols, :].reshape(n1 * g, width).astype(BF16)
        y_ref[:, cols, :] = jnp.dot(kron, ub, preferred_element_type=F32).reshape(2 * n1, g, width)
        return carry

    lax.fori_loop(0, n2 // g, stage1, 0)
    c2, s2 = dft_ref[0], dft_ref[1]

    def stage2(grp, carry):
        parts = []
        for j in range(g):
            k1 = grp * g + j
            tc, ts = tw_ref[k1, 0:1, :], tw_ref[k1, 1:2, :]
            mc = c2 * tc - s2 * ts
            ms = s2 * tc + c2 * ts
            m = jnp.concatenate([jnp.concatenate([mc, ms], axis=1),
                                 jnp.concatenate([-ms, mc], axis=1)], axis=0).astype(BF16)
            ys = jnp.concatenate([y_ref[k1], y_ref[n1 + k1]], axis=0).astype(BF16)
            z = jnp.dot(m, ys, preferred_element_type=F32)
            parts.append(jnp.concatenate([z[:n2], z[n2:]], axis=1).astype(BF16))
        out = _fft_tail(jnp.concatenate(parts, axis=0), cs_ref, wf_ref, norm)
        for j in range(g):
            for h in range(halves):
                rows_ref[h, pl.ds(j, n2, stride=g), :] = out[j * n2:(j + 1) * n2, h * LANES:(h + 1) * LANES]
        cols = pl.ds(pl.multiple_of(grp * g, g), g)
        for h in range(halves):
            o_ref[:, cols, h * LANES:(h + 1) * LANES] = rows_ref[h].reshape(n2, g, LANES)
        return carry

    lax.fori_loop(0, n1 // g, stage2, 0)


def _fft_direct_kernel(m_ref, u_ref, cs_ref, wf_ref, o_ref, *, norm):
    n = u_ref.shape[0]
    z = jnp.dot(m_ref[...].astype(BF16), u_ref[...].astype(BF16), preferred_element_type=F32)
    o_ref[...] = _fft_tail(jnp.concatenate([z[:n], z[n:]], axis=1).astype(BF16), cs_ref, wf_ref, norm)


def _dft_cos_sin(n):
    k = np.arange(n, dtype=np.int64)
    ang = 2.0 * np.pi * ((k[:, None] * k[None, :]) % n).astype(np.float64) / n
    return np.cos(ang), np.sin(ang)


def _channel_dft():
    c, s = _dft_cos_sin(FFT_DG)
    eye = np.eye(FFT_GROUPS)
    return jnp.asarray(np.concatenate([np.kron(eye, c), np.kron(eye, s)], axis=0), F32)


def _fft_latent_call(u, wf_bd, layer, *, batch):
    n = u.shape[0] // batch
    n1, n2 = n // FFT_N2, FFT_N2
    halves = W_FFT // LANES
    g = FFT_GROUP
    assert n1 % g == 0 and n2 % g == 0
    c1, s1 = _dft_cos_sin(n1)
    kron =jnp.asarray(np.kron(np.concatenate([c1, -s1], axis=0), np.eye(g)), F32)
    tw_ang = 2.0 * np.pi * (np.arange(n1, dtype=np.int64)[:, None] * np.arange(n2, dtype=np.int64)[None, :]) / n
    tw = jnp.asarray(np.stack([np.cos(tw_ang), np.sin(tw_ang)], axis=1), F32)
    dft2 = jnp.asarray(np.stack(_dft_cos_sin(n2), axis=0), F32)
    cs = _channel_dft()
    norm = float(1.0 / np.sqrt(n * FFT_DG))
    out = pl.pallas_call(
        functools.partial(_fft_kernel, norm=norm),
        grid=(batch,),
        in_specs=[_const_spec(kron.shape), _const_spec(tw.shape), _const_spec(dft2.shape), _const_spec(cs.shape),
                  _layer_spec(wf_bd, layer),
                  pl.BlockSpec((None, n1, n2, W_FFT), lambda b: (b, 0, 0, 0), pipeline_mode=pl.Buffered(1))],
        out_specs=pl.BlockSpec((None, n2, n1, W_FFT), lambda b: (b, 0, 0, 0)),
        out_shape=jax.ShapeDtypeStruct((batch, n2, n1, W_FFT), F32),
        scratch_shapes=[pltpu.VMEM((2 * n1, n2, W_FFT), F32), pltpu.VMEM((halves, n2 * g, LANES), F32)],
        compiler_params=_params("arbitrary"),
        name="fft",
    )(kron, tw, dft2, cs, wf_bd, u.reshape(batch, n1, n2, W_FFT))
    return out.reshape(batch * n, W_FFT)


def _fft_direct_call(u, wf_bd, layer, *, batch):
    n = u.shape[0] // batch
    c, s = _dft_cos_sin(n)
    m = jnp.asarray(np.concatenate([c, -s], axis=0), F32)
    cs = _channel_dft()
    norm = float(1.0 / np.sqrt(n * FFT_DG))
    return pl.pallas_call(
        functools.partial(_fft_direct_kernel, norm=norm),
        grid=(batch,),
        in_specs=[_const_spec(m.shape), pl.BlockSpec((n, W_FFT), lambda b: (b, 0)),
                  _const_spec(cs.shape), _layer_spec(wf_bd, layer)],
        out_specs=pl.BlockSpec((n, W_FFT), lambda b: (b, 0)),
        out_shape=jax.ShapeDtypeStruct((batch * n, W_FFT), F32),
        compiler_params=_params("arbitrary"),
        name="fft_direct",
    )(m, u, cs, wf_bd)


def _out_kernel(x_ref, mod_ref, of_ref, ob_ref, gate_ref, fft_ref, ycp_ref, glag_ref, ones_ref, wout_ref,
                n2g_ref, wup_ref, cw_ref, cb_ref, wdn_ref, fing_ref, o_ref,
                x1_ref, h2_ref, au_ref, h_ref, acc_ref, *, row_len, final):
    _out_head(x_ref, mod_ref, of_ref, ob_ref, gate_ref, fft_ref, ycp_ref, glag_ref, ones_ref, wout_ref,
              n2g_ref, x1_ref, h2_ref)
    _out_ffn(mod_ref, wup_ref, cw_ref, cb_ref, wdn_ref, fing_ref, o_ref, x1_ref, h2_ref, au_ref, h_ref,
             acc_ref, row_len, final)


def _out_head(x_ref, mod_ref, of_ref, ob_ref, gate_ref, fft_ref, ycp_ref, glag_ref, ones_ref, wout_ref, n2g_ref,
              x1_ref, h2_ref):
    for r0 in range(0, x_ref.shape[0], TOKEN_TILE):
        rows = slice(r0, r0 + TOKEN_TILE)
        o = of_ref[rows, :] + ob_ref[rows, :]
        osq = o * o
        hi = osq.astype(BF16)
        lo = (osq - hi.astype(F32)).astype(BF16)
        ms = (jnp.dot(hi, ones_ref[...], preferred_element_type=F32)
              + jnp.dot(lo, ones_ref[...], preferred_element_type=F32)) * (1.0 / GLA_DV)
        gl = (o * lax.rsqrt(ms + EPS)) * glag_ref[...] * _silu(gate_ref[rows, :])
        ymix = jnp.concatenate([gl.astype(BF16), fft_ref[rows, :].astype(BF16), ycp_ref[rows, :]], axis=1)
        x1 = x_ref[rows, :] + mod_ref[2:3, :] * jnp.dot(ymix, wout_ref[...], preferred_element_type=F32)
        x1_ref[rows, :] = x1
        h2_ref[rows, :] = _rms_mod(x1, n2g_ref[...], mod_ref[4:5, :], mod_ref[3:4, :]).astype(BF16)


def _out_ffn(mod_ref, wup_ref, cw_ref, cb_ref, wdn_ref, fing_ref, o_ref, x1_ref, h2_ref, au_ref, h_ref, acc_ref,
             row_len, final):
    t_rows = o_ref.shape[0]
    d_ff = wdn_ref.shape[0]
    nchunk = d_ff // FFN_CHUNK
    acc_ref[...] = jnp.zeros(acc_ref.shape, F32)

    def up(c, slot, row_step=None):
        row_step = row_step or t_rows
        for r0 in range(0, t_rows, row_step):
            rows = slice(r0, r0 + row_step)
            for half in range(2):
                cols = slice(half * d_ff + c * FFN_CHUNK, half * d_ff + (c + 1) * FFN_CHUNK)
                au_ref[slot, rows, half * FFN_CHUNK:(half + 1) * FFN_CHUNK] = jnp.dot(
                    h2_ref[rows, :], wup_ref[:, cols], preferred_element_type=F32)

    def elem(c, slot):
        cols = slice(c * FFN_CHUNK, (c + 1) * FFN_CHUNK)
        pos = lax.broadcasted_iota(jnp.int32, (t_rows, FFN_CHUNK), 0) & (row_len - 1)
        a = au_ref[slot, :, :FFN_CHUNK]
        cw = cw_ref[:, cols]
        a = (jnp.where(pos == 0, 0.0, pltpu.roll(a, 1, 0)) * cw[0:1] + a * cw[1:2]
             + jnp.where(pos == row_len - 1, 0.0, pltpu.roll(a, t_rows - 1, 0)) * cw[2:3] + cb_ref[:, cols])
        h_ref[slot] = (_silu(a) * au_ref[slot, :, FFN_CHUNK:]).astype(BF16)

    def down(c, slot):
        rows = slice(c * FFN_CHUNK, (c + 1) * FFN_CHUNK)
        acc_ref[...] += jnp.dot(h_ref[slot], wdn_ref[rows, :], preferred_element_type=F32)

    def stage(c):
        up(c + 1, 0)
        elem(c, 1)
        down(c - 1, 0)
        up(c + 2, 1)
        elem(c + 1, 0)
        down(c, 1)

    assert nchunk % 2 == 1 and nchunk >= 3
    up(0, 0, TOKEN_TILE)
    up(1, 1, TOKEN_TILE)
    elem(0, 0)

    for i in range((nchunk - 3) // 2):
        stage(2 * i + 1)
    c = nchunk - 2
    up(c + 1, 0)
    elem(c, 1)
    down(c - 1, 0)
    elem(c + 1, 0)
    down(c, 1)
    down(c + 1, 0)
    x2 = x1_ref[...] + mod_ref[5:6, :] * acc_ref[...]
    if final:
        ms2 = jnp.mean(x2 * x2, axis=-1, keepdims=True)
        x2 = (x2 * lax.rsqrt(ms2 + EPS)) * fing_ref[...]
    o_ref[...] = x2


def _out_call(xt, mod4, layer, mod_base, groups, o_f, o_b, gate, yfft, ycp, params, fing, *, row_len, final):
    n, d = xt.shape
    t = OUT_TILE
    tiles_per_group = n // groups // t
    assert tiles_per_group * t * groups == n
    hd = np.arange(W_GLA) // GLA_DV
    ones = jnp.asarray((hd[:, None] == hd[None, :]).astype(np.float32), BF16)
    row = lambda i: (i, 0)
    acts = [xt, o_f, o_b, gate, yfft, ycp]
    glag, wout, n2g, wup, cw, cb, wdn = params
    consts = [glag, ones, wout, n2g, wup, cw, cb, wdn, fing]
    return pl.pallas_call(
        functools.partial(_out_kernel, row_len=row_len, final=final),
        grid=(n // t,),
        in_specs=[pl.BlockSpec((t, d), row), _mod_spec(mod4, layer, mod_base, tiles_per_group)]
                 + [pl.BlockSpec((t, a.shape[1]), row) for a in acts[1:]]
                 + [_const_spec(a.shape) if a is ones or a is fing else _layer_spec(a, layer) for a in consts],
        out_specs=pl.BlockSpec((t, d), row),
        out_shape=jax.ShapeDtypeStruct((n, d), F32),
        scratch_shapes=[pltpu.VMEM((t, d), F32), pltpu.VMEM((t, d), BF16),
                        pltpu.VMEM((2, t, 2 * FFN_CHUNK), F32), pltpu.VMEM((2, t, FFN_CHUNK), BF16),
                        pltpu.VMEM((t, d), F32)],
        compiler_params=_params("arbitrary"),
        name="out_ffn",
    )(xt, mod4, *acts[1:], *consts)


def _block_diag(w):
    dep, g, a, b = w.shape
    eye = jnp.eye(g, dtype=w.dtype)
    return (eye[None, :, None, :, None] * w[:, :, :, None, :]).reshape(dep, g * a, g * b)


def _w_in_layout_kernel(wt_ref, o_ref):
    a0, a1 = COL_STARTS[2], COL_STARTS[4]
    wt = wt_ref[...]
    pad = jnp.zeros((IN_COLS - wt.shape[0], wt.shape[1]), wt.dtype)
    o_ref[...] = jnp.concatenate([wt[:a0], wt[a1:], wt[a0:a1], pad], axis=0).T.astype(BF16)


def _w_in_layout_call(w_in):
    dep, d, cols = w_in.shape
    rows = TOKEN_TILE
    return pl.pallas_call(
        _w_in_layout_kernel,
        grid=(dep, d // rows),
        in_specs=[pl.BlockSpec((None, cols, rows), lambda i, j: (i, 0, j))],
        out_specs=pl.BlockSpec((None, rows, IN_COLS), lambda i, j: (i, j, 0)),
        out_shape=jax.ShapeDtypeStruct((dep, d, IN_COLS), BF16),
        compiler_params=_params("arbitrary", "arbitrary"),
        name="w_in_layout",
    )(jnp.swapaxes(w_in, 1, 2))


def _prep_params(norm1_g, norm2_g, w_in, gla_w_a2, gla_b_a2, gla_norm_g, fft_w, conv_w, conv_b, pool_w,
                 pool_scale, w_out, ffn_w_up, ffn_conv_w, ffn_conv_b, ffn_w_down):
    dep = w_in.shape[0]
    w_in_r = _w_in_layout_call(w_in)
    wa2 = jnp.zeros((dep, LANES, 2 * GLA_QK), F32)
    wa2 = wa2.at[:, 0:GLA_RANK, 0:GLA_QK].set(gla_w_a2[:, 0])
    wa2 = wa2.at[:, GLA_RANK:2 * GLA_RANK, GLA_QK:].set(gla_w_a2[:, 1])
    in_params = (norm1_g[:, None, :], w_in_r, wa2.astype(BF16), gla_b_a2.reshape(dep, 1, 2 * GLA_QK),
                 conv_w, conv_b[:, None, :], _block_diag(pool_w).astype(BF16), pool_scale[:, None, :])
    out_params = (jnp.tile(gla_norm_g, (1, GLA_HEADS))[:, None, :], w_out.astype(BF16), norm2_g[:, None, :],
                  ffn_w_up.astype(BF16), ffn_conv_w, ffn_conv_b[:, None, :], ffn_w_down.astype(BF16))
    return in_params, out_params, _block_diag(fft_w).astype(BF16)


def kernel(x, c, ctx, c_ctx, norm1_g, norm2_g, w_mod, b_mod, w_in, gla_w_a2, gla_b_a2, gla_norm_g,
           fft_w, conv_w, conv_b, pool_w, pool_scale, w_out, ffn_w_up, ffn_conv_w, ffn_conv_b,
           ffn_w_down, final_norm_g):
    batch, seq, d = x.shape
    ctx_len = ctx.shape[1]
    depth = w_mod.shape[0]
    assert seq % GLA_BLOCK == 0 and seq % FFT_N2 == 0 and ctx_len == TOKEN_TILE

    cc = jnp.concatenate([c, c_ctx[None, :], jnp.zeros((MOD_ROWS - batch - 1, d), F32)], axis=0)
    mod4 = _mod_call(cc, w_mod, b_mod).reshape(depth, MOD_ROWS, 6, d)
    in_params, out_params, wf_bd = _prep_params(
        norm1_g, norm2_g, w_in, gla_w_a2, gla_b_a2, gla_norm_g, fft_w, conv_w, conv_b, pool_w, pool_scale,
        w_out, ffn_w_up, ffn_conv_w, ffn_conv_b, ffn_w_down)

    xt = x.reshape(batch * seq, d)
    xc = ctx.reshape(batch * ctx_len, d)
    zero_state = jnp.zeros((batch, W_GLA, GLA_QK), F32)
    fing = final_norm_g.reshape(1, d)
    for i in range(depth):
        last = i == depth - 1
        kq, v, la, gate, ufft, ycp = _in_call(xc, mod4, i, batch, 1, in_params, row_len=ctx_len)
        o_f, o_b, s_f, s_b = _gla_call(kq, v, la, zero_state, zero_state, batch=batch)
        if not last:
            yfft = _fft_direct_call(ufft, wf_bd, i, batch=batch)
            xc = _out_call(xc, mod4, i, batch, 1, o_f, o_b, gate, yfft, ycp, out_params, fing,
                           row_len=ctx_len, final=False)

        kq, v, la, gate, ufft, ycp = _in_call(xt, mod4, i, 0, batch, in_params, row_len=GRID_W)
        o_f, o_b, _, _ = _gla_call(kq, v, la, s_f, s_b, batch=batch)
        yfft = _fft_latent_call(ufft, wf_bd, i, batch=batch)
        xt = _out_call(xt, mod4, i, 0, batch, o_f, o_b, gate, yfft, ycp, out_params, fing,
                       row_len=GRID_W, final=last)
    return xt.reshape(batch, seq, d)
```

```python
import functools

import numpy as np
import jax
import jax.numpy as jnp
from jax import lax
from jax.experimental import pallas as pl
from jax.experimental.pallas import tpu as pltpu

F32 = jnp.float32
BF16 = jnp.bfloat16

GRID_W = 64
EPS = 1e-6
GLA_HEADS = 4
GLA_DK = 32
GLA_DV = 64
GLA_QK = GLA_HEADS * GLA_DK
W_GLA = GLA_HEADS * GLA_DV
GLA_RANK = 16
GLA_TAU = 16.0
FFT_GROUPS = 4
FFT_DG = 64
W_FFT = FFT_GROUPS * FFT_DG
W_CONV = 256
POOL_WINDOWS = (2, 4, 8, 16)
POOL_DG = 64
W_POOL = len(POOL_WINDOWS) * POOL_DG
COL_SIZES = (GLA_QK, W_GLA, GLA_RANK, GLA_RANK, GLA_QK, W_GLA, W_FFT, W_CONV, W_CONV, W_CONV, W_POOL)
COL_STARTS = tuple(int(s) for s in np.cumsum((0,) + COL_SIZES)[:-1])

LANES = 128
VMEM_LIMIT_BYTES = 56 * 1024 * 1024

TOKEN_TILE = 256
IN_TILE = 2048
OUT_TILE = 512
GLA_CHUNK = 64
GLA_SUB = 8
GLA_SUBBLOCK = 256
GLA_BLOCK = 512
FFN_CHUNK = 256
FFT_N2 = 128
FFT_GROUP = 8
NEG_BIG = -1e30
LOG2_E = 1.4426950408889634

IN_K, IN_V, IN_Q, IN_G, IN_FFT, IN_H, IN_BG, IN_CG, IN_POOL, IN_A = (
    0, 128, 384, 512, 768, 1024, 1280, 1536, 1792, 2048)
IN_COLS = IN_A + LANES
MOD_ROWS = 8


def _const_spec(shape):
    nd = len(shape)
    return pl.BlockSpec(shape, lambda *_: (0,) * nd, pipeline_mode=pl.Buffered(1))


def _layer_spec(arr, layer):
    nd = arr.ndim
    return pl.BlockSpec((None,) + arr.shape[1:], lambda *_: (layer,) + (0,) * (nd - 1),
                        pipeline_mode=pl.Buffered(1))


def _mod_spec(mod4, layer, base, tiles_per_group):
    return pl.BlockSpec((None, None) + mod4.shape[2:], lambda i: (layer, base + i // tiles_per_group, 0, 0))


def _params(*sem):
    return pltpu.CompilerParams(dimension_semantics=sem, vmem_limit_bytes=VMEM_LIMIT_BYTES)


def _rms_mod(x, g, scale, shift):
    ms = jnp.mean(x * x, axis=-1, keepdims=True)
    return (x * lax.rsqrt(ms + EPS)) * g * (1.0 + scale) + shift


def _silu(a):
    return a * jax.nn.sigmoid(a)


def _mod_kernel(c_ref, w_ref, b_ref, o_ref):
    s = _silu(c_ref[...]).astype(BF16)
    o_ref[0] = jnp.dot(s, w_ref[0].astype(BF16), preferred_element_type=F32) + b_ref[0]


def _mod_call(cc, w_mod, b_mod):
    depth, d, n = w_mod.shape
    tn = 1536
    return pl.pallas_call(
        _mod_kernel,
        grid=(depth, n // tn),
        in_specs=[pl.BlockSpec(cc.shape, lambda i, j: (0, 0)),
                  pl.BlockSpec((1, d, tn), lambda i, j: (i, 0, j)),
                  pl.BlockSpec((1, 1, tn), lambda i, j: (i, 0, j))],
        out_specs=pl.BlockSpec((1, cc.shape[0], tn), lambda i, j: (i, 0, j)),
        out_shape=jax.ShapeDtypeStruct((depth, cc.shape[0], n), F32),
        compiler_params=_params("arbitrary", "arbitrary"),
        name="modulation",
    )(cc, w_mod, b_mod.reshape(depth, 1, n))


def _in_kernel(x_ref, mod_ref, g_ref, w_ref, wa2_ref, ba2_ref, cw_ref, cb_ref, cnt_ref, wpool_ref, pscale_ref,
               kq_ref, v_ref, la_ref, gate_ref, fft_ref, ycp_ref, *, row_len):
    nsub = x_ref.shape[0] // TOKEN_TILE

    def project(s):
        rows = slice(s * TOKEN_TILE, (s + 1) * TOKEN_TILE)
        h = _rms_mod(x_ref[rows, :], g_ref[...], mod_ref[1:2, :], mod_ref[0:1, :]).astype(BF16)
        return jnp.dot(h, w_ref[...], preferred_element_type=F32)

    p_next = project(0)
    for s in range(nsub):
        p, p_next = p_next, (project(s + 1) if s + 1 < nsub else None)
        _in_mixers(p, slice(s * TOKEN_TILE, (s + 1) * TOKEN_TILE), wa2_ref, ba2_ref, cw_ref, cb_ref, cnt_ref,
                   wpool_ref, pscale_ref, kq_ref, v_ref, la_ref, gate_ref, fft_ref, ycp_ref, row_len)


def _in_mixers(p, rows, wa2_ref, ba2_ref, cw_ref, cb_ref, cnt_ref, wpool_ref, pscale_ref,
               kq_ref, v_ref, la_ref, gate_ref, fft_ref, ycp_ref, row_len):
    t_rows = p.shape[0]
    kq_ref[rows, :GLA_QK] = p[:, IN_K:IN_K + GLA_QK]
    kq_ref[rows, GLA_QK:] = p[:, IN_Q:IN_Q + GLA_QK] * (GLA_DK ** -0.5)
    v_ref[rows, :] = p[:, IN_V:IN_V + W_GLA]
    gate_ref[rows, :] = p[:, IN_G:IN_G + W_GLA]
    fft_ref[rows, :] = p[:, IN_FFT:IN_FFT + W_FFT]
    z = jnp.dot(p[:, IN_A:IN_A + LANES].astype(BF16), wa2_ref[...], preferred_element_type=F32) + ba2_ref[...]
    la_ref[rows, :] = (jnp.minimum(z, 0.0) - jnp.log(1.0 + jnp.exp(-jnp.abs(z)))) * (1.0 / GLA_TAU)

    def pos(a):
        return lax.broadcasted_iota(jnp.int32, a.shape, 0) & (row_len - 1)

    def prev(a, s):
        return jnp.where(pos(a) >= s, pltpu.roll(a, s, 0), 0.0)

    def nxt(a, s):
        return jnp.where(pos(a) < row_len - s, pltpu.roll(a, t_rows - s, 0), 0.0)

    t = p[:, IN_CG:IN_CG + W_CONV] * p[:, IN_H:IN_H + W_CONV]
    cw = cw_ref[...]
    conv = prev(t, 1) * cw[0:1] + t * cw[1:2] + nxt(t, 1) * cw[2:3] + cb_ref[...]
    ycp_ref[rows, :W_CONV] = (p[:, IN_BG:IN_BG + W_CONV] * conv).astype(BF16)

    u = p[:, IN_POOL:IN_POOL + W_POOL]
    halves = []
    for side, steps in ((0, 1), (1, 3)):
        f = u[:, side * LANES:(side + 1) * LANES]
        g = prev(f, 1)
        sums = [g + f]
        for i in range(steps):
            f = f + nxt(f, 1 << i)
            g = g + prev(g, 1 << i)
            sums.append(g + f)
        lane = lax.broadcasted_iota(jnp.int32, f.shape, 1)
        halves.append(jnp.where(lane < POOL_DG, sums[-2], sums[-1]))
    tot = jnp.concatenate(halves, axis=1)
    pooled = tot / cnt_ref[...] - u
    yp = jnp.dot(pooled.astype(BF16), wpool_ref[...], preferred_element_type=F32) * pscale_ref[...]
    ycp_ref[rows, W_CONV:] = yp.astype(BF16)


def _pool_counts(t_rows, row_len):
    pos = np.arange(t_rows) % row_len
    cols = []
    for w in POOL_WINDOWS:
        lo = np.clip(pos - w // 2, 0, row_len - 1)
        hi = np.clip(pos + w // 2 - 1, 0, row_len - 1)
        cols.append(np.repeat((hi - lo + 1).astype(np.float32)[:, None], POOL_DG, axis=1))
    return np.concatenate(cols, axis=1)


def _in_call(xt, mod4, layer, mod_base, groups, params, *, row_len):
    n, d = xt.shape
    t = min(IN_TILE, n // groups)
    tiles_per_group = n // groups // t
    assert tiles_per_group * t * groups == n
    cnt = jnp.asarray(_pool_counts(TOKEN_TILE, row_len))
    row = lambda i: (i, 0)
    outs = [jax.ShapeDtypeStruct((n, 2 * GLA_QK), F32), jax.ShapeDtypeStruct((n, W_GLA), F32),
            jax.ShapeDtypeStruct((n, 2 * GLA_QK), F32), jax.ShapeDtypeStruct((n, W_GLA), F32),
            jax.ShapeDtypeStruct((n, W_FFT), F32), jax.ShapeDtypeStruct((n, W_CONV + W_POOL), BF16)]
    g, w, wa2, ba2, cw, cb, wpool, pscale = params
    return pl.pallas_call(
        functools.partial(_in_kernel, row_len=row_len),
        grid=(n // t,),
        in_specs=[pl.BlockSpec((t, d), row), _mod_spec(mod4, layer, mod_base, tiles_per_group),
                  _layer_spec(g, layer), _layer_spec(w, layer), _layer_spec(wa2, layer), _layer_spec(ba2, layer),
                  _layer_spec(cw, layer), _layer_spec(cb, layer), _const_spec(cnt.shape),
                  _layer_spec(wpool, layer), _layer_spec(pscale, layer)],
        out_specs=[pl.BlockSpec((t, o.shape[1]), row) for o in outs],
        out_shape=outs,
        compiler_params=_params("arbitrary"),
        name="in_proj",
    )(xt, mod4, g, w, wa2, ba2, cw, cb, cnt, wpool, pscale)


def _gla_block(kq_ref, v_ref, la_ref, s_ref, cum_ref, o_ref, consts, fwd):
    tri2, rexp, bdmask, hq, hv = consts
    k, q, v, la = kq_ref[:, :GLA_QK], kq_ref[:, GLA_QK:], v_ref[...], la_ref[...]
    t_rows = q.shape[0]
    c, sub = GLA_CHUNK, GLA_SUB
    nchunk = t_rows // c
    nt = (((1,), (1,)), ((), ()))
    tn = (((0,), (0,)), ((), ()))
    la_hi = la.astype(BF16)
    la_lo = (la - la_hi.astype(F32)).astype(BF16)
    cums = jnp.dot(tri2, jnp.concatenate([la_hi, la_lo], axis=1), preferred_element_type=F32)
    cum = (cums[:, :GLA_QK] + cums[:, GLA_QK:]) * LOG2_E
    cum_ref[...] = cum
    end = jnp.concatenate([jnp.broadcast_to(cum_ref[pl.ds(ci * c + (c - 1 if fwd else 0), 1), :], (c, GLA_QK))
                           for ci in range(nchunk)], axis=0)
    yield

    nblk = t_rows // sub

    def row_of_block(ref, width, j):
        cols = [jnp.concatenate([jnp.broadcast_to(ref[pl.ds(sub * blk + j, 1), c0:c0 + LANES], (sub, LANES))
                                 for blk in range(nblk)], axis=0) for c0 in range(0, width, LANES)]
        return cols[0] if len(cols) == 1 else jnp.concatenate(cols, axis=1)

    ii = lax.broadcasted_iota(jnp.int32, (t_rows, GLA_QK), 0) & (sub - 1)
    es = []
    for j in range(sub):
        keep = (ii >= j) if fwd else (ii <= j)
        w = jnp.exp2(jnp.where(keep, cum - row_of_block(cum_ref, GLA_QK, j), NEG_BIG))
        es.append((w * (q * row_of_block(kq_ref, GLA_QK, j))).astype(BF16))
    pr = jnp.dot(jnp.concatenate(es, axis=0), rexp, preferred_element_type=F32)
    yield

    qe = (q * jnp.exp2(cum)).astype(BF16)
    kd = (k * jnp.exp2(end - cum)).astype(BF16)
    v16 = v.astype(BF16)
    s = s_ref[0]
    o_parts = [None] * nchunk
    for ci in (range(nchunk) if fwd else reversed(range(nchunk))):
        rows = slice(ci * c, (ci + 1) * c)
        o_parts[ci] = lax.dot_general(qe[rows], s.astype(BF16), nt, preferred_element_type=F32)
        upd = lax.dot_general(v16[rows], kd[rows], tn, preferred_element_type=F32)
        s = s * jnp.exp2(end[ci * c:ci * c + 1]) + upd * bdmask
    s_ref[0] = s
    o = jnp.concatenate(o_parts, axis=0)

    half = t_rows // 2
    b = c // 2
    while b >= sub:
        npair = t_rows // (2 * b)
        refs = []
        for p in range(npair):
            r = 2 * b * p + (b - 1 if fwd else b)
            refs.append(jnp.broadcast_to(cum_ref[pl.ds(r, 1), :], (2 * b, GLA_QK)))
        e = jnp.exp2(-jnp.abs(cum - jnp.concatenate(refs, axis=0)))
        qfull, kfull = q * e, k * e
        first = [slice(2 * b * p, 2 * b * p + b) for p in range(npair)]
        second = [slice(2 * b * p + b, 2 * b * p + 2 * b) for p in range(npair)]
        qrows, krows = (second, first) if fwd else (first, second)
        qsel = jnp.concatenate([qfull[r] for r in qrows], axis=0)
        ksel = jnp.concatenate([kfull[r] for r in krows], axis=0).astype(BF16)
        vsel = jnp.concatenate([v[r] for r in krows], axis=0).astype(BF16)
        qst = jnp.concatenate([qsel * hq[h] for h in range(GLA_HEADS)], axis=0).astype(BF16)
        att = lax.dot_general(qst, ksel, nt, preferred_element_type=F32)
        ri = lax.broadcasted_iota(jnp.int32, att.shape, 0)
        ci = lax.broadcasted_iota(jnp.int32, att.shape, 1)
        att = jnp.where(((ri & (half - 1)) ^ ci) < b, att, 0.0).astype(BF16)
        att_k = jnp.concatenate([att[h * half:(h + 1) * half] for h in range(GLA_HEADS)], axis=1)
        v_k = jnp.concatenate([vsel * hv[h].astype(BF16) for h in range(GLA_HEADS)], axis=0)
        res = jnp.dot(att_k, v_k, preferred_element_type=F32)
        zero = jnp.zeros((b, W_GLA), F32)
        pieces = []
        for p in range(npair):
            piece = res[p * b:(p + 1) * b]
            pieces += [zero, piece] if fwd else [piece, zero]
        o = o + jnp.concatenate(pieces, axis=0)
        b //= 2
    yield

    for j in range(sub):
        o = o + pr[t_rows * j:t_rows * (j + 1)] * row_of_block(v_ref, W_GLA, j)
    o_ref[...] = o


def _gla_kernel(kqf_ref, vf_ref, laf_ref, kqb_ref, vb_ref, lab_ref, h0f_ref, h0b_ref,
                trif_ref, trib_ref, rexp_ref, bd_ref,
                of_ref, ob_ref, sf_ref, sb_ref, cum_ref):
    @pl.when(pl.program_id(1) == 0)
    def _():
        sf_ref[...] = h0f_ref[...]
        sb_ref[...] = h0b_ref[...]

    lq = lax.broadcasted_iota(jnp.int32, (1, GLA_QK), 1)
    lv = lax.broadcasted_iota(jnp.int32, (1, W_GLA), 1)
    hq = [jnp.where((lq >= h * GLA_DK) & (lq < (h + 1) * GLA_DK), 1.0, 0.0) for h in range(GLA_HEADS)]
    hv = [jnp.where((lv >= h * GLA_DV) & (lv < (h + 1) * GLA_DV), 1.0, 0.0) for h in range(GLA_HEADS)]
    rexp, bdmask = rexp_ref[...], bd_ref[...]
    nsub = kqf_ref.shape[0] // GLA_SUBBLOCK
    scans = []
    for r in range(nsub):
        lo, hi = pl.ds(r * GLA_SUBBLOCK, GLA_SUBBLOCK), pl.ds((nsub - 1 - r) * GLA_SUBBLOCK, GLA_SUBBLOCK)
        scans.append(_gla_block(kqf_ref.at[lo], vf_ref.at[lo], laf_ref.at[lo], sf_ref, cum_ref.at[2 * r],
                                of_ref.at[lo], (trif_ref[...], rexp, bdmask, hq, hv), True))
        scans.append(_gla_block(kqb_ref.at[hi], vb_ref.at[hi], lab_ref.at[hi], sb_ref, cum_ref.at[2 * r + 1],
                                ob_ref.at[hi], (trib_ref[...], rexp, bdmask, hq, hv), False))
    live = True
    while live:
        live = False
        for scan in scans:
            live = next(scan, "done") != "done" or live


def _gla_consts(t):
    c = GLA_CHUNK
    i = np.arange(t)
    same = (i[None, :] // c) == (i[:, None] // c)
    trif = (same & (i[None, :] <= i[:, None])).astype(np.float32)
    trib = (same & (i[None, :] >= i[:, None])).astype(np.float32)
    hk = np.arange(GLA_QK) // GLA_DK
    hd = np.arange(W_GLA) // GLA_DV
    rexp = (hk[:, None] == hd[None, :]).astype(np.float32)
    bd = (hd[:, None] == hk[None, :]).astype(np.float32)
    return (jnp.asarray(trif, BF16), jnp.asarray(trib, BF16), jnp.asarray(rexp, BF16), jnp.asarray(bd, F32))


def _gla_call(kq, v, la, h0f, h0b, *, batch):
    n = kq.shape[0]
    tg = min(GLA_BLOCK, n // batch)
    nb = n // batch // tg
    trif, trib, rexp, bd = _gla_consts(GLA_SUBBLOCK)
    fw = lambda b, i: (b * nb + i, 0)
    bw = lambda b, i: (b * nb + nb - 1 - i, 0)
    bw1 = lambda b, i: (b * nb + nb - 1 - i, 1)
    st = lambda b, i: (b, 0, 0)
    sshape = jax.ShapeDtypeStruct((batch, W_GLA, GLA_QK), F32)
    return pl.pallas_call(
        _gla_kernel,
        grid=(batch, nb),
        in_specs=[pl.BlockSpec((tg, 2 * GLA_QK), fw), pl.BlockSpec((tg, W_GLA), fw), pl.BlockSpec((tg, GLA_QK), fw),
                  pl.BlockSpec((tg, 2 * GLA_QK), bw), pl.BlockSpec((tg, W_GLA), bw), pl.BlockSpec((tg, GLA_QK), bw1),
                  pl.BlockSpec((1, W_GLA, GLA_QK), st), pl.BlockSpec((1, W_GLA, GLA_QK), st),
                  _const_spec(trif.shape), _const_spec(trib.shape), _const_spec(rexp.shape), _const_spec(bd.shape)],
        out_specs=[pl.BlockSpec((tg, W_GLA), fw), pl.BlockSpec((tg, W_GLA), bw),
                   pl.BlockSpec((1, W_GLA, GLA_QK), st), pl.BlockSpec((1, W_GLA, GLA_QK), st)],
        out_shape=[jax.ShapeDtypeStruct((n, W_GLA), F32), jax.ShapeDtypeStruct((n, W_GLA), F32), sshape, sshape],
        scratch_shapes=[pltpu.VMEM((2 * tg // GLA_SUBBLOCK, GLA_SUBBLOCK, GLA_QK), F32)],
        compiler_params=_params("arbitrary", "arbitrary"),
        name="gla",
    )(kq, v, la, kq, v, la, h0f, h0b, trif, trib, rexp, bd)


def _fft_tail(ab, cs_ref, wf_ref, norm):
    f = jnp.dot(ab, cs_ref[...].astype(BF16), preferred_element_type=F32) * norm
    return jnp.dot(f.astype(BF16), wf_ref[...], preferred_element_type=F32)


def _fft_kernel(kron_ref, tw_ref, dft_ref, cs_ref, wf_ref, u_ref, o_ref, y_ref, rows_ref, *, norm):
    n1, n2, width = u_ref.shape
    halves = rows_ref.shape[0]
    g = FFT_GROUP
    kron = kron_ref[...].astype(BF16)

    def stage1(blk, carry):
        cols = pl.ds(pl.multiple_of(blk * g, g), g)
        ub = u_ref[:, cols, :].reshape(n1 * g, width).astype(BF16)
        y_ref[:, cols, :] = jnp.dot(kron, ub, preferred_element_type=F32).reshape(2 * n1, g, width)
        return carry

    lax.fori_loop(0, n2 // g, stage1, 0)
    c2, s2 = dft_ref[0], dft_ref[1]

    def stage2(grp, carry):
        parts = []
        for j in range(g):
            k1 = grp * g + j
            tc, ts = tw_ref[k1, 0:1, :], tw_ref[k1, 1:2, :]
            mc = c2 * tc - s2 * ts
            ms = s2 * tc + c2 * ts
            m = jnp.concatenate([jnp.concatenate([mc, ms], axis=1),
                                 jnp.concatenate([-ms, mc], axis=1)], axis=0).astype(BF16)
            ys = jnp.concatenate([y_ref[k1], y_ref[n1 + k1]], axis=0).astype(BF16)
            z = jnp.dot(m, ys, preferred_element_type=F32)
            parts.append(jnp.concatenate([z[:n2], z[n2:]], axis=1).astype(BF16))
        out = _fft_tail(jnp.concatenate(parts, axis=0), cs_ref, wf_ref, norm)
        for j in range(g):
            for h in range(halves):
                rows_ref[h, pl.ds(j, n2, stride=g), :] = out[j * n2:(j + 1) * n2, h * LANES:(h + 1) * LANES]
        cols = pl.ds(pl.multiple_of(grp * g, g), g)
        for h in range(halves):
            o_ref[:, cols, h * LANES:(h + 1) * LANES] = rows_ref[h].reshape(n2, g, LANES)
        return carry

    lax.fori_loop(0, n1 // g, stage2, 0)


def _fft_direct_kernel(m_ref, u_ref, cs_ref, wf_ref, o_ref, *, norm):
    n = u_ref.shape[0]
    z = jnp.dot(m_ref[...].astype(BF16), u_ref[...].astype(BF16), preferred_element_type=F32)
    o_ref[...] = _fft_tail(jnp.concatenate([z[:n], z[n:]], axis=1).astype(BF16), cs_ref, wf_ref, norm)


def _dft_cos_sin(n):
    k = np.arange(n, dtype=np.int64)
    ang = 2.0 * np.pi * ((k[:, None] * k[None, :]) % n).astype(np.float64) / n
    return np.cos(ang), np.sin(ang)


def _channel_dft():
    c, s = _dft_cos_sin(FFT_DG)
    eye = np.eye(FFT_GROUPS)
    return jnp.asarray(np.concatenate([np.kron(eye, c), np.kron(eye, s)], axis=0), F32)


def _fft_latent_call(u, wf_bd, layer, *, batch):
    n = u.shape[0] // batch
    n1, n2 = n // FFT_N2, FFT_N2
    halves = W_FFT // LANES
    g = FFT_GROUP
    assert n1 % g == 0 and n2 % g == 0
    c1, s1 = _dft_cos_sin(n1)
    kron =jnp.asarray(np.kron(np.concatenate([c1, -s1], axis=0), np.eye(g)), F32)
    tw_ang = 2.0 * np.pi * (np.arange(n1, dtype=np.int64)[:, None] * np.arange(n2, dtype=np.int64)[None, :]) / n
    tw = jnp.asarray(np.stack([np.cos(tw_ang), np.sin(tw_ang)], axis=1), F32)
    dft2 = jnp.asarray(np.stack(_dft_cos_sin(n2), axis=0), F32)
    cs = _channel_dft()
    norm = float(1.0 / np.sqrt(n * FFT_DG))
    out = pl.pallas_call(
        functools.partial(_fft_kernel, norm=norm),
        grid=(batch,),
        in_specs=[_const_spec(kron.shape), _const_spec(tw.shape), _const_spec(dft2.shape), _const_spec(cs.shape),
                  _layer_spec(wf_bd, layer),
                  pl.BlockSpec((None, n1, n2, W_FFT), lambda b: (b, 0, 0, 0), pipeline_mode=pl.Buffered(1))],
        out_specs=pl.BlockSpec((None, n2, n1, W_FFT), lambda b: (b, 0, 0, 0)),
        out_shape=jax.ShapeDtypeStruct((batch, n2, n1, W_FFT), F32),
        scratch_shapes=[pltpu.VMEM((2 * n1, n2, W_FFT), F32), pltpu.VMEM((halves, n2 * g, LANES), F32)],
        compiler_params=_params("arbitrary"),
        name="fft",
    )(kron, tw, dft2, cs, wf_bd, u.reshape(batch, n1, n2, W_FFT))
    return out.reshape(batch * n, W_FFT)


def _fft_direct_call(u, wf_bd, layer, *, batch):
    n = u.shape[0] // batch
    c, s = _dft_cos_sin(n)
    m = jnp.asarray(np.concatenate([c, -s], axis=0), F32)
    cs = _channel_dft()
    norm = float(1.0 / np.sqrt(n * FFT_DG))
    return pl.pallas_call(
        functools.partial(_fft_direct_kernel, norm=norm),
        grid=(batch,),
        in_specs=[_const_spec(m.shape), pl.BlockSpec((n, W_FFT), lambda b: (b, 0)),
                  _const_spec(cs.shape), _layer_spec(wf_bd, layer)],
        out_specs=pl.BlockSpec((n, W_FFT), lambda b: (b, 0)),
        out_shape=jax.ShapeDtypeStruct((batch * n, W_FFT), F32),
        compiler_params=_params("arbitrary"),
        name="fft_direct",
    )(m, u, cs, wf_bd)


def _out_kernel(x_ref, mod_ref, of_ref, ob_ref, gate_ref, fft_ref, ycp_ref, glag_ref, ones_ref, wout_ref,
                n2g_ref, wup_ref, cw_ref, cb_ref, wdn_ref, fing_ref, o_ref,
                x1_ref, h2_ref, au_ref, h_ref, acc_ref, *, row_len, final):
    _out_head(x_ref, mod_ref, of_ref, ob_ref, gate_ref, fft_ref, ycp_ref, glag_ref, ones_ref, wout_ref,
              n2g_ref, x1_ref, h2_ref)
    _out_ffn(mod_ref, wup_ref, cw_ref, cb_ref, wdn_ref, fing_ref, o_ref, x1_ref, h2_ref, au_ref, h_ref,
             acc_ref, row_len, final)


def _out_head(x_ref, mod_ref, of_ref, ob_ref, gate_ref, fft_ref, ycp_ref, glag_ref, ones_ref, wout_ref, n2g_ref,
              x1_ref, h2_ref):
    for r0 in range(0, x_ref.shape[0], TOKEN_TILE):
        rows = slice(r0, r0 + TOKEN_TILE)
        o = of_ref[rows, :] + ob_ref[rows, :]
        osq = o * o
        hi = osq.astype(BF16)
        lo = (osq - hi.astype(F32)).astype(BF16)
        ms = (jnp.dot(hi, ones_ref[...], preferred_element_type=F32)
              + jnp.dot(lo, ones_ref[...], preferred_element_type=F32)) * (1.0 / GLA_DV)
        gl = (o * lax.rsqrt(ms + EPS)) * glag_ref[...] * _silu(gate_ref[rows, :])
        ymix = jnp.concatenate([gl.astype(BF16), fft_ref[rows, :].astype(BF16), ycp_ref[rows, :]], axis=1)
        x1 = x_ref[rows, :] + mod_ref[2:3, :] * jnp.dot(ymix, wout_ref[...], preferred_element_type=F32)
        x1_ref[rows, :] = x1
        h2_ref[rows, :] = _rms_mod(x1, n2g_ref[...], mod_ref[4:5, :], mod_ref[3:4, :]).astype(BF16)


def _out_ffn(mod_ref, wup_ref, cw_ref, cb_ref, wdn_ref, fing_ref, o_ref, x1_ref, h2_ref, au_ref, h_ref, acc_ref,
             row_len, final):
    t_rows = o_ref.shape[0]
    d_ff = wdn_ref.shape[0]
    nchunk = d_ff // FFN_CHUNK
    acc_ref[...] = jnp.zeros(acc_ref.shape, F32)

    def up(c, slot, row_step=None):
        row_step = row_step or t_rows
        for r0 in range(0, t_rows, row_step):
            rows = slice(r0, r0 + row_step)
            for half in range(2):
                cols = slice(half * d_ff + c * FFN_CHUNK, half * d_ff + (c + 1) * FFN_CHUNK)
                au_ref[slot, rows, half * FFN_CHUNK:(half + 1) * FFN_CHUNK] = jnp.dot(
                    h2_ref[rows, :], wup_ref[:, cols], preferred_element_type=F32)

    def elem(c, slot):
        cols = slice(c * FFN_CHUNK, (c + 1) * FFN_CHUNK)
        pos = lax.broadcasted_iota(jnp.int32, (t_rows, FFN_CHUNK), 0) & (row_len - 1)
        a = au_ref[slot, :, :FFN_CHUNK]
        cw = cw_ref[:, cols]
        a = (jnp.where(pos == 0, 0.0, pltpu.roll(a, 1, 0)) * cw[0:1] + a * cw[1:2]
             + jnp.where(pos == row_len - 1, 0.0, pltpu.roll(a, t_rows - 1, 0)) * cw[2:3] + cb_ref[:, cols])
        h_ref[slot] = (_silu(a) * au_ref[slot, :, FFN_CHUNK:]).astype(BF16)

    def down(c, slot):
        rows = slice(c * FFN_CHUNK, (c + 1) * FFN_CHUNK)
        acc_ref[...] += jnp.dot(h_ref[slot], wdn_ref[rows, :], preferred_element_type=F32)

    def stage(c):
        up(c + 1, 0)
        elem(c, 1)
        down(c - 1, 0)
        up(c + 2, 1)
        elem(c + 1, 0)
        down(c, 1)

    assert nchunk % 2 == 1 and nchunk >= 3
    up(0, 0, TOKEN_TILE)
    up(1, 1, TOKEN_TILE)
    elem(0, 0)

    for i in range((nchunk - 3) // 2):
        stage(2 * i + 1)
    c = nchunk - 2
    up(c + 1, 0)
    elem(c, 1)
    down(c - 1, 0)
    elem(c + 1, 0)
    down(c, 1)
    down(c + 1, 0)
    x2 = x1_ref[...] + mod_ref[5:6, :] * acc_ref[...]
    if final:
        ms2 = jnp.mean(x2 * x2, axis=-1, keepdims=True)
        x2 = (x2 * lax.rsqrt(ms2 + EPS)) * fing_ref[...]
    o_ref[...] = x2


def _out_call(xt, mod4, layer, mod_base, groups, o_f, o_b, gate, yfft, ycp, params, fing, *, row_len, final):
    n, d = xt.shape
    t = OUT_TILE
    tiles_per_group = n // groups // t
    assert tiles_per_group * t * groups == n
    hd = np.arange(W_GLA) // GLA_DV
    ones = jnp.asarray((hd[:, None] == hd[None, :]).astype(np.float32), BF16)
    row = lambda i: (i, 0)
    acts = [xt, o_f, o_b, gate, yfft, ycp]
    glag, wout, n2g, wup, cw, cb, wdn = params
    consts = [glag, ones, wout, n2g, wup, cw, cb, wdn, fing]
    return pl.pallas_call(
        functools.partial(_out_kernel, row_len=row_len, final=final),
        grid=(n // t,),
        in_specs=[pl.BlockSpec((t, d), row), _mod_spec(mod4, layer, mod_base, tiles_per_group)]
                 + [pl.BlockSpec((t, a.shape[1]), row) for a in acts[1:]]
                 + [_const_spec(a.shape) if a is ones or a is fing else _layer_spec(a, layer) for a in consts],
        out_specs=pl.BlockSpec((t, d), row),
        out_shape=jax.ShapeDtypeStruct((n, d), F32),
        scratch_shapes=[pltpu.VMEM((t, d), F32), pltpu.VMEM((t, d), BF16),
                        pltpu.VMEM((2, t, 2 * FFN_CHUNK), F32), pltpu.VMEM((2, t, FFN_CHUNK), BF16),
                        pltpu.VMEM((t, d), F32)],
        compiler_params=_params("arbitrary"),
        name="out_ffn",
    )(xt, mod4, *acts[1:], *consts)


def _block_diag(w):
    dep, g, a, b = w.shape
    eye = jnp.eye(g, dtype=w.dtype)
    return (eye[None, :, None, :, None] * w[:, :, :, None, :]).reshape(dep, g * a, g * b)


def _w_in_layout_kernel(wt_ref, o_ref):
    a0, a1 = COL_STARTS[2], COL_STARTS[4]
    wt = wt_ref[...]
    pad = jnp.zeros((IN_COLS - wt.shape[0], wt.shape[1]), wt.dtype)
    o_ref[...] = jnp.concatenate([wt[:a0], wt[a1:], wt[a0:a1], pad], axis=0).T.astype(BF16)


def _w_in_layout_call(w_in):
    dep, d, cols = w_in.shape
    rows = TOKEN_TILE
    return pl.pallas_call(
        _w_in_layout_kernel,
        grid=(dep, d // rows),
        in_specs=[pl.BlockSpec((None, cols, rows), lambda i, j: (i, 0, j))],
        out_specs=pl.BlockSpec((None, rows, IN_COLS), lambda i, j: (i, j, 0)),
        out_shape=jax.ShapeDtypeStruct((dep, d, IN_COLS), BF16),
        compiler_params=_params("arbitrary", "arbitrary"),
        name="w_in_layout",
    )(jnp.swapaxes(w_in, 1, 2))


def _prep_params(norm1_g, norm2_g, w_in, gla_w_a2, gla_b_a2, gla_norm_g, fft_w, conv_w, conv_b, pool_w,
                 pool_scale, w_out, ffn_w_up, ffn_conv_w, ffn_conv_b, ffn_w_down):
    dep = w_in.shape[0]
    w_in_r = _w_in_layout_call(w_in)
    wa2 = jnp.zeros((dep, LANES, 2 * GLA_QK), F32)
    wa2 = wa2.at[:, 0:GLA_RANK, 0:GLA_QK].set(gla_w_a2[:, 0])
    wa2 = wa2.at[:, GLA_RANK:2 * GLA_RANK, GLA_QK:].set(gla_w_a2[:, 1])
    in_params = (norm1_g[:, None, :], w_in_r, wa2.astype(BF16), gla_b_a2.reshape(dep, 1, 2 * GLA_QK),
                 conv_w, conv_b[:, None, :], _block_diag(pool_w).astype(BF16), pool_scale[:, None, :])
    out_params = (jnp.tile(gla_norm_g, (1, GLA_HEADS))[:, None, :], w_out.astype(BF16), norm2_g[:, None, :],
                  ffn_w_up.astype(BF16), ffn_conv_w, ffn_conv_b[:, None, :], ffn_w_down.astype(BF16))
    return in_params, out_params, _block_diag(fft_w).astype(BF16)


def kernel(x, c, ctx, c_ctx, norm1_g, norm2_g, w_mod, b_mod, w_in, gla_w_a2, gla_b_a2, gla_norm_g,
           fft_w, conv_w, conv_b, pool_w, pool_scale, w_out, ffn_w_up, ffn_conv_w, ffn_conv_b,
           ffn_w_down, final_norm_g):
    batch, seq, d = x.shape
    ctx_len = ctx.shape[1]
    depth = w_mod.shape[0]
    assert seq % GLA_BLOCK == 0 and seq % FFT_N2 == 0 and ctx_len == TOKEN_TILE

    cc = jnp.concatenate([c, c_ctx[None, :], jnp.zeros((MOD_ROWS - batch - 1, d), F32)], axis=0)
    mod4 = _mod_call(cc, w_mod, b_mod).reshape(depth, MOD_ROWS, 6, d)
    in_params, out_params, wf_bd = _prep_params(
        norm1_g, norm2_g, w_in, gla_w_a2, gla_b_a2, gla_norm_g, fft_w, conv_w, conv_b, pool_w, pool_scale,
        w_out, ffn_w_up, ffn_conv_w, ffn_conv_b, ffn_w_down)

    xt = x.reshape(batch * seq, d)
    xc = ctx.reshape(batch * ctx_len, d)
    zero_state = jnp.zeros((batch, W_GLA, GLA_QK), F32)
    fing = final_norm_g.reshape(1, d)
    for i in range(depth):
        last = i == depth - 1
        kq, v, la, gate, ufft, ycp = _in_call(xc, mod4, i, batch, 1, in_params, row_len=ctx_len)
        o_f, o_b, s_f, s_b = _gla_call(kq, v, la, zero_state, zero_state, batch=batch)
        if not last:
            yfft = _fft_direct_call(ufft, wf_bd, i, batch=batch)
            xc = _out_call(xc, mod4, i, batch, 1, o_f, o_b, gate, yfft, ycp, out_params, fing,
                           row_len=ctx_len, final=False)

        kq, v, la, gate, ufft, ycp = _in_call(xt, mod4, i, 0, batch, in_params, row_len=GRID_W)
        o_f, o_b, _, _ = _gla_call(kq, v, la, s_f, s_b, batch=batch)
        yfft = _fft_latent_call(ufft, wf_bd, i, batch=batch)
        xt = _out_call(xt, mod4, i, 0, batch, o_f, o_b, gate, yfft, ycp, out_params, fing,
                       row_len=GRID_W, final=last)
    return xt.reshape(batch, seq, d)
```

```python
import functools

import numpy as np
import jax
import jax.numpy as jnp
from jax import lax
from jax.experimental import pallas as pl
from jax.experimental.pallas import tpu as pltpu

F32 = jnp.float32
BF16 = jnp.bfloat16

GRID_W = 64
EPS = 1e-6
GLA_HEADS = 4
GLA_DK = 32
GLA_DV = 64
GLA_QK = GLA_HEADS * GLA_DK
W_GLA = GLA_HEADS * GLA_DV
GLA_RANK = 16
GLA_TAU = 16.0
FFT_GROUPS = 4
FFT_DG = 64
W_FFT = FFT_GROUPS * FFT_DG
W_CONV = 256
POOL_WINDOWS = (2, 4, 8, 16)
POOL_DG = 64
W_POOL = len(POOL_WINDOWS) * POOL_DG
COL_SIZES = (GLA_QK, W_GLA, GLA_RANK, GLA_RANK, GLA_QK, W_GLA, W_FFT, W_CONV, W_CONV, W_CONV, W_POOL)
COL_STARTS = tuple(int(s) for s in np.cumsum((0,) + COL_SIZES)[:-1])

LANES = 128
VMEM_LIMIT_BYTES = 56 * 1024 * 1024

TOKEN_TILE = 256
IN_TILE = 1024
OUT_TILE = 512
GLA_CHUNK = 64
GLA_SUB = 8
GLA_SUBBLOCK = 256
GLA_BLOCK = 512
FFN_CHUNK = 256
WEIGHT_STAGE_ROWS = 256
WEIGHT_STAGE_ROWS_WIDE = 128
FFT_N2 = 128
FFT_GROUP = 8
NEG_BIG = -1e30
LOG2_E = 1.4426950408889634

IN_K, IN_V, IN_Q, IN_G, IN_FFT, IN_H, IN_BG, IN_CG, IN_POOL, IN_A = (
    0, 128, 384, 512, 768, 1024, 1280, 1536, 1792, 2048)
IN_COLS = IN_A + LANES
MOD_ROWS = 8


def _const_spec(shape):
    nd = len(shape)
    return pl.BlockSpec(shape, lambda *_: (0,) * nd, pipeline_mode=pl.Buffered(1))


def _layer_spec(arr, layer):
    nd = arr.ndim
    return pl.BlockSpec((None,) + arr.shape[1:], lambda *_: (layer,) + (0,) * (nd - 1),
                        pipeline_mode=pl.Buffered(1))


def _mod_spec(mod4, layer, base, tiles_per_group):
    return pl.BlockSpec((None, None) + mod4.shape[2:], lambda i: (layer, base + i // tiles_per_group, 0, 0))


def _params(*sem):
    return pltpu.CompilerParams(dimension_semantics=sem, vmem_limit_bytes=VMEM_LIMIT_BYTES)


def _rms_mod(x, g, scale, shift):
    ms = jnp.mean(x * x, axis=-1, keepdims=True)
    return (x * lax.rsqrt(ms + EPS)) * g * (1.0 + scale) + shift


def _silu(a):
    return a * jax.nn.sigmoid(a)


def _mod_kernel(c_ref, w_ref, b_ref, o_ref):
    s = _silu(c_ref[...]).astype(BF16)
    o_ref[0] = jnp.dot(s, w_ref[0].astype(BF16), preferred_element_type=F32) + b_ref[0]


def _mod_call(cc, w_mod, b_mod):
    depth, d, n = w_mod.shape
    tn = 1536
    return pl.pallas_call(
        _mod_kernel,
        grid=(depth, n // tn),
        in_specs=[pl.BlockSpec(cc.shape, lambda i, j: (0, 0)),
                  pl.BlockSpec((1, d, tn), lambda i, j: (i, 0, j)),
                  pl.BlockSpec((1, 1, tn), lambda i, j: (i, 0, j))],
        out_specs=pl.BlockSpec((1, cc.shape[0], tn), lambda i, j: (i, 0, j)),
        out_shape=jax.ShapeDtypeStruct((depth, cc.shape[0], n), F32),
        compiler_params=_params("arbitrary", "arbitrary"),
        name="modulation",
    )(cc, w_mod, b_mod.reshape(depth, 1, n))


def _in_kernel(x_ref, mod_ref, g_ref, w_ref, wa2_ref, ba2_ref, cw_ref, cb_ref, cnt_ref, wpool_ref, pscale_ref,
               kq_ref, v_ref, la_ref, gate_ref, fft_ref, ycp_ref, *, row_len):
    nsub = x_ref.shape[0] // TOKEN_TILE

    def project(s):
        rows = slice(s * TOKEN_TILE, (s + 1) * TOKEN_TILE)
        h = _rms_mod(x_ref[rows, :], g_ref[...], mod_ref[1:2, :], mod_ref[0:1, :]).astype(BF16)
        return jnp.dot(h, w_ref[...], preferred_element_type=F32)

    p_next = project(0)
    for s in range(nsub):
        p, p_next = p_next, (project(s + 1) if s + 1 < nsub else None)
        _in_mixers(p, slice(s * TOKEN_TILE, (s + 1) * TOKEN_TILE), wa2_ref, ba2_ref, cw_ref, cb_ref, cnt_ref,
                   wpool_ref, pscale_ref, kq_ref, v_ref, la_ref, gate_ref, fft_ref, ycp_ref, row_len)


def _in_mixers(p, rows, wa2_ref, ba2_ref, cw_ref, cb_ref, cnt_ref, wpool_ref, pscale_ref,
               kq_ref, v_ref, la_ref, gate_ref, fft_ref, ycp_ref, row_len):
    t_rows = p.shape[0]
    kq_ref[rows, :GLA_QK] = p[:, IN_K:IN_K + GLA_QK]
    kq_ref[rows, GLA_QK:] = p[:, IN_Q:IN_Q + GLA_QK] * (GLA_DK ** -0.5)
    v_ref[rows, :] = p[:, IN_V:IN_V + W_GLA]
    gate_ref[rows, :] = p[:, IN_G:IN_G + W_GLA]
    fft_ref[rows, :] = p[:, IN_FFT:IN_FFT + W_FFT]
    z = jnp.dot(p[:, IN_A:IN_A + LANES].astype(BF16), wa2_ref[...], preferred_element_type=F32) + ba2_ref[...]
    la_ref[rows, :] = (jnp.minimum(z, 0.0) - jnp.log(1.0 + jnp.exp(-jnp.abs(z)))) * (1.0 / GLA_TAU)

    def pos(a):
        return lax.broadcasted_iota(jnp.int32, a.shape, 0) & (row_len - 1)

    def prev(a, s):
        return jnp.where(pos(a) >= s, pltpu.roll(a, s, 0), 0.0)

    def nxt(a, s):
        return jnp.where(pos(a) < row_len - s, pltpu.roll(a, t_rows - s, 0), 0.0)

    t = p[:, IN_CG:IN_CG + W_CONV] * p[:, IN_H:IN_H + W_CONV]
    cw = cw_ref[...]
    conv = prev(t, 1) * cw[0:1] + t * cw[1:2] + nxt(t, 1) * cw[2:3] + cb_ref[...]
    ycp_ref[rows, :W_CONV] = (p[:, IN_BG:IN_BG + W_CONV] * conv).astype(BF16)

    u = p[:, IN_POOL:IN_POOL + W_POOL]
    halves = []
    for side, steps in ((0, 1), (1, 3)):
        f = u[:, side * LANES:(side + 1) * LANES]
        g = prev(f, 1)
        sums = [g + f]
        for i in range(steps):
            f = f + nxt(f, 1 << i)
            g = g + prev(g, 1 << i)
            sums.append(g + f)
        lane = lax.broadcasted_iota(jnp.int32, f.shape, 1)
        halves.append(jnp.where(lane < POOL_DG, sums[-2], sums[-1]))
    tot = jnp.concatenate(halves, axis=1)
    pooled = tot / cnt_ref[...] - u
    yp = jnp.dot(pooled.astype(BF16), wpool_ref[...], preferred_element_type=F32) * pscale_ref[...]
    ycp_ref[rows, W_CONV:] = yp.astype(BF16)


def _pool_counts(t_rows, row_len):
    pos = np.arange(t_rows) % row_len
    cols = []
    for w in POOL_WINDOWS:
        lo = np.clip(pos - w // 2, 0, row_len - 1)
        hi = np.clip(pos + w // 2 - 1, 0, row_len - 1)
        cols.append(np.repeat((hi - lo + 1).astype(np.float32)[:, None], POOL_DG, axis=1))
    return np.concatenate(cols, axis=1)


def _in_call(xt, mod4, layer, mod_base, groups, params, *, row_len):
    n, d = xt.shape
    t = min(IN_TILE, n // groups)
    tiles_per_group = n // groups // t
    assert tiles_per_group * t * groups == n
    cnt = jnp.asarray(_pool_counts(TOKEN_TILE, row_len))
    row = lambda i: (i, 0)
    outs = [jax.ShapeDtypeStruct((n, 2 * GLA_QK), F32), jax.ShapeDtypeStruct((n, W_GLA), F32),
            jax.ShapeDtypeStruct((n, 2 * GLA_QK), F32), jax.ShapeDtypeStruct((n, W_GLA), F32),
            jax.ShapeDtypeStruct((n, W_FFT), F32), jax.ShapeDtypeStruct((n, W_CONV + W_POOL), BF16)]
    g, w, wa2, ba2, cw, cb, wpool, pscale = params
    return pl.pallas_call(
        functools.partial(_in_kernel, row_len=row_len),
        grid=(n // t,),
        in_specs=[pl.BlockSpec((t, d), row), _mod_spec(mod4, layer, mod_base, tiles_per_group),
                  _layer_spec(g, layer), _layer_spec(w, layer), _layer_spec(wa2, layer), _layer_spec(ba2, layer),
                  _layer_spec(cw, layer), _layer_spec(cb, layer), _const_spec(cnt.shape),
                  _layer_spec(wpool, layer), _layer_spec(pscale, layer)],
        out_specs=[pl.BlockSpec((t, o.shape[1]), row) for o in outs],
        out_shape=outs,
        compiler_params=_params("arbitrary"),
        name="in_proj",
    )(xt, mod4, g, w, wa2, ba2, cw, cb, cnt, wpool, pscale)


def _gla_block(kq_ref, v_ref, la_ref, s_ref, cum_ref, o_ref, consts, fwd):
    tri2, rexp, bdmask, hq, hv = consts
    k, q, v, la = kq_ref[:, :GLA_QK], kq_ref[:, GLA_QK:], v_ref[...], la_ref[...]
    t_rows = q.shape[0]
    c, sub = GLA_CHUNK, GLA_SUB
    nchunk = t_rows // c
    nt = (((1,), (1,)), ((), ()))
    tn = (((0,), (0,)), ((), ()))
    la_hi = la.astype(BF16)
    la_lo = (la - la_hi.astype(F32)).astype(BF16)
    cums = jnp.dot(tri2, jnp.concatenate([la_hi, la_lo], axis=1), preferred_element_type=F32)
    cum = (cums[:, :GLA_QK] + cums[:, GLA_QK:]) * LOG2_E
    cum_ref[...] = cum
    end = jnp.concatenate([jnp.broadcast_to(cum_ref[pl.ds(ci * c + (c - 1 if fwd else 0), 1), :], (c, GLA_QK))
                           for ci in range(nchunk)], axis=0)
    yield

    nblk = t_rows // sub

    def row_of_block(ref, width, j):
        cols = [jnp.concatenate([jnp.broadcast_to(ref[pl.ds(sub * blk + j, 1), c0:c0 + LANES], (sub, LANES))
                                 for blk in range(nblk)], axis=0) for c0 in range(0, width, LANES)]
        return cols[0] if len(cols) == 1 else jnp.concatenate(cols, axis=1)

    ii = lax.broadcasted_iota(jnp.int32, (t_rows, GLA_QK), 0) & (sub - 1)
    es = []
    for j in range(sub):
        keep = (ii >= j) if fwd else (ii <= j)
        w = jnp.exp2(jnp.where(keep, cum - row_of_block(cum_ref, GLA_QK, j), NEG_BIG))
        es.append((w * (q * row_of_block(kq_ref, GLA_QK, j))).astype(BF16))
    pr = jnp.dot(jnp.concatenate(es, axis=0), rexp, preferred_element_type=F32)
    yield

    qe = (q * jnp.exp2(cum)).astype(BF16)
    kd = (k * jnp.exp2(end - cum)).astype(BF16)
    v16 = v.astype(BF16)
    s = s_ref[0]
    o_parts = [None] * nchunk
    for ci in (range(nchunk) if fwd else reversed(range(nchunk))):
        rows = slice(ci * c, (ci + 1) * c)
        o_parts[ci] = lax.dot_general(qe[rows], s.astype(BF16), nt, preferred_element_type=F32)
        upd = lax.dot_general(v16[rows], kd[rows], tn, preferred_element_type=F32)
        s = s * jnp.exp2(end[ci * c:ci * c + 1]) + upd * bdmask
    s_ref[0] = s
    o = jnp.concatenate(o_parts, axis=0)

    half = t_rows // 2
    b = c // 2
    while b >= sub:
        npair = t_rows // (2 * b)
        refs = []
        for p in range(npair):
            r = 2 * b * p + (b - 1 if fwd else b)
            refs.append(jnp.broadcast_to(cum_ref[pl.ds(r, 1), :], (2 * b, GLA_QK)))
        e = jnp.exp2(-jnp.abs(cum - jnp.concatenate(refs, axis=0)))
        qfull, kfull = q * e, k * e
        first = [slice(2 * b * p, 2 * b * p + b) for p in range(npair)]
        second = [slice(2 * b * p + b, 2 * b * p + 2 * b) for p in range(npair)]
        qrows, krows = (second, first) if fwd else (first, second)
        qsel = jnp.concatenate([qfull[r] for r in qrows], axis=0)
        ksel = jnp.concatenate([kfull[r] for r in krows], axis=0).astype(BF16)
        vsel = jnp.concatenate([v[r] for r in krows], axis=0).astype(BF16)
        qst = jnp.concatenate([qsel * hq[h] for h in range(GLA_HEADS)], axis=0).astype(BF16)
        att = lax.dot_general(qst, ksel, nt, preferred_element_type=F32)
        ri = lax.broadcasted_iota(jnp.int32, att.shape, 0)
        ci = lax.broadcasted_iota(jnp.int32, att.shape, 1)
        att = jnp.where(((ri & (half - 1)) ^ ci) < b, att, 0.0).astype(BF16)
        att_k = jnp.concatenate([att[h * half:(h + 1) * half] for h in range(GLA_HEADS)], axis=1)
        v_k = jnp.concatenate([vsel * hv[h].astype(BF16) for h in range(GLA_HEADS)], axis=0)
        res = jnp.dot(att_k, v_k, preferred_element_type=F32)
        zero = jnp.zeros((b, W_GLA), F32)
        pieces = []
        for p in range(npair):
            piece = res[p * b:(p + 1) * b]
            pieces += [zero, piece] if fwd else [piece, zero]
        o = o + jnp.concatenate(pieces, axis=0)
        b //= 2
    yield

    for j in range(sub):
        o = o + pr[t_rows * j:t_rows * (j + 1)] * row_of_block(v_ref, W_GLA, j)
    o_ref[...] = o


def _gla_kernel(kqf_ref, vf_ref, laf_ref, kqb_ref, vb_ref, lab_ref, h0f_ref, h0b_ref,
                trif_ref, trib_ref, rexp_ref, bd_ref,
                of_ref, ob_ref, sf_ref, sb_ref, cum_ref):
    @pl.when(pl.program_id(1) == 0)
    def _():
        sf_ref[...] = h0f_ref[...]
        sb_ref[...] = h0b_ref[...]

    lq = lax.broadcasted_iota(jnp.int32, (1, GLA_QK), 1)
    lv = lax.broadcasted_iota(jnp.int32, (1, W_GLA), 1)
    hq = [jnp.where((lq >= h * GLA_DK) & (lq < (h + 1) * GLA_DK), 1.0, 0.0) for h in range(GLA_HEADS)]
    hv = [jnp.where((lv >= h * GLA_DV) & (lv < (h + 1) * GLA_DV), 1.0, 0.0) for h in range(GLA_HEADS)]
    rexp, bdmask = rexp_ref[...], bd_ref[...]
    nsub = kqf_ref.shape[0] // GLA_SUBBLOCK
    scans = []
    for r in range(nsub):
        lo, hi = pl.ds(r * GLA_SUBBLOCK, GLA_SUBBLOCK), pl.ds((nsub - 1 - r) * GLA_SUBBLOCK, GLA_SUBBLOCK)
        scans.append(_gla_block(kqf_ref.at[lo], vf_ref.at[lo], laf_ref.at[lo], sf_ref, cum_ref.at[2 * r],
                                of_ref.at[lo], (trif_ref[...], rexp, bdmask, hq, hv), True))
        scans.append(_gla_block(kqb_ref.at[hi], vb_ref.at[hi], lab_ref.at[hi], sb_ref, cum_ref.at[2 * r + 1],
                                ob_ref.at[hi], (trib_ref[...], rexp, bdmask, hq, hv), False))
    live = True
    while live:
        live = False
        for scan in scans:
            live = next(scan, "done") != "done" or live


def _gla_consts(t):
    c = GLA_CHUNK
    i = np.arange(t)
    same = (i[None, :] // c) == (i[:, None] // c)
    trif = (same & (i[None, :] <= i[:, None])).astype(np.float32)
    trib = (same & (i[None, :] >= i[:, None])).astype(np.float32)
    hk = np.arange(GLA_QK) // GLA_DK
    hd = np.arange(W_GLA) // GLA_DV
    rexp = (hk[:, None] == hd[None, :]).astype(np.float32)
    bd = (hd[:, None] == hk[None, :]).astype(np.float32)
    return (jnp.asarray(trif, BF16), jnp.asarray(trib, BF16), jnp.asarray(rexp, BF16), jnp.asarray(bd, F32))


def _gla_call(kq, v, la, h0f, h0b, *, batch):
    n = kq.shape[0]
    tg = min(GLA_BLOCK, n // batch)
    nb = n // batch // tg
    trif, trib, rexp, bd = _gla_consts(GLA_SUBBLOCK)
    fw = lambda b, i: (b * nb + i, 0)
    bw = lambda b, i: (b * nb + nb - 1 - i, 0)
    bw1 = lambda b, i: (b * nb + nb - 1 - i, 1)
    st = lambda b, i: (b, 0, 0)
    sshape = jax.ShapeDtypeStruct((batch, W_GLA, GLA_QK), F32)
    return pl.pallas_call(
        _gla_kernel,
        grid=(batch, nb),
        in_specs=[pl.BlockSpec((tg, 2 * GLA_QK), fw), pl.BlockSpec((tg, W_GLA), fw), pl.BlockSpec((tg, GLA_QK), fw),
                  pl.BlockSpec((tg, 2 * GLA_QK), bw), pl.BlockSpec((tg, W_GLA), bw), pl.BlockSpec((tg, GLA_QK), bw1),
                  pl.BlockSpec((1, W_GLA, GLA_QK), st), pl.BlockSpec((1, W_GLA, GLA_QK), st),
                  _const_spec(trif.shape), _const_spec(trib.shape), _const_spec(rexp.shape), _const_spec(bd.shape)],
        out_specs=[pl.BlockSpec((tg, W_GLA), fw), pl.BlockSpec((tg, W_GLA), bw),
                   pl.BlockSpec((1, W_GLA, GLA_QK), st), pl.BlockSpec((1, W_GLA, GLA_QK), st)],
        out_shape=[jax.ShapeDtypeStruct((n, W_GLA), F32), jax.ShapeDtypeStruct((n, W_GLA), F32), sshape, sshape],
        scratch_shapes=[pltpu.VMEM((2 * tg // GLA_SUBBLOCK, GLA_SUBBLOCK, GLA_QK), F32)],
        compiler_params=_params("arbitrary", "arbitrary"),
        name="gla",
    )(kq, v, la, kq, v, la, h0f, h0b, trif, trib, rexp, bd)


def _fft_tail(ab, cs_ref, wf_ref, norm):
    f = jnp.dot(ab, cs_ref[...].astype(BF16), preferred_element_type=F32) * norm
    return jnp.dot(f.astype(BF16), wf_ref[...], preferred_element_type=F32)


def _fft_kernel(kron_ref, tw_ref, dft_ref, cs_ref, wf_ref, u_ref, o_ref, y_ref, rows_ref, *, norm):
    n1, n2, width = u_ref.shape
    halves = rows_ref.shape[0]
    g = FFT_GROUP
    kron = kron_ref[...].astype(BF16)

    def stage1(blk, carry):
        cols = pl.ds(pl.multiple_of(blk * g, g), g)
        ub = u_ref[:, cols, :].reshape(n1 * g, width).astype(BF16)
        y_ref[:, cols, :] = jnp.dot(kron, ub, preferred_element_type=F32).reshape(2 * n1, g, width)
        return carry

    lax.fori_loop(0, n2 // g, stage1, 0)
    c2, s2 = dft_ref[0], dft_ref[1]

    def stage2(grp, carry):
        parts = []
        for j in range(g):
            k1 = grp * g + j
            tc, ts = tw_ref[k1, 0:1, :], tw_ref[k1, 1:2, :]
            mc = c2 * tc - s2 * ts
            ms = s2 * tc + c2 * ts
            m = jnp.concatenate([jnp.concatenate([mc, ms], axis=1),
                                 jnp.concatenate([-ms, mc], axis=1)], axis=0).astype(BF16)
            ys = jnp.concatenate([y_ref[k1], y_ref[n1 + k1]], axis=0).astype(BF16)
            z = jnp.dot(m, ys, preferred_element_type=F32)
            parts.append(jnp.concatenate([z[:n2], z[n2:]], axis=1).astype(BF16))
        out = _fft_tail(jnp.concatenate(parts, axis=0), cs_ref, wf_ref, norm)
        for j in range(g):
            for h in range(halves):
                rows_ref[h, pl.ds(j, n2, stride=g), :] = out[j * n2:(j + 1) * n2, h * LANES:(h + 1) * LANES]
        cols = pl.ds(pl.multiple_of(grp * g, g), g)
        for h in range(halves):
            o_ref[:, cols, h * LANES:(h + 1) * LANES] = rows_ref[h].reshape(n2, g, LANES)
        return carry

    lax.fori_loop(0, n1 // g, stage2, 0)


def _fft_direct_kernel(m_ref, u_ref, cs_ref, wf_ref, o_ref, *, norm):
    n = u_ref.shape[0]
    z = jnp.dot(m_ref[...].astype(BF16), u_ref[...].astype(BF16), preferred_element_type=F32)
    o_ref[...] = _fft_tail(jnp.concatenate([z[:n], z[n:]], axis=1).astype(BF16), cs_ref, wf_ref, norm)


def _dft_cos_sin(n):
    k = np.arange(n, dtype=np.int64)
    ang = 2.0 * np.pi * ((k[:, None] * k[None, :]) % n).astype(np.float64) / n
    return np.cos(ang), np.sin(ang)


def _channel_dft():
    c, s = _dft_cos_sin(FFT_DG)
    eye = np.eye(FFT_GROUPS)
    return jnp.asarray(np.concatenate([np.kron(eye, c), np.kron(eye, s)], axis=0), F32)


def _fft_latent_call(u, wf_bd, layer, *, batch):
    n = u.shape[0] // batch
    n1, n2 = n // FFT_N2, FFT_N2
    halves = W_FFT // LANES
    g = FFT_GROUP
    assert n1 % g == 0 and n2 % g == 0
    c1, s1 = _dft_cos_sin(n1)
    kron =jnp.asarray(np.kron(np.concatenate([c1, -s1], axis=0), np.eye(g)), F32)
    tw_ang = 2.0 * np.pi * (np.arange(n1, dtype=np.int64)[:, None] * np.arange(n2, dtype=np.int64)[None, :]) / n
    tw = jnp.asarray(np.stack([np.cos(tw_ang), np.sin(tw_ang)], axis=1), F32)
    dft2 = jnp.asarray(np.stack(_dft_cos_sin(n2), axis=0), F32)
    cs = _channel_dft()
    norm = float(1.0 / np.sqrt(n * FFT_DG))
    out = pl.pallas_call(
        functools.partial(_fft_kernel, norm=norm),
        grid=(batch,),
        in_specs=[_const_spec(kron.shape), _const_spec(tw.shape), _const_spec(dft2.shape), _const_spec(cs.shape),
                  _layer_spec(wf_bd, layer),
                  pl.BlockSpec((None, n1, n2, W_FFT), lambda b: (b, 0, 0, 0), pipeline_mode=pl.Buffered(1))],
        out_specs=pl.BlockSpec((None, n2, n1, W_FFT), lambda b: (b, 0, 0, 0)),
        out_shape=jax.ShapeDtypeStruct((batch, n2, n1, W_FFT), F32),
        scratch_shapes=[pltpu.VMEM((2 * n1, n2, W_FFT), F32), pltpu.VMEM((halves, n2 * g, LANES), F32)],
        compiler_params=_params("arbitrary"),
        name="fft",
    )(kron, tw, dft2, cs, wf_bd, u.reshape(batch, n1, n2, W_FFT))
    return out.reshape(batch * n, W_FFT)


def _fft_direct_call(u, wf_bd, layer, *, batch):
    n = u.shape[0] // batch
    c, s = _dft_cos_sin(n)
    m = jnp.asarray(np.concatenate([c, -s], axis=0), F32)
    cs = _channel_dft()
    norm = float(1.0 / np.sqrt(n * FFT_DG))
    return pl.pallas_call(
        functools.partial(_fft_direct_kernel, norm=norm),
        grid=(batch,),
        in_specs=[_const_spec(m.shape), pl.BlockSpec((n, W_FFT), lambda b: (b, 0)),
                  _const_spec(cs.shape), _layer_spec(wf_bd, layer)],
        out_specs=pl.BlockSpec((n, W_FFT), lambda b: (b, 0)),
        out_shape=jax.ShapeDtypeStruct((batch * n, W_FFT), F32),
        compiler_params=_params("arbitrary"),
        name="fft_direct",
    )(m, u, cs, wf_bd)


def _load_cast(w_hbm, layer, dst_ref, stage_ref, sem_ref):
    step = stage_ref.shape[1]
    nchunks = dst_ref.shape[0] // step

    def copy(k):
        return pltpu.make_async_copy(w_hbm.at[layer, pl.ds(k * step, step), :], stage_ref.at[k % 2],
                                     sem_ref.at[k % 2])

    copy(0).start()
    for k in range(nchunks):
        if k + 1 < nchunks:
            copy(k + 1).start()
        copy(k).wait()
        dst_ref[pl.ds(k * step, step), :] = stage_ref[k % 2].astype(BF16)


def _out_kernel(x_ref, mod_ref, of_ref, ob_ref, gate_ref, fft_ref, ycp_ref, glag_ref, ones_ref, wout_hbm,
                n2g_ref, wup_hbm, cw_ref, cb_ref, wdn_hbm, fing_ref, o_ref,
                x1_ref, h2_ref, au_ref, h_ref, acc_ref, wout_ref, wup_ref, wdn_ref, stage_up_ref, stage_sq_ref,
                sem_up_ref, sem_sq_ref, *, row_len, final, layer):
    @pl.when(pl.program_id(0) == 0)
    def _():
        _load_cast(wout_hbm, layer, wout_ref, stage_sq_ref, sem_sq_ref)
        _load_cast(wup_hbm, layer, wup_ref, stage_up_ref, sem_up_ref)
        _load_cast(wdn_hbm, layer, wdn_ref, stage_sq_ref, sem_sq_ref)

    _out_head(x_ref, mod_ref, of_ref, ob_ref, gate_ref, fft_ref, ycp_ref, glag_ref, ones_ref, wout_ref,
              n2g_ref, x1_ref, h2_ref)
    _out_ffn(mod_ref, wup_ref, cw_ref, cb_ref, wdn_ref, fing_ref, o_ref, x1_ref, h2_ref, au_ref, h_ref,
             acc_ref, row_len, final)


def _out_head(x_ref, mod_ref, of_ref, ob_ref, gate_ref, fft_ref, ycp_ref, glag_ref, ones_ref, wout_ref, n2g_ref,
              x1_ref, h2_ref):
    for r0 in range(0, x_ref.shape[0], TOKEN_TILE):
        rows = slice(r0, r0 + TOKEN_TILE)
        o = of_ref[rows, :] + ob_ref[rows, :]
        osq = o * o
        hi = osq.astype(BF16)
        lo = (osq - hi.astype(F32)).astype(BF16)
        ms = (jnp.dot(hi, ones_ref[...], preferred_element_type=F32)
              + jnp.dot(lo, ones_ref[...], preferred_element_type=F32)) * (1.0 / GLA_DV)
        gl = (o * lax.rsqrt(ms + EPS)) * glag_ref[...] * _silu(gate_ref[rows, :])
        ymix = jnp.concatenate([gl.astype(BF16), fft_ref[rows, :].astype(BF16), ycp_ref[rows, :]], axis=1)
        x1 = x_ref[rows, :] + mod_ref[2:3, :] * jnp.dot(ymix, wout_ref[...], preferred_element_type=F32)
        x1_ref[rows, :] = x1
        h2_ref[rows, :] = _rms_mod(x1, n2g_ref[...], mod_ref[4:5, :], mod_ref[3:4, :]).astype(BF16)


def _out_ffn(mod_ref, wup_ref, cw_ref, cb_ref, wdn_ref, fing_ref, o_ref, x1_ref, h2_ref, au_ref, h_ref, acc_ref,
             row_len, final):
    t_rows = o_ref.shape[0]
    d_ff = wdn_ref.shape[0]
    nchunk = d_ff // FFN_CHUNK
    acc_ref[...] = jnp.zeros(acc_ref.shape, F32)

    def up(c, slot, row_step=None):
        row_step = row_step or t_rows
        for r0 in range(0, t_rows, row_step):
            rows = slice(r0, r0 + row_step)
            for half in range(2):
                cols = slice(half * d_ff + c * FFN_CHUNK, half * d_ff + (c + 1) * FFN_CHUNK)
                au_ref[slot, rows, half * FFN_CHUNK:(half + 1) * FFN_CHUNK] = jnp.dot(
                    h2_ref[rows, :], wup_ref[:, cols], preferred_element_type=F32)

    def elem(c, slot):
        cols = slice(c * FFN_CHUNK, (c + 1) * FFN_CHUNK)
        pos = lax.broadcasted_iota(jnp.int32, (t_rows, FFN_CHUNK), 0) & (row_len - 1)
        a = au_ref[slot, :, :FFN_CHUNK]
        cw = cw_ref[:, cols]
        a = (jnp.where(pos == 0, 0.0, pltpu.roll(a, 1, 0)) * cw[0:1] + a * cw[1:2]
             + jnp.where(pos == row_len - 1, 0.0, pltpu.roll(a, t_rows - 1, 0)) * cw[2:3] + cb_ref[:, cols])
        h_ref[slot] = (_silu(a) * au_ref[slot, :, FFN_CHUNK:]).astype(BF16)

    def down(c, slot):
        rows = slice(c * FFN_CHUNK, (c + 1) * FFN_CHUNK)
        acc_ref[...] += jnp.dot(h_ref[slot], wdn_ref[rows, :], preferred_element_type=F32)

    def stage(c):
        up(c + 1, 0)
        elem(c, 1)
        down(c - 1, 0)
        up(c + 2, 1)
        elem(c + 1, 0)
        down(c, 1)

    assert nchunk % 2 == 1 and nchunk >= 3
    up(0, 0, TOKEN_TILE)
    up(1, 1, TOKEN_TILE)
    elem(0, 0)

    for i in range((nchunk - 3) // 2):
        stage(2 * i + 1)
    c = nchunk - 2
    up(c + 1, 0)
    elem(c, 1)
    down(c - 1, 0)
    elem(c + 1, 0)
    down(c, 1)
    down(c + 1, 0)
    x2 = x1_ref[...] + mod_ref[5:6, :] * acc_ref[...]
    if final:
        ms2 = jnp.mean(x2 * x2, axis=-1, keepdims=True)
        x2 = (x2 * lax.rsqrt(ms2 + EPS)) * fing_ref[...]
    o_ref[...] = x2


def _out_call(xt, mod4, layer, mod_base, groups, o_f, o_b, gate, yfft, ycp, params, fing, *, row_len, final):
    n, d = xt.shape
    t = OUT_TILE
    tiles_per_group = n // groups // t
    assert tiles_per_group * t * groups == n
    hd = np.arange(W_GLA) // GLA_DV
    ones = jnp.asarray((hd[:, None] == hd[None, :]).astype(np.float32), BF16)
    row = lambda i: (i, 0)
    acts = [xt, o_f, o_b, gate, yfft, ycp]
    glag, wout, n2g, wup, cw, cb, wdn = params
    consts = [glag, ones, wout, n2g, wup, cw, cb, wdn, fing]

    def const_spec(a):
        if a is wout or a is wup or a is wdn:
            return pl.BlockSpec(memory_space=pl.ANY)
        return _const_spec(a.shape) if a is ones or a is fing else _layer_spec(a, layer)

    assert wout.shape[1] % WEIGHT_STAGE_ROWS == 0 and wdn.shape[1] % WEIGHT_STAGE_ROWS == 0
    assert wup.shape[1] % WEIGHT_STAGE_ROWS_WIDE == 0
    return pl.pallas_call(
        functools.partial(_out_kernel, row_len=row_len, final=final, layer=layer),
        grid=(n // t,),
        in_specs=[pl.BlockSpec((t, d), row), _mod_spec(mod4, layer, mod_base, tiles_per_group)]
                 + [pl.BlockSpec((t, a.shape[1]), row) for a in acts[1:]]
                 + [const_spec(a) for a in consts],
        out_specs=pl.BlockSpec((t, d), row),
        out_shape=jax.ShapeDtypeStruct((n, d), F32),
        scratch_shapes=[pltpu.VMEM((t, d), F32), pltpu.VMEM((t, d), BF16),
                        pltpu.VMEM((2, t, 2 * FFN_CHUNK), F32), pltpu.VMEM((2, t, FFN_CHUNK), BF16),
                        pltpu.VMEM((t, d), F32),
                        pltpu.VMEM(wout.shape[1:], BF16), pltpu.VMEM(wup.shape[1:], BF16),
                        pltpu.VMEM(wdn.shape[1:], BF16),
                        pltpu.VMEM((2, WEIGHT_STAGE_ROWS_WIDE, wup.shape[2]), F32),
                        pltpu.VMEM((2, WEIGHT_STAGE_ROWS, d), F32),
                        pltpu.SemaphoreType.DMA((2,)), pltpu.SemaphoreType.DMA((2,))],
        compiler_params=_params("arbitrary"),
        name="out_ffn",
    )(xt, mod4, *acts[1:], *consts)


def _block_diag(w):
    dep, g, a, b = w.shape
    eye = jnp.eye(g, dtype=w.dtype)
    return (eye[None, :, None, :, None] * w[:, :, :, None, :]).reshape(dep, g * a, g * b)


def _w_in_layout_kernel(wt_ref, o_ref):
    a0, a1 = COL_STARTS[2], COL_STARTS[4]
    wt = wt_ref[...]
    pad = jnp.zeros((IN_COLS - wt.shape[0], wt.shape[1]), wt.dtype)
    o_ref[...] = jnp.concatenate([wt[:a0], wt[a1:], wt[a0:a1], pad], axis=0).T.astype(BF16)


def _w_in_layout_call(w_in):
    dep, d, cols = w_in.shape
    rows = TOKEN_TILE
    return pl.pallas_call(
        _w_in_layout_kernel,
        grid=(dep, d // rows),
        in_specs=[pl.BlockSpec((None, cols, rows), lambda i, j: (i, 0, j))],
        out_specs=pl.BlockSpec((None, rows, IN_COLS), lambda i, j: (i, j, 0)),
        out_shape=jax.ShapeDtypeStruct((dep, d, IN_COLS), BF16),
        compiler_params=_params("arbitrary", "arbitrary"),
        name="w_in_layout",
    )(jnp.swapaxes(w_in, 1, 2))


def _prep_params(norm1_g, norm2_g, w_in, gla_w_a2, gla_b_a2, gla_norm_g, fft_w, conv_w, conv_b, pool_w,
                 pool_scale, w_out, ffn_w_up, ffn_conv_w, ffn_conv_b, ffn_w_down):
    dep = w_in.shape[0]
    w_in_r = _w_in_layout_call(w_in)
    wa2 = jnp.zeros((dep, LANES, 2 * GLA_QK), F32)
    wa2 = wa2.at[:, 0:GLA_RANK, 0:GLA_QK].set(gla_w_a2[:, 0])
    wa2 = wa2.at[:, GLA_RANK:2 * GLA_RANK, GLA_QK:].set(gla_w_a2[:, 1])
    in_params = (norm1_g[:, None, :], w_in_r, wa2.astype(BF16), gla_b_a2.reshape(dep, 1, 2 * GLA_QK),
                 conv_w, conv_b[:, None, :], _block_diag(pool_w).astype(BF16), pool_scale[:, None, :])
    out_params = (jnp.tile(gla_norm_g, (1, GLA_HEADS))[:, None, :], w_out, norm2_g[:, None, :],
                  ffn_w_up, ffn_conv_w, ffn_conv_b[:, None, :], ffn_w_down)
    return in_params, out_params, _block_diag(fft_w).astype(BF16)


def kernel(x, c, ctx, c_ctx, norm1_g, norm2_g, w_mod, b_mod, w_in, gla_w_a2, gla_b_a2, gla_norm_g,
           fft_w, conv_w, conv_b, pool_w, pool_scale, w_out, ffn_w_up, ffn_conv_w, ffn_conv_b,
           ffn_w_down, final_norm_g):
    batch, seq, d = x.shape
    ctx_len = ctx.shape[1]
    depth = w_mod.shape[0]
    assert seq % GLA_BLOCK == 0 and seq % FFT_N2 == 0 and ctx_len == TOKEN_TILE

    cc = jnp.concatenate([c, c_ctx[None, :], jnp.zeros((MOD_ROWS - batch - 1, d), F32)], axis=0)
    mod4 = _mod_call(cc, w_mod, b_mod).reshape(depth, MOD_ROWS, 6, d)
    in_params, out_params, wf_bd = _prep_params(
        norm1_g, norm2_g, w_in, gla_w_a2, gla_b_a2, gla_norm_g, fft_w, conv_w, conv_b, pool_w, pool_scale,
        w_out, ffn_w_up, ffn_conv_w, ffn_conv_b, ffn_w_down)

    xt = x.reshape(batch * seq, d)
    xc = ctx.reshape(batch * ctx_len, d)
    zero_state = jnp.zeros((batch, W_GLA, GLA_QK), F32)
    fing = final_norm_g.reshape(1, d)
    for i in range(depth):
        last = i == depth - 1
        kq, v, la, gate, ufft, ycp = _in_call(xc, mod4, i, batch, 1, in_params, row_len=ctx_len)
        o_f, o_b, s_f, s_b = _gla_call(kq, v, la, zero_state, zero_state, batch=batch)
        if not last:
            yfft = _fft_direct_call(ufft, wf_bd, i, batch=batch)
            xc = _out_call(xc, mod4, i, batch, 1, o_f, o_b, gate, yfft, ycp, out_params, fing,
                           row_len=ctx_len, final=False)

        kq, v, la, gate, ufft, ycp = _in_call(xt, mod4, i, 0, batch, in_params, row_len=GRID_W)
        o_f, o_b, _, _ = _gla_call(kq, v, la, s_f, s_b, batch=batch)
        yfft = _fft_latent_call(ufft, wf_bd, i, batch=batch)
        xt = _out_call(xt, mod4, i, 0, batch, o_f, o_b, gate, yfft, ycp, out_params, fing,
                       row_len=GRID_W, final=last)
    return xt.reshape(batch, seq, d)
```

```python
import functools

import numpy as np
import jax
import jax.numpy as jnp
from jax import lax
from jax.experimental import pallas as pl
from jax.experimental.pallas import tpu as pltpu

F32 = jnp.float32
BF16 = jnp.bfloat16

GRID_W = 64
EPS = 1e-6
GLA_HEADS = 4
GLA_DK = 32
GLA_DV = 64
GLA_QK = GLA_HEADS * GLA_DK
W_GLA = GLA_HEADS * GLA_DV
GLA_RANK = 16
GLA_TAU = 16.0
FFT_GROUPS = 4
FFT_DG = 64
W_FFT = FFT_GROUPS * FFT_DG
W_CONV = 256
POOL_WINDOWS = (2, 4, 8, 16)
POOL_DG = 64
W_POOL = len(POOL_WINDOWS) * POOL_DG
COL_SIZES = (GLA_QK, W_GLA, GLA_RANK, GLA_RANK, GLA_QK, W_GLA, W_FFT, W_CONV, W_CONV, W_CONV, W_POOL)
COL_STARTS = tuple(int(s) for s in np.cumsum((0,) + COL_SIZES)[:-1])

LANES = 128
VMEM_LIMIT_BYTES = 56 * 1024 * 1024

TOKEN_TILE = 256
IN_TILE = 1024
OUT_TILE = 512
GLA_CHUNK = 64
GLA_SUB = 8
GLA_SUBBLOCK = 256
GLA_BLOCK = 512
FFN_CHUNK = 256
WEIGHT_STAGE_ROWS = 128
WEIGHT_STAGE_ROWS_WIDE = 64
WEIGHT_STAGE_SLOTS = 4
FFT_N2 = 128
FFT_GROUP = 8
NEG_BIG = -1e30
LOG2_E = 1.4426950408889634

IN_K, IN_V, IN_Q, IN_G, IN_FFT, IN_H, IN_BG, IN_CG, IN_POOL, IN_A = (
    0, 128, 384, 512, 768, 1024, 1280, 1536, 1792, 2048)
IN_COLS = IN_A + LANES
MOD_ROWS = 8


def _const_spec(shape):
    nd = len(shape)
    return pl.BlockSpec(shape, lambda *_: (0,) * nd, pipeline_mode=pl.Buffered(1))


def _layer_spec(arr, layer):
    nd = arr.ndim
    return pl.BlockSpec((None,) + arr.shape[1:], lambda *_: (layer,) + (0,) * (nd - 1),
                        pipeline_mode=pl.Buffered(1))


def _mod_spec(mod4, layer, base, tiles_per_group):
    return pl.BlockSpec((None, None) + mod4.shape[2:], lambda i: (layer, base + i // tiles_per_group, 0, 0))


def _params(*sem):
    return pltpu.CompilerParams(dimension_semantics=sem, vmem_limit_bytes=VMEM_LIMIT_BYTES)


def _rms_mod(x, g, scale, shift):
    ms = jnp.mean(x * x, axis=-1, keepdims=True)
    return (x * lax.rsqrt(ms + EPS)) * g * (1.0 + scale) + shift


def _silu(a):
    return a * jax.nn.sigmoid(a)


def _mod_kernel(c_ref, w_ref, b_ref, o_ref):
    s = _silu(c_ref[...]).astype(BF16)
    o_ref[0] = jnp.dot(s, w_ref[0].astype(BF16), preferred_element_type=F32) + b_ref[0]


def _mod_call(cc, w_mod, b_mod):
    depth, d, n = w_mod.shape
    tn = 1536
    return pl.pallas_call(
        _mod_kernel,
        grid=(depth, n // tn),
        in_specs=[pl.BlockSpec(cc.shape, lambda i, j: (0, 0)),
                  pl.BlockSpec((1, d, tn), lambda i, j: (i, 0, j)),
                  pl.BlockSpec((1, 1, tn), lambda i, j: (i, 0, j))],
        out_specs=pl.BlockSpec((1, cc.shape[0], tn), lambda i, j: (i, 0, j)),
        out_shape=jax.ShapeDtypeStruct((depth, cc.shape[0], n), F32),
        compiler_params=_params("arbitrary", "arbitrary"),
        name="modulation",
    )(cc, w_mod, b_mod.reshape(depth, 1, n))


def _in_kernel(x_ref, mod_ref, g_ref, w_ref, wa2_ref, ba2_ref, cw_ref, cb_ref, cnt_ref, wpool_ref, pscale_ref,
               kq_ref, v_ref, la_ref, gate_ref, fft_ref, ycp_ref, *, row_len):
    nsub = x_ref.shape[0] // TOKEN_TILE

    def project(s):
        rows = slice(s * TOKEN_TILE, (s + 1) * TOKEN_TILE)
        h = _rms_mod(x_ref[rows, :], g_ref[...], mod_ref[1:2, :], mod_ref[0:1, :]).astype(BF16)
        return jnp.dot(h, w_ref[...], preferred_element_type=F32)

    p_next = project(0)
    for s in range(nsub):
        p, p_next = p_next, (project(s + 1) if s + 1 < nsub else None)
        _in_mixers(p, slice(s * TOKEN_TILE, (s + 1) * TOKEN_TILE), wa2_ref, ba2_ref, cw_ref, cb_ref, cnt_ref,
                   wpool_ref, pscale_ref, kq_ref, v_ref, la_ref, gate_ref, fft_ref, ycp_ref, row_len)


def _in_mixers(p, rows, wa2_ref, ba2_ref, cw_ref, cb_ref, cnt_ref, wpool_ref, pscale_ref,
               kq_ref, v_ref, la_ref, gate_ref, fft_ref, ycp_ref, row_len):
    t_rows = p.shape[0]
    kq_ref[rows, :GLA_QK] = p[:, IN_K:IN_K + GLA_QK]
    kq_ref[rows, GLA_QK:] = p[:, IN_Q:IN_Q + GLA_QK] * (GLA_DK ** -0.5)
    v_ref[rows, :] = p[:, IN_V:IN_V + W_GLA]
    gate_ref[rows, :] = p[:, IN_G:IN_G + W_GLA]
    fft_ref[rows, :] = p[:, IN_FFT:IN_FFT + W_FFT]
    z = jnp.dot(p[:, IN_A:IN_A + LANES].astype(BF16), wa2_ref[...], preferred_element_type=F32) + ba2_ref[...]
    la_ref[rows, :] = (jnp.minimum(z, 0.0) - jnp.log(1.0 + jnp.exp(-jnp.abs(z)))) * (1.0 / GLA_TAU)

    def pos(a):
        return lax.broadcasted_iota(jnp.int32, a.shape, 0) & (row_len - 1)

    def prev(a, s):
        return jnp.where(pos(a) >= s, pltpu.roll(a, s, 0), 0.0)

    def nxt(a, s):
        return jnp.where(pos(a) < row_len - s, pltpu.roll(a, t_rows - s, 0), 0.0)

    t = p[:, IN_CG:IN_CG + W_CONV] * p[:, IN_H:IN_H + W_CONV]
    cw = cw_ref[...]
    conv = prev(t, 1) * cw[0:1] + t * cw[1:2] + nxt(t, 1) * cw[2:3] + cb_ref[...]
    ycp_ref[rows, :W_CONV] = (p[:, IN_BG:IN_BG + W_CONV] * conv).astype(BF16)

    u = p[:, IN_POOL:IN_POOL + W_POOL]
    halves = []
    for side, steps in ((0, 1), (1, 3)):
        f = u[:, side * LANES:(side + 1) * LANES]
        g = prev(f, 1)
        sums = [g + f]
        for i in range(steps):
            f = f + nxt(f, 1 << i)
            g = g + prev(g, 1 << i)
            sums.append(g + f)
        lane = lax.broadcasted_iota(jnp.int32, f.shape, 1)
        halves.append(jnp.where(lane < POOL_DG, sums[-2], sums[-1]))
    tot = jnp.concatenate(halves, axis=1)
    pooled = tot / cnt_ref[...] - u
    yp = jnp.dot(pooled.astype(BF16), wpool_ref[...], preferred_element_type=F32) * pscale_ref[...]
    ycp_ref[rows, W_CONV:] = yp.astype(BF16)


def _pool_counts(t_rows, row_len):
    pos = np.arange(t_rows) % row_len
    cols = []
    for w in POOL_WINDOWS:
        lo = np.clip(pos - w // 2, 0, row_len - 1)
        hi = np.clip(pos + w // 2 - 1, 0, row_len - 1)
        cols.append(np.repeat((hi - lo + 1).astype(np.float32)[:, None], POOL_DG, axis=1))
    return np.concatenate(cols, axis=1)


def _in_call(xt, mod4, layer, mod_base, groups, params, *, row_len):
    n, d = xt.shape
    t = min(IN_TILE, n // groups)
    tiles_per_group = n // groups // t
    assert tiles_per_group * t * groups == n
    cnt = jnp.asarray(_pool_counts(TOKEN_TILE, row_len))
    row = lambda i: (i, 0)
    outs = [jax.ShapeDtypeStruct((n, 2 * GLA_QK), F32), jax.ShapeDtypeStruct((n, W_GLA), F32),
            jax.ShapeDtypeStruct((n, 2 * GLA_QK), F32), jax.ShapeDtypeStruct((n, W_GLA), F32),
            jax.ShapeDtypeStruct((n, W_FFT), F32), jax.ShapeDtypeStruct((n, W_CONV + W_POOL), BF16)]
    g, w, wa2, ba2, cw, cb, wpool, pscale = params
    return pl.pallas_call(
        functools.partial(_in_kernel, row_len=row_len),
        grid=(n // t,),
        in_specs=[pl.BlockSpec((t, d), row), _mod_spec(mod4, layer, mod_base, tiles_per_group),
                  _layer_spec(g, layer), _layer_spec(w, layer), _layer_spec(wa2, layer), _layer_spec(ba2, layer),
                  _layer_spec(cw, layer), _layer_spec(cb, layer), _const_spec(cnt.shape),
                  _layer_spec(wpool, layer), _layer_spec(pscale, layer)],
        out_specs=[pl.BlockSpec((t, o.shape[1]), row) for o in outs],
        out_shape=outs,
        compiler_params=_params("arbitrary"),
        name="in_proj",
    )(xt, mod4, g, w, wa2, ba2, cw, cb, cnt, wpool, pscale)


def _gla_block(kq_ref, v_ref, la_ref, s_ref, cum_ref, o_ref, consts, fwd):
    tri2, rexp, bdmask, hq, hv = consts
    k, q, v, la = kq_ref[:, :GLA_QK], kq_ref[:, GLA_QK:], v_ref[...], la_ref[...]
    t_rows = q.shape[0]
    c, sub = GLA_CHUNK, GLA_SUB
    nchunk = t_rows // c
    nt = (((1,), (1,)), ((), ()))
    tn = (((0,), (0,)), ((), ()))
    la_hi = la.astype(BF16)
    la_lo = (la - la_hi.astype(F32)).astype(BF16)
    cums = jnp.dot(tri2, jnp.concatenate([la_hi, la_lo], axis=1), preferred_element_type=F32)
    cum = (cums[:, :GLA_QK] + cums[:, GLA_QK:]) * LOG2_E
    cum_ref[...] = cum
    end = jnp.concatenate([jnp.broadcast_to(cum_ref[pl.ds(ci * c + (c - 1 if fwd else 0), 1), :], (c, GLA_QK))
                           for ci in range(nchunk)], axis=0)
    yield

    nblk = t_rows // sub

    def row_of_block(ref, width, j):
        cols = [jnp.concatenate([jnp.broadcast_to(ref[pl.ds(sub * blk + j, 1), c0:c0 + LANES], (sub, LANES))
                                 for blk in range(nblk)], axis=0) for c0 in range(0, width, LANES)]
        return cols[0] if len(cols) == 1 else jnp.concatenate(cols, axis=1)

    ii = lax.broadcasted_iota(jnp.int32, (t_rows, GLA_QK), 0) & (sub - 1)
    es = []
    for j in range(sub):
        keep = (ii >= j) if fwd else (ii <= j)
        w = jnp.exp2(jnp.where(keep, cum - row_of_block(cum_ref, GLA_QK, j), NEG_BIG))
        es.append((w * (q * row_of_block(kq_ref, GLA_QK, j))).astype(BF16))
    pr = jnp.dot(jnp.concatenate(es, axis=0), rexp, preferred_element_type=F32)
    yield

    qe = (q * jnp.exp2(cum)).astype(BF16)
    kd = (k * jnp.exp2(end - cum)).astype(BF16)
    v16 = v.astype(BF16)
    s = s_ref[0]
    o_parts = [None] * nchunk
    for ci in (range(nchunk) if fwd else reversed(range(nchunk))):
        rows = slice(ci * c, (ci + 1) * c)
        o_parts[ci] = lax.dot_general(qe[rows], s.astype(BF16), nt, preferred_element_type=F32)
        upd = lax.dot_general(v16[rows], kd[rows], tn, preferred_element_type=F32)
        s = s * jnp.exp2(end[ci * c:ci * c + 1]) + upd * bdmask
    s_ref[0] = s
    o = jnp.concatenate(o_parts, axis=0)

    half = t_rows // 2
    b = c // 2
    while b >= sub:
        npair = t_rows // (2 * b)
        refs = []
        for p in range(npair):
            r = 2 * b * p + (b - 1 if fwd else b)
            refs.append(jnp.broadcast_to(cum_ref[pl.ds(r, 1), :], (2 * b, GLA_QK)))
        e = jnp.exp2(-jnp.abs(cum - jnp.concatenate(refs, axis=0)))
        qfull, kfull = q * e, k * e
        first = [slice(2 * b * p, 2 * b * p + b) for p in range(npair)]
        second = [slice(2 * b * p + b, 2 * b * p + 2 * b) for p in range(npair)]
        qrows, krows = (second, first) if fwd else (first, second)
        qsel = jnp.concatenate([qfull[r] for r in qrows], axis=0)
        ksel = jnp.concatenate([kfull[r] for r in krows], axis=0).astype(BF16)
        vsel = jnp.concatenate([v[r] for r in krows], axis=0).astype(BF16)
        qst = jnp.concatenate([qsel * hq[h] for h in range(GLA_HEADS)], axis=0).astype(BF16)
        att = lax.dot_general(qst, ksel, nt, preferred_element_type=F32)
        ri = lax.broadcasted_iota(jnp.int32, att.shape, 0)
        ci = lax.broadcasted_iota(jnp.int32, att.shape, 1)
        att = jnp.where(((ri & (half - 1)) ^ ci) < b, att, 0.0).astype(BF16)
        att_k = jnp.concatenate([att[h * half:(h + 1) * half] for h in range(GLA_HEADS)], axis=1)
        v_k = jnp.concatenate([vsel * hv[h].astype(BF16) for h in range(GLA_HEADS)], axis=0)
        res = jnp.dot(att_k, v_k, preferred_element_type=F32)
        zero = jnp.zeros((b, W_GLA), F32)
        pieces = []
        for p in range(npair):
            piece = res[p * b:(p + 1) * b]
            pieces += [zero, piece] if fwd else [piece, zero]
        o = o + jnp.concatenate(pieces, axis=0)
        b //= 2
    yield

    for j in range(sub):
        o = o + pr[t_rows * j:t_rows * (j + 1)] * row_of_block(v_ref, W_GLA, j)
    o_ref[...] = o


def _gla_kernel(kqf_ref, vf_ref, laf_ref, kqb_ref, vb_ref, lab_ref, h0f_ref, h0b_ref,
                trif_ref, trib_ref, rexp_ref, bd_ref,
                of_ref, ob_ref, sf_ref, sb_ref, cum_ref):
    @pl.when(pl.program_id(1) == 0)
    def _():
        sf_ref[...] = h0f_ref[...]
        sb_ref[...] = h0b_ref[...]

    lq = lax.broadcasted_iota(jnp.int32, (1, GLA_QK), 1)
    lv = lax.broadcasted_iota(jnp.int32, (1, W_GLA), 1)
    hq = [jnp.where((lq >= h * GLA_DK) & (lq < (h + 1) * GLA_DK), 1.0, 0.0) for h in range(GLA_HEADS)]
    hv = [jnp.where((lv >= h * GLA_DV) & (lv < (h + 1) * GLA_DV), 1.0, 0.0) for h in range(GLA_HEADS)]
    rexp, bdmask = rexp_ref[...], bd_ref[...]
    nsub = kqf_ref.shape[0] // GLA_SUBBLOCK
    scans = []
    for r in range(nsub):
        lo, hi = pl.ds(r * GLA_SUBBLOCK, GLA_SUBBLOCK), pl.ds((nsub - 1 - r) * GLA_SUBBLOCK, GLA_SUBBLOCK)
        scans.append(_gla_block(kqf_ref.at[lo], vf_ref.at[lo], laf_ref.at[lo], sf_ref, cum_ref.at[2 * r],
                                of_ref.at[lo], (trif_ref[...], rexp, bdmask, hq, hv), True))
        scans.append(_gla_block(kqb_ref.at[hi], vb_ref.at[hi], lab_ref.at[hi], sb_ref, cum_ref.at[2 * r + 1],
                                ob_ref.at[hi], (trib_ref[...], rexp, bdmask, hq, hv), False))
    live = True
    while live:
        live = False
        for scan in scans:
            live = next(scan, "done") != "done" or live


def _gla_consts(t):
    c = GLA_CHUNK
    i = np.arange(t)
    same = (i[None, :] // c) == (i[:, None] // c)
    trif = (same & (i[None, :] <= i[:, None])).astype(np.float32)
    trib = (same & (i[None, :] >= i[:, None])).astype(np.float32)
    hk = np.arange(GLA_QK) // GLA_DK
    hd = np.arange(W_GLA) // GLA_DV
    rexp = (hk[:, None] == hd[None, :]).astype(np.float32)
    bd = (hd[:, None] == hk[None, :]).astype(np.float32)
    return (jnp.asarray(trif, BF16), jnp.asarray(trib, BF16), jnp.asarray(rexp, BF16), jnp.asarray(bd, F32))


def _gla_call(kq, v, la, h0f, h0b, *, batch):
    n = kq.shape[0]
    tg = min(GLA_BLOCK, n // batch)
    nb = n // batch // tg
    trif, trib, rexp, bd = _gla_consts(GLA_SUBBLOCK)
    fw = lambda b, i: (b * nb + i, 0)
    bw = lambda b, i: (b * nb + nb - 1 - i, 0)
    bw1 = lambda b, i: (b * nb + nb - 1 - i, 1)
    st = lambda b, i: (b, 0, 0)
    sshape = jax.ShapeDtypeStruct((batch, W_GLA, GLA_QK), F32)
    return pl.pallas_call(
        _gla_kernel,
        grid=(batch, nb),
        in_specs=[pl.BlockSpec((tg, 2 * GLA_QK), fw), pl.BlockSpec((tg, W_GLA), fw), pl.BlockSpec((tg, GLA_QK), fw),
                  pl.BlockSpec((tg, 2 * GLA_QK), bw), pl.BlockSpec((tg, W_GLA), bw), pl.BlockSpec((tg, GLA_QK), bw1),
                  pl.BlockSpec((1, W_GLA, GLA_QK), st), pl.BlockSpec((1, W_GLA, GLA_QK), st),
                  _const_spec(trif.shape), _const_spec(trib.shape), _const_spec(rexp.shape), _const_spec(bd.shape)],
        out_specs=[pl.BlockSpec((tg, W_GLA), fw), pl.BlockSpec((tg, W_GLA), bw),
                   pl.BlockSpec((1, W_GLA, GLA_QK), st), pl.BlockSpec((1, W_GLA, GLA_QK), st)],
        out_shape=[jax.ShapeDtypeStruct((n, W_GLA), F32), jax.ShapeDtypeStruct((n, W_GLA), F32), sshape, sshape],
        scratch_shapes=[pltpu.VMEM((2 * tg // GLA_SUBBLOCK, GLA_SUBBLOCK, GLA_QK), F32)],
        compiler_params=_params("arbitrary", "arbitrary"),
        name="gla",
    )(kq, v, la, kq, v, la, h0f, h0b, trif, trib, rexp, bd)


def _fft_tail(ab, cs_ref, wf_ref, norm):
    f = jnp.dot(ab, cs_ref[...].astype(BF16), preferred_element_type=F32) * norm
    return jnp.dot(f.astype(BF16), wf_ref[...], preferred_element_type=F32)


def _fft_kernel(kron_ref, tw_ref, dft_ref, cs_ref, wf_ref, u_ref, o_ref, y_ref, rows_ref, *, norm):
    n1, n2, width = u_ref.shape
    halves = rows_ref.shape[0]
    g = FFT_GROUP
    kron = kron_ref[...].astype(BF16)

    def stage1(blk, carry):
        cols = pl.ds(pl.multiple_of(blk * g, g), g)
        ub = u_ref[:, cols, :].reshape(n1 * g, width).astype(BF16)
        y_ref[:, cols, :] = jnp.dot(kron, ub, preferred_element_type=F32).reshape(2 * n1, g, width)
        return carry

    lax.fori_loop(0, n2 // g, stage1, 0)
    c2, s2 = dft_ref[0], dft_ref[1]

    def stage2(grp, carry):
        parts = []
        for j in range(g):
            k1 = grp * g + j
            tc, ts = tw_ref[k1, 0:1, :], tw_ref[k1, 1:2, :]
            mc = c2 * tc - s2 * ts
            ms = s2 * tc + c2 * ts
            m = jnp.concatenate([jnp.concatenate([mc, ms], axis=1),
                                 jnp.concatenate([-ms, mc], axis=1)], axis=0).astype(BF16)
            ys = jnp.concatenate([y_ref[k1], y_ref[n1 + k1]], axis=0).astype(BF16)
            z = jnp.dot(m, ys, preferred_element_type=F32)
            parts.append(jnp.concatenate([z[:n2], z[n2:]], axis=1).astype(BF16))
        out = _fft_tail(jnp.concatenate(parts, axis=0), cs_ref, wf_ref, norm)
        for j in range(g):
            for h in range(halves):
                rows_ref[h, pl.ds(j, n2, stride=g), :] = out[j * n2:(j + 1) * n2, h * LANES:(h + 1) * LANES]
        cols = pl.ds(pl.multiple_of(grp * g, g), g)
        for h in range(halves):
            o_ref[:, cols, h * LANES:(h + 1) * LANES] = rows_ref[h].reshape(n2, g, LANES)
        return carry

    lax.fori_loop(0, n1 // g, stage2, 0)


def _fft_direct_kernel(m_ref, u_ref, cs_ref, wf_ref, o_ref, *, norm):
    n = u_ref.shape[0]
    z = jnp.dot(m_ref[...].astype(BF16), u_ref[...].astype(BF16), preferred_element_type=F32)
    o_ref[...] = _fft_tail(jnp.concatenate([z[:n], z[n:]], axis=1).astype(BF16), cs_ref, wf_ref, norm)


def _dft_cos_sin(n):
    k = np.arange(n, dtype=np.int64)
    ang = 2.0 * np.pi * ((k[:, None] * k[None, :]) % n).astype(np.float64) / n
    return np.cos(ang), np.sin(ang)


def _channel_dft():
    c, s = _dft_cos_sin(FFT_DG)
    eye = np.eye(FFT_GROUPS)
    return jnp.asarray(np.concatenate([np.kron(eye, c), np.kron(eye, s)], axis=0), F32)


def _fft_latent_call(u, wf_bd, layer, *, batch):
    n = u.shape[0] // batch
    n1, n2 = n // FFT_N2, FFT_N2
    halves = W_FFT // LANES
    g = FFT_GROUP
    assert n1 % g == 0 and n2 % g == 0
    c1, s1 = _dft_cos_sin(n1)
    kron =jnp.asarray(np.kron(np.concatenate([c1, -s1], axis=0), np.eye(g)), F32)
    tw_ang = 2.0 * np.pi * (np.arange(n1, dtype=np.int64)[:, None] * np.arange(n2, dtype=np.int64)[None, :]) / n
    tw = jnp.asarray(np.stack([np.cos(tw_ang), np.sin(tw_ang)], axis=1), F32)
    dft2 = jnp.asarray(np.stack(_dft_cos_sin(n2), axis=0), F32)
    cs = _channel_dft()
    norm = float(1.0 / np.sqrt(n * FFT_DG))
    out = pl.pallas_call(
        functools.partial(_fft_kernel, norm=norm),
        grid=(batch,),
        in_specs=[_const_spec(kron.shape), _const_spec(tw.shape), _const_spec(dft2.shape), _const_spec(cs.shape),
                  _layer_spec(wf_bd, layer),
                  pl.BlockSpec((None, n1, n2, W_FFT), lambda b: (b, 0, 0, 0), pipeline_mode=pl.Buffered(1))],
        out_specs=pl.BlockSpec((None, n2, n1, W_FFT), lambda b: (b, 0, 0, 0)),
        out_shape=jax.ShapeDtypeStruct((batch, n2, n1, W_FFT), F32),
        scratch_shapes=[pltpu.VMEM((2 * n1, n2, W_FFT), F32), pltpu.VMEM((halves, n2 * g, LANES), F32)],
        compiler_params=_params("arbitrary"),
        name="fft",
    )(kron, tw, dft2, cs, wf_bd, u.reshape(batch, n1, n2, W_FFT))
    return out.reshape(batch * n, W_FFT)


def _fft_direct_call(u, wf_bd, layer, *, batch):
    n = u.shape[0] // batch
    c, s = _dft_cos_sin(n)
    m = jnp.asarray(np.concatenate([c, -s], axis=0), F32)
    cs = _channel_dft()
    norm = float(1.0 / np.sqrt(n * FFT_DG))
    return pl.pallas_call(
        functools.partial(_fft_direct_kernel, norm=norm),
        grid=(batch,),
        in_specs=[_const_spec(m.shape), pl.BlockSpec((n, W_FFT), lambda b: (b, 0)),
                  _const_spec(cs.shape), _layer_spec(wf_bd, layer)],
        out_specs=pl.BlockSpec((n, W_FFT), lambda b: (b, 0)),
        out_shape=jax.ShapeDtypeStruct((batch * n, W_FFT), F32),
        compiler_params=_params("arbitrary"),
        name="fft_direct",
    )(m, u, cs, wf_bd)


def _load_cast(w_hbm, layer, dst_ref, stage_ref, sem_ref):
    nslots, step = stage_ref.shape[0], stage_ref.shape[1]
    nchunks = dst_ref.shape[0] // step

    def copy(k):
        return pltpu.make_async_copy(w_hbm.at[layer, pl.ds(k * step, step), :], stage_ref.at[k % nslots],
                                     sem_ref.at[k % nslots])

    for k in range(min(nslots - 1, nchunks)):
        copy(k).start()
    for k in range(nchunks):
        if k + nslots - 1 < nchunks:
            copy(k + nslots - 1).start()
        copy(k).wait()
        dst_ref[pl.ds(k * step, step), :] = stage_ref[k % nslots].astype(BF16)


def _out_kernel(x_ref, mod_ref, of_ref, ob_ref, gate_ref, fft_ref, ycp_ref, glag_ref, ones_ref, wout_hbm,
                n2g_ref, wup_hbm, cw_ref, cb_ref, wdn_hbm, fing_ref, o_ref,
                x1_ref, h2_ref, au_ref, h_ref, acc_ref, wout_ref, wup_ref, wdn_ref, stage_up_ref, stage_sq_ref,
                sem_up_ref, sem_sq_ref, *, row_len, final, layer):
    @pl.when(pl.program_id(0) == 0)
    def _():
        _load_cast(wout_hbm, layer, wout_ref, stage_sq_ref, sem_sq_ref)
        _load_cast(wup_hbm, layer, wup_ref, stage_up_ref, sem_up_ref)
        _load_cast(wdn_hbm, layer, wdn_ref, stage_sq_ref, sem_sq_ref)

    _out_head(x_ref, mod_ref, of_ref, ob_ref, gate_ref, fft_ref, ycp_ref, glag_ref, ones_ref, wout_ref,
              n2g_ref, x1_ref, h2_ref)
    _out_ffn(mod_ref, wup_ref, cw_ref, cb_ref, wdn_ref, fing_ref, o_ref, x1_ref, h2_ref, au_ref, h_ref,
             acc_ref, row_len, final)


def _out_head(x_ref, mod_ref, of_ref, ob_ref, gate_ref, fft_ref, ycp_ref, glag_ref, ones_ref, wout_ref, n2g_ref,
              x1_ref, h2_ref):
    for r0 in range(0, x_ref.shape[0], TOKEN_TILE):
        rows = slice(r0, r0 + TOKEN_TILE)
        o = of_ref[rows, :] + ob_ref[rows, :]
        osq = o * o
        hi = osq.astype(BF16)
        lo = (osq - hi.astype(F32)).astype(BF16)
        ms = (jnp.dot(hi, ones_ref[...], preferred_element_type=F32)
              + jnp.dot(lo, ones_ref[...], preferred_element_type=F32)) * (1.0 / GLA_DV)
        gl = (o * lax.rsqrt(ms + EPS)) * glag_ref[...] * _silu(gate_ref[rows, :])
        ymix = jnp.concatenate([gl.astype(BF16), fft_ref[rows, :].astype(BF16), ycp_ref[rows, :]], axis=1)
        x1 = x_ref[rows, :] + mod_ref[2:3, :] * jnp.dot(ymix, wout_ref[...], preferred_element_type=F32)
        x1_ref[rows, :] = x1
        h2_ref[rows, :] = _rms_mod(x1, n2g_ref[...], mod_ref[4:5, :], mod_ref[3:4, :]).astype(BF16)


def _out_ffn(mod_ref, wup_ref, cw_ref, cb_ref, wdn_ref, fing_ref, o_ref, x1_ref, h2_ref, au_ref, h_ref, acc_ref,
             row_len, final):
    t_rows = o_ref.shape[0]
    d_ff = wdn_ref.shape[0]
    nchunk = d_ff // FFN_CHUNK
    acc_ref[...] = jnp.zeros(acc_ref.shape, F32)

    def up(c, slot, row_step=None):
        row_step = row_step or t_rows
        for r0 in range(0, t_rows, row_step):
            rows = slice(r0, r0 + row_step)
            for half in range(2):
                cols = slice(half * d_ff + c * FFN_CHUNK, half * d_ff + (c + 1) * FFN_CHUNK)
                au_ref[slot, rows, half * FFN_CHUNK:(half + 1) * FFN_CHUNK] = jnp.dot(
                    h2_ref[rows, :], wup_ref[:, cols], preferred_element_type=F32)

    def elem(c, slot):
        cols = slice(c * FFN_CHUNK, (c + 1) * FFN_CHUNK)
        pos = lax.broadcasted_iota(jnp.int32, (t_rows, FFN_CHUNK), 0) & (row_len - 1)
        a = au_ref[slot, :, :FFN_CHUNK]
        cw = cw_ref[:, cols]
        a = (jnp.where(pos == 0, 0.0, pltpu.roll(a, 1, 0)) * cw[0:1] + a * cw[1:2]
             + jnp.where(pos == row_len - 1, 0.0, pltpu.roll(a, t_rows - 1, 0)) * cw[2:3] + cb_ref[:, cols])
        h_ref[slot] = (_silu(a) * au_ref[slot, :, FFN_CHUNK:]).astype(BF16)

    def down(c, slot):
        rows = slice(c * FFN_CHUNK, (c + 1) * FFN_CHUNK)
        acc_ref[...] += jnp.dot(h_ref[slot], wdn_ref[rows, :], preferred_element_type=F32)

    def stage(c):
        up(c + 1, 0)
        elem(c, 1)
        down(c - 1, 0)
        up(c + 2, 1)
        elem(c + 1, 0)
        down(c, 1)

    assert nchunk % 2 == 1 and nchunk >= 3
    up(0, 0, TOKEN_TILE)
    up(1, 1, TOKEN_TILE)
    elem(0, 0)

    for i in range((nchunk - 3) // 2):
        stage(2 * i + 1)
    c = nchunk - 2
    up(c + 1, 0)
    elem(c, 1)
    down(c - 1, 0)
    elem(c + 1, 0)
    down(c, 1)
    down(c + 1, 0)
    x2 = x1_ref[...] + mod_ref[5:6, :] * acc_ref[...]
    if final:
        ms2 = jnp.mean(x2 * x2, axis=-1, keepdims=True)
        x2 = (x2 * lax.rsqrt(ms2 + EPS)) * fing_ref[...]
    o_ref[...] = x2


def _out_call(xt, mod4, layer, mod_base, groups, o_f, o_b, gate, yfft, ycp, params, fing, *, row_len, final):
    n, d = xt.shape
    t = OUT_TILE
    tiles_per_group = n // groups // t
    assert tiles_per_group * t * groups == n
    hd = np.arange(W_GLA) // GLA_DV
    ones = jnp.asarray((hd[:, None] == hd[None, :]).astype(np.float32), BF16)
    row = lambda i: (i, 0)
    acts = [xt, o_f, o_b, gate, yfft, ycp]
    glag, wout, n2g, wup, cw, cb, wdn = params
    consts = [glag, ones, wout, n2g, wup, cw, cb, wdn, fing]

    def const_spec(a):
        if a is wout or a is wup or a is wdn:
            return pl.BlockSpec(memory_space=pl.ANY)
        return _const_spec(a.shape) if a is ones or a is fing else _layer_spec(a, layer)

    assert wout.shape[1] % WEIGHT_STAGE_ROWS == 0 and wdn.shape[1] % WEIGHT_STAGE_ROWS == 0
    assert wup.shape[1] % WEIGHT_STAGE_ROWS_WIDE == 0
    return pl.pallas_call(
        functools.partial(_out_kernel, row_len=row_len, final=final, layer=layer),
        grid=(n // t,),
        in_specs=[pl.BlockSpec((t, d), row), _mod_spec(mod4, layer, mod_base, tiles_per_group)]
                 + [pl.BlockSpec((t, a.shape[1]), row) for a in acts[1:]]
                 + [const_spec(a) for a in consts],
        out_specs=pl.BlockSpec((t, d), row),
        out_shape=jax.ShapeDtypeStruct((n, d), F32),
        scratch_shapes=[pltpu.VMEM((t, d), F32), pltpu.VMEM((t, d), BF16),
                        pltpu.VMEM((2, t, 2 * FFN_CHUNK), F32), pltpu.VMEM((2, t, FFN_CHUNK), BF16),
                        pltpu.VMEM((t, d), F32),
                        pltpu.VMEM(wout.shape[1:], BF16), pltpu.VMEM(wup.shape[1:], BF16),
                        pltpu.VMEM(wdn.shape[1:], BF16),
                        pltpu.VMEM((WEIGHT_STAGE_SLOTS, WEIGHT_STAGE_ROWS_WIDE, wup.shape[2]), F32),
                        pltpu.VMEM((WEIGHT_STAGE_SLOTS, WEIGHT_STAGE_ROWS, d), F32),
                        pltpu.SemaphoreType.DMA((WEIGHT_STAGE_SLOTS,)),
                        pltpu.SemaphoreType.DMA((WEIGHT_STAGE_SLOTS,))],
        compiler_params=_params("arbitrary"),
        name="out_ffn",
    )(xt, mod4, *acts[1:], *consts)


def _block_diag(w):
    dep, g, a, b = w.shape
    eye = jnp.eye(g, dtype=w.dtype)
    return (eye[None, :, None, :, None] * w[:, :, :, None, :]).reshape(dep, g * a, g * b)


def _w_in_layout_kernel(wt_ref, o_ref):
    a0, a1 = COL_STARTS[2], COL_STARTS[4]
    wt = wt_ref[...]
    pad = jnp.zeros((IN_COLS - wt.shape[0], wt.shape[1]), wt.dtype)
    o_ref[...] = jnp.concatenate([wt[:a0], wt[a1:], wt[a0:a1], pad], axis=0).T.astype(BF16)


def _w_in_layout_call(w_in):
    dep, d, cols = w_in.shape
    rows = TOKEN_TILE
    return pl.pallas_call(
        _w_in_layout_kernel,
        grid=(dep, d // rows),
        in_specs=[pl.BlockSpec((None, cols, rows), lambda i, j: (i, 0, j))],
        out_specs=pl.BlockSpec((None, rows, IN_COLS), lambda i, j: (i, j, 0)),
        out_shape=jax.ShapeDtypeStruct((dep, d, IN_COLS), BF16),
        compiler_params=_params("arbitrary", "arbitrary"),
        name="w_in_layout",
    )(jnp.swapaxes(w_in, 1, 2))


def _prep_params(norm1_g, norm2_g, w_in, gla_w_a2, gla_b_a2, gla_norm_g, fft_w, conv_w, conv_b, pool_w,
                 pool_scale, w_out, ffn_w_up, ffn_conv_w, ffn_conv_b, ffn_w_down):
    dep = w_in.shape[0]
    w_in_r = _w_in_layout_call(w_in)
    wa2 = jnp.zeros((dep, LANES, 2 * GLA_QK), F32)
    wa2 = wa2.at[:, 0:GLA_RANK, 0:GLA_QK].set(gla_w_a2[:, 0])
    wa2 = wa2.at[:, GLA_RANK:2 * GLA_RANK, GLA_QK:].set(gla_w_a2[:, 1])
    in_params = (norm1_g[:, None, :], w_in_r, wa2.astype(BF16), gla_b_a2.reshape(dep, 1, 2 * GLA_QK),
                 conv_w, conv_b[:, None, :], _block_diag(pool_w).astype(BF16), pool_scale[:, None, :])
    out_params = (jnp.tile(gla_norm_g, (1, GLA_HEADS))[:, None, :], w_out, norm2_g[:, None, :],
                  ffn_w_up, ffn_conv_w, ffn_conv_b[:, None, :], ffn_w_down)
    return in_params, out_params, _block_diag(fft_w).astype(BF16)


def kernel(x, c, ctx, c_ctx, norm1_g, norm2_g, w_mod, b_mod, w_in, gla_w_a2, gla_b_a2, gla_norm_g,
           fft_w, conv_w, conv_b, pool_w, pool_scale, w_out, ffn_w_up, ffn_conv_w, ffn_conv_b,
           ffn_w_down, final_norm_g):
    batch, seq, d = x.shape
    ctx_len = ctx.shape[1]
    depth = w_mod.shape[0]
    assert seq % GLA_BLOCK == 0 and seq % FFT_N2 == 0 and ctx_len == TOKEN_TILE

    cc = jnp.concatenate([c, c_ctx[None, :], jnp.zeros((MOD_ROWS - batch - 1, d), F32)], axis=0)
    mod4 = _mod_call(cc, w_mod, b_mod).reshape(depth, MOD_ROWS, 6, d)
    in_params, out_params, wf_bd = _prep_params(
        norm1_g, norm2_g, w_in, gla_w_a2, gla_b_a2, gla_norm_g, fft_w, conv_w, conv_b, pool_w, pool_scale,
        w_out, ffn_w_up, ffn_conv_w, ffn_conv_b, ffn_w_down)

    xt = x.reshape(batch * seq, d)
    xc = ctx.reshape(batch * ctx_len, d)
    zero_state = jnp.zeros((batch, W_GLA, GLA_QK), F32)
    fing = final_norm_g.reshape(1, d)
    for i in range(depth):
        last = i == depth - 1
        kq, v, la, gate, ufft, ycp = _in_call(xc, mod4, i, batch, 1, in_params, row_len=ctx_len)
        o_f, o_b, s_f, s_b = _gla_call(kq, v, la, zero_state, zero_state, batch=batch)
        if not last:
            yfft = _fft_direct_call(ufft, wf_bd, i, batch=batch)
            xc = _out_call(xc, mod4, i, batch, 1, o_f, o_b, gate, yfft, ycp, out_params, fing,
                           row_len=ctx_len, final=False)

        kq, v, la, gate, ufft, ycp = _in_call(xt, mod4, i, 0, batch, in_params, row_len=GRID_W)
        o_f, o_b, _, _ = _gla_call(kq, v, la, s_f, s_b, batch=batch)
        yfft = _fft_latent_call(ufft, wf_bd, i, batch=batch)
        xt = _out_call(xt, mod4, i, 0, batch, o_f, o_b, gate, yfft, ycp, out_params, fing,
                       row_len=GRID_W, final=last)
    return xt.reshape(batch, seq, d)
```

```python
import functools

import numpy as np
import jax
import jax.numpy as jnp
from jax import lax
from jax.experimental import pallas as pl
from jax.experimental.pallas import tpu as pltpu

F32 = jnp.float32
BF16 = jnp.bfloat16

GRID_W = 64
EPS = 1e-6
GLA_HEADS = 4
GLA_DK = 32
GLA_DV = 64
GLA_QK = GLA_HEADS * GLA_DK
W_GLA = GLA_HEADS * GLA_DV
GLA_RANK = 16
GLA_TAU = 16.0
FFT_GROUPS = 4
FFT_DG = 64
W_FFT = FFT_GROUPS * FFT_DG
W_CONV = 256
POOL_WINDOWS = (2, 4, 8, 16)
POOL_DG = 64
W_POOL = len(POOL_WINDOWS) * POOL_DG
COL_SIZES = (GLA_QK, W_GLA, GLA_RANK, GLA_RANK, GLA_QK, W_GLA, W_FFT, W_CONV, W_CONV, W_CONV, W_POOL)
COL_STARTS = tuple(int(s) for s in np.cumsum((0,) + COL_SIZES)[:-1])

LANES = 128
VMEM_LIMIT_BYTES = 56 * 1024 * 1024

TOKEN_TILE = 256
IN_TILE = 1024
OUT_TILE = 512
GLA_CHUNK = 64
GLA_SUB = 8
GLA_SUBBLOCK = 256
GLA_BLOCK = 1024
FFN_CHUNK = 256
WEIGHT_STAGE_ROWS = 128
WEIGHT_STAGE_ROWS_WIDE = 64
WEIGHT_STAGE_SLOTS = 4
FFT_N2 = 128
FFT_GROUP = 8
NEG_BIG = -1e30
LOG2_E = 1.4426950408889634

IN_K, IN_V, IN_Q, IN_G, IN_FFT, IN_H, IN_BG, IN_CG, IN_POOL, IN_A = (
    0, 128, 384, 512, 768, 1024, 1280, 1536, 1792, 2048)
IN_COLS = IN_A + LANES
MOD_COL_TILE = 1536
MOD_ROWS = 8


def _const_spec(shape):
    nd = len(shape)
    return pl.BlockSpec(shape, lambda *_: (0,) * nd, pipeline_mode=pl.Buffered(1))


def _layer_spec(arr, layer):
    nd = arr.ndim
    return pl.BlockSpec((None,) + arr.shape[1:], lambda *_: (layer,) + (0,) * (nd - 1),
                        pipeline_mode=pl.Buffered(1))


def _mod_spec(mod4, layer, base, tiles_per_group):
    return pl.BlockSpec((None, None) + mod4.shape[2:], lambda i: (layer, base + i // tiles_per_group, 0, 0))


def _params(*sem):
    return pltpu.CompilerParams(dimension_semantics=sem, vmem_limit_bytes=VMEM_LIMIT_BYTES)


def _rms_mod(x, g, scale, shift):
    ms = jnp.mean(x * x, axis=-1, keepdims=True)
    return (x * lax.rsqrt(ms + EPS)) * g * (1.0 + scale) + shift


def _silu(a):
    return a * jax.nn.sigmoid(a)


def _mod_kernel(c_ref, w_ref, b_ref, o_ref):
    s = _silu(c_ref[...]).astype(BF16)
    o_ref[0] = jnp.dot(s, w_ref[0].astype(BF16), preferred_element_type=F32) + b_ref[0]


def _mod_call(cc, w_mod, b_mod):
    depth, d, n = w_mod.shape
    tn = MOD_COL_TILE
    return pl.pallas_call(
        _mod_kernel,
        grid=(depth, n // tn),
        in_specs=[pl.BlockSpec(cc.shape, lambda i, j: (0, 0)),
                  pl.BlockSpec((1, d, tn), lambda i, j: (i, 0, j)),
                  pl.BlockSpec((1, 1, tn), lambda i, j: (i, 0, j))],
        out_specs=pl.BlockSpec((1, cc.shape[0], tn), lambda i, j: (i, 0, j)),
        out_shape=jax.ShapeDtypeStruct((depth, cc.shape[0], n), F32),
        compiler_params=_params("arbitrary", "arbitrary"),
        name="modulation",
    )(cc, w_mod, b_mod.reshape(depth, 1, n))


def _in_kernel(x_ref, mod_ref, g_ref, w_ref, wa2_ref, ba2_ref, cw_ref, cb_ref, cnt_ref, wpool_ref, pscale_ref,
               kq_ref, v_ref, la_ref, gate_ref, fft_ref, ycp_ref, *, row_len):
    nsub = x_ref.shape[0] // TOKEN_TILE

    def project(s):
        rows = slice(s * TOKEN_TILE, (s + 1) * TOKEN_TILE)
        h = _rms_mod(x_ref[rows, :], g_ref[...], mod_ref[1:2, :], mod_ref[0:1, :]).astype(BF16)
        return jnp.dot(h, w_ref[...], preferred_element_type=F32)

    p_next = project(0)
    for s in range(nsub):
        p, p_next = p_next, (project(s + 1) if s + 1 < nsub else None)
        _in_mixers(p, slice(s * TOKEN_TILE, (s + 1) * TOKEN_TILE), wa2_ref, ba2_ref, cw_ref, cb_ref, cnt_ref,
                   wpool_ref, pscale_ref, kq_ref, v_ref, la_ref, gate_ref, fft_ref, ycp_ref, row_len)


def _in_mixers(p, rows, wa2_ref, ba2_ref, cw_ref, cb_ref, cnt_ref, wpool_ref, pscale_ref,
               kq_ref, v_ref, la_ref, gate_ref, fft_ref, ycp_ref, row_len):
    t_rows = p.shape[0]
    kq_ref[rows, :GLA_QK] = p[:, IN_K:IN_K + GLA_QK]
    kq_ref[rows, GLA_QK:] = p[:, IN_Q:IN_Q + GLA_QK] * (GLA_DK ** -0.5)
    v_ref[rows, :] = p[:, IN_V:IN_V + W_GLA]
    gate_ref[rows, :] = p[:, IN_G:IN_G + W_GLA]
    fft_ref[rows, :] = p[:, IN_FFT:IN_FFT + W_FFT]
    z = jnp.dot(p[:, IN_A:IN_A + LANES].astype(BF16), wa2_ref[...], preferred_element_type=F32) + ba2_ref[...]
    la_ref[rows, :] = (jnp.minimum(z, 0.0) - jnp.log(1.0 + jnp.exp(-jnp.abs(z)))) * (1.0 / GLA_TAU)

    def pos(a):
        return lax.broadcasted_iota(jnp.int32, a.shape, 0) & (row_len - 1)

    def prev(a, s):
        return jnp.where(pos(a) >= s, pltpu.roll(a, s, 0), 0.0)

    def nxt(a, s):
        return jnp.where(pos(a) < row_len - s, pltpu.roll(a, t_rows - s, 0), 0.0)

    t = p[:, IN_CG:IN_CG + W_CONV] * p[:, IN_H:IN_H + W_CONV]
    cw = cw_ref[...]
    conv = prev(t, 1) * cw[0:1] + t * cw[1:2] + nxt(t, 1) * cw[2:3] + cb_ref[...]
    ycp_ref[rows, :W_CONV] = (p[:, IN_BG:IN_BG + W_CONV] * conv).astype(BF16)

    u = p[:, IN_POOL:IN_POOL + W_POOL]
    halves = []
    for side, steps in ((0, 1), (1, 3)):
        f = u[:, side * LANES:(side + 1) * LANES]
        g = prev(f, 1)
        sums = [g + f]
        for i in range(steps):
            f = f + nxt(f, 1 << i)
            g = g + prev(g, 1 << i)
            sums.append(g + f)
        lane = lax.broadcasted_iota(jnp.int32, f.shape, 1)
        halves.append(jnp.where(lane < POOL_DG, sums[-2], sums[-1]))
    tot = jnp.concatenate(halves, axis=1)
    pooled = tot / cnt_ref[...] - u
    yp = jnp.dot(pooled.astype(BF16), wpool_ref[...], preferred_element_type=F32) * pscale_ref[...]
    ycp_ref[rows, W_CONV:] = yp.astype(BF16)


def _pool_counts(t_rows, row_len):
    pos = np.arange(t_rows) % row_len
    cols = []
    for w in POOL_WINDOWS:
        lo = np.clip(pos - w // 2, 0, row_len - 1)
        hi = np.clip(pos + w // 2 - 1, 0, row_len - 1)
        cols.append(np.repeat((hi - lo + 1).astype(np.float32)[:, None], POOL_DG, axis=1))
    return np.concatenate(cols, axis=1)


def _in_call(xt, mod4, layer, mod_base, groups, params, *, row_len):
    n, d = xt.shape
    t = min(IN_TILE, n // groups)
    tiles_per_group = n // groups // t
    assert tiles_per_group * t * groups == n
    cnt = jnp.asarray(_pool_counts(TOKEN_TILE, row_len))
    row = lambda i: (i, 0)
    outs = [jax.ShapeDtypeStruct((n, 2 * GLA_QK), F32), jax.ShapeDtypeStruct((n, W_GLA), F32),
            jax.ShapeDtypeStruct((n, 2 * GLA_QK), F32), jax.ShapeDtypeStruct((n, W_GLA), F32),
            jax.ShapeDtypeStruct((n, W_FFT), F32), jax.ShapeDtypeStruct((n, W_CONV + W_POOL), BF16)]
    g, w, wa2, ba2, cw, cb, wpool, pscale = params
    return pl.pallas_call(
        functools.partial(_in_kernel, row_len=row_len),
        grid=(n // t,),
        in_specs=[pl.BlockSpec((t, d), row), _mod_spec(mod4, layer, mod_base, tiles_per_group),
                  _layer_spec(g, layer), _layer_spec(w, layer), _layer_spec(wa2, layer), _layer_spec(ba2, layer),
                  _layer_spec(cw, layer), _layer_spec(cb, layer), _const_spec(cnt.shape),
                  _layer_spec(wpool, layer), _layer_spec(pscale, layer)],
        out_specs=[pl.BlockSpec((t, o.shape[1]), row) for o in outs],
        out_shape=outs,
        compiler_params=_params("arbitrary"),
        name="in_proj",
    )(xt, mod4, g, w, wa2, ba2, cw, cb, cnt, wpool, pscale)


def _gla_block(kq_ref, v_ref, la_ref, s_ref, cum_ref, o_ref, consts, fwd):
    tri2, rexp, bdmask, hq, hv = consts
    k, q, v, la = kq_ref[:, :GLA_QK], kq_ref[:, GLA_QK:], v_ref[...], la_ref[...]
    t_rows = q.shape[0]
    c, sub = GLA_CHUNK, GLA_SUB
    nchunk = t_rows // c
    nt = (((1,), (1,)), ((), ()))
    tn = (((0,), (0,)), ((), ()))
    la_hi = la.astype(BF16)
    la_lo = (la - la_hi.astype(F32)).astype(BF16)
    cums = jnp.dot(tri2, jnp.concatenate([la_hi, la_lo], axis=1), preferred_element_type=F32)
    cum = (cums[:, :GLA_QK] + cums[:, GLA_QK:]) * LOG2_E
    cum_ref[...] = cum
    end = jnp.concatenate([jnp.broadcast_to(cum_ref[pl.ds(ci * c + (c - 1 if fwd else 0), 1), :], (c, GLA_QK))
                           for ci in range(nchunk)], axis=0)
    yield

    nblk = t_rows // sub

    def row_of_block(ref, width, j):
        cols = [jnp.concatenate([jnp.broadcast_to(ref[pl.ds(sub * blk + j, 1), c0:c0 + LANES], (sub, LANES))
                                 for blk in range(nblk)], axis=0) for c0 in range(0, width, LANES)]
        return cols[0] if len(cols) == 1 else jnp.concatenate(cols, axis=1)

    ii = lax.broadcasted_iota(jnp.int32, (t_rows, GLA_QK), 0) & (sub - 1)
    es = []
    for j in range(sub):
        keep = (ii >= j) if fwd else (ii <= j)
        w = jnp.exp2(jnp.where(keep, cum - row_of_block(cum_ref, GLA_QK, j), NEG_BIG))
        es.append((w * (q * row_of_block(kq_ref, GLA_QK, j))).astype(BF16))
    pr = jnp.dot(jnp.concatenate(es, axis=0), rexp, preferred_element_type=F32)
    yield

    qe = (q * jnp.exp2(cum)).astype(BF16)
    kd = (k * jnp.exp2(end - cum)).astype(BF16)
    v16 = v.astype(BF16)
    s = s_ref[0]
    o_parts = [None] * nchunk
    for ci in (range(nchunk) if fwd else reversed(range(nchunk))):
        rows = slice(ci * c, (ci + 1) * c)
        o_parts[ci] = lax.dot_general(qe[rows], s.astype(BF16), nt, preferred_element_type=F32)
        upd = lax.dot_general(v16[rows], kd[rows], tn, preferred_element_type=F32)
        s = s * jnp.exp2(end[ci * c:ci * c + 1]) + upd * bdmask
    s_ref[0] = s
    o = jnp.concatenate(o_parts, axis=0)

    half = t_rows // 2
    b = c // 2
    while b >= sub:
        npair = t_rows // (2 * b)
        refs = []
        for p in range(npair):
            r = 2 * b * p + (b - 1 if fwd else b)
            refs.append(jnp.broadcast_to(cum_ref[pl.ds(r, 1), :], (2 * b, GLA_QK)))
        e = jnp.exp2(-jnp.abs(cum - jnp.concatenate(refs, axis=0)))
        qfull, kfull = q * e, k * e
        first = [slice(2 * b * p, 2 * b * p + b) for p in range(npair)]
        second = [slice(2 * b * p + b, 2 * b * p + 2 * b) for p in range(npair)]
        qrows, krows = (second, first) if fwd else (first, second)
        qsel = jnp.concatenate([qfull[r] for r in qrows], axis=0)
        ksel = jnp.concatenate([kfull[r] for r in krows], axis=0).astype(BF16)
        vsel = jnp.concatenate([v[r] for r in krows], axis=0).astype(BF16)
        qst = jnp.concatenate([qsel * hq[h] for h in range(GLA_HEADS)], axis=0).astype(BF16)
        att = lax.dot_general(qst, ksel, nt, preferred_element_type=F32)
        ri = lax.broadcasted_iota(jnp.int32, att.shape, 0)
        ci = lax.broadcasted_iota(jnp.int32, att.shape, 1)
        att = jnp.where(((ri & (half - 1)) ^ ci) < b, att, 0.0).astype(BF16)
        att_k = jnp.concatenate([att[h * half:(h + 1) * half] for h in range(GLA_HEADS)], axis=1)
        v_k = jnp.concatenate([vsel * hv[h].astype(BF16) for h in range(GLA_HEADS)], axis=0)
        res = jnp.dot(att_k, v_k, preferred_element_type=F32)
        zero = jnp.zeros((b, W_GLA), F32)
        pieces = []
        for p in range(npair):
            piece = res[p * b:(p + 1) * b]
            pieces += [zero, piece] if fwd else [piece, zero]
        o = o + jnp.concatenate(pieces, axis=0)
        b //= 2
    yield

    for j in range(sub):
        o = o + pr[t_rows * j:t_rows * (j + 1)] * row_of_block(v_ref, W_GLA, j)
    o_ref[...] = o


def _gla_kernel(kqf_ref, vf_ref, laf_ref, kqb_ref, vb_ref, lab_ref, h0f_ref, h0b_ref,
                trif_ref, trib_ref, rexp_ref, bd_ref,
                of_ref, ob_ref, sf_ref, sb_ref, cum_ref):
    @pl.when(pl.program_id(1) == 0)
    def _():
        sf_ref[...] = h0f_ref[...]
        sb_ref[...] = h0b_ref[...]

    lq = lax.broadcasted_iota(jnp.int32, (1, GLA_QK), 1)
    lv = lax.broadcasted_iota(jnp.int32, (1, W_GLA), 1)
    hq = [jnp.where((lq >= h * GLA_DK) & (lq < (h + 1) * GLA_DK), 1.0, 0.0) for h in range(GLA_HEADS)]
    hv = [jnp.where((lv >= h * GLA_DV) & (lv < (h + 1) * GLA_DV), 1.0, 0.0) for h in range(GLA_HEADS)]
    rexp, bdmask = rexp_ref[...], bd_ref[...]
    nsub = kqf_ref.shape[0] // GLA_SUBBLOCK
    scans = []
    for r in range(nsub):
        lo, hi = pl.ds(r * GLA_SUBBLOCK, GLA_SUBBLOCK), pl.ds((nsub - 1 - r) * GLA_SUBBLOCK, GLA_SUBBLOCK)
        scans.append(_gla_block(kqf_ref.at[lo], vf_ref.at[lo], laf_ref.at[lo], sf_ref, cum_ref.at[2 * r],
                                of_ref.at[lo], (trif_ref[...], rexp, bdmask, hq, hv), True))
        scans.append(_gla_block(kqb_ref.at[hi], vb_ref.at[hi], lab_ref.at[hi], sb_ref, cum_ref.at[2 * r + 1],
                                ob_ref.at[hi], (trib_ref[...], rexp, bdmask, hq, hv), False))
    live = True
    while live:
        live = False
        for scan in scans:
            live = next(scan, "done") != "done" or live


def _gla_consts(t):
    c = GLA_CHUNK
    i = np.arange(t)
    same = (i[None, :] // c) == (i[:, None] // c)
    trif = (same & (i[None, :] <= i[:, None])).astype(np.float32)
    trib = (same & (i[None, :] >= i[:, None])).astype(np.float32)
    hk = np.arange(GLA_QK) // GLA_DK
    hd = np.arange(W_GLA) // GLA_DV
    rexp = (hk[:, None] == hd[None, :]).astype(np.float32)
    bd = (hd[:, None] == hk[None, :]).astype(np.float32)
    return (jnp.asarray(trif, BF16), jnp.asarray(trib, BF16), jnp.asarray(rexp, BF16), jnp.asarray(bd, F32))


def _gla_call(kq, v, la, h0f, h0b, *, batch):
    n = kq.shape[0]
    tg = min(GLA_BLOCK, n // batch)
    nb = n // batch // tg
    trif, trib, rexp, bd = _gla_consts(GLA_SUBBLOCK)
    fw = lambda b, i: (b * nb + i, 0)
    bw = lambda b, i: (b * nb + nb - 1 - i, 0)
    bw1 = lambda b, i: (b * nb + nb - 1 - i, 1)
    st = lambda b, i: (b, 0, 0)
    sshape = jax.ShapeDtypeStruct((batch, W_GLA, GLA_QK), F32)
    return pl.pallas_call(
        _gla_kernel,
        grid=(batch, nb),
        in_specs=[pl.BlockSpec((tg, 2 * GLA_QK), fw), pl.BlockSpec((tg, W_GLA), fw), pl.BlockSpec((tg, GLA_QK), fw),
                  pl.BlockSpec((tg, 2 * GLA_QK), bw), pl.BlockSpec((tg, W_GLA), bw), pl.BlockSpec((tg, GLA_QK), bw1),
                  pl.BlockSpec((1, W_GLA, GLA_QK), st), pl.BlockSpec((1, W_GLA, GLA_QK), st),
                  _const_spec(trif.shape), _const_spec(trib.shape), _const_spec(rexp.shape), _const_spec(bd.shape)],
        out_specs=[pl.BlockSpec((tg, W_GLA), fw), pl.BlockSpec((tg, W_GLA), bw),
                   pl.BlockSpec((1, W_GLA, GLA_QK), st), pl.BlockSpec((1, W_GLA, GLA_QK), st)],
        out_shape=[jax.ShapeDtypeStruct((n, W_GLA), F32), jax.ShapeDtypeStruct((n, W_GLA), F32), sshape, sshape],
        scratch_shapes=[pltpu.VMEM((2 * tg // GLA_SUBBLOCK, GLA_SUBBLOCK, GLA_QK), F32)],
        compiler_params=_params("arbitrary", "arbitrary"),
        name="gla",
    )(kq, v, la, kq, v, la, h0f, h0b, trif, trib, rexp, bd)


def _fft_tail(ab, cs_ref, wf_ref, norm):
    f = jnp.dot(ab, cs_ref[...].astype(BF16), preferred_element_type=F32) * norm
    return jnp.dot(f.astype(BF16), wf_ref[...], preferred_element_type=F32)


def _fft_kernel(kron_ref, tw_ref, dft_ref, cs_ref, wf_ref, u_ref, o_ref, y_ref, rows_ref, *, norm):
    n1, n2, width = u_ref.shape
    halves = rows_ref.shape[0]
    g = FFT_GROUP
    kron = kron_ref[...].astype(BF16)

    def stage1(blk, carry):
        cols = pl.ds(pl.multiple_of(blk * g, g), g)
        ub = u_ref[:, cols, :].reshape(n1 * g, width).astype(BF16)
        y_ref[:, cols, :] = jnp.dot(kron, ub, preferred_element_type=F32).reshape(2 * n1, g, width)
        return carry

    lax.fori_loop(0, n2 // g, stage1, 0)
    c2, s2 = dft_ref[0], dft_ref[1]

    def stage2(grp, carry):
        parts = []
        for j in range(g):
            k1 = grp * g + j
            tc, ts = tw_ref[k1, 0:1, :], tw_ref[k1, 1:2, :]
            mc = c2 * tc - s2 * ts
            ms = s2 * tc + c2 * ts
            m = jnp.concatenate([jnp.concatenate([mc, ms], axis=1),
                                 jnp.concatenate([-ms, mc], axis=1)], axis=0).astype(BF16)
            ys = jnp.concatenate([y_ref[k1], y_ref[n1 + k1]], axis=0).astype(BF16)
            z = jnp.dot(m, ys, preferred_element_type=F32)
            parts.append(jnp.concatenate([z[:n2], z[n2:]], axis=1).astype(BF16))
        out = _fft_tail(jnp.concatenate(parts, axis=0), cs_ref, wf_ref, norm)
        for j in range(g):
            for h in range(halves):
                rows_ref[h, pl.ds(j, n2, stride=g), :] = out[j * n2:(j + 1) * n2, h * LANES:(h + 1) * LANES]
        cols = pl.ds(pl.multiple_of(grp * g, g), g)
        for h in range(halves):
            o_ref[:, cols, h * LANES:(h + 1) * LANES] = rows_ref[h].reshape(n2, g, LANES)
        return carry

    lax.fori_loop(0, n1 // g, stage2, 0)


def _fft_direct_kernel(m_ref, u_ref, cs_ref, wf_ref, o_ref, *, norm):
    n = u_ref.shape[0]
    z = jnp.dot(m_ref[...].astype(BF16), u_ref[...].astype(BF16), preferred_element_type=F32)
    o_ref[...] = _fft_tail(jnp.concatenate([z[:n], z[n:]], axis=1).astype(BF16), cs_ref, wf_ref, norm)


def _dft_cos_sin(n):
    k = np.arange(n, dtype=np.int64)
    ang = 2.0 * np.pi * ((k[:, None] * k[None, :]) % n).astype(np.float64) / n
    return np.cos(ang), np.sin(ang)


def _channel_dft():
    c, s = _dft_cos_sin(FFT_DG)
    eye = np.eye(FFT_GROUPS)
    return jnp.asarray(np.concatenate([np.kron(eye, c), np.kron(eye, s)], axis=0), F32)


def _fft_latent_call(u, wf_bd, layer, *, batch):
    n = u.shape[0] // batch
    n1, n2 = n // FFT_N2, FFT_N2
    halves = W_FFT // LANES
    g = FFT_GROUP
    assert n1 % g == 0 and n2 % g == 0
    c1, s1 = _dft_cos_sin(n1)
    kron =jnp.asarray(np.kron(np.concatenate([c1, -s1], axis=0), np.eye(g)), F32)
    tw_ang = 2.0 * np.pi * (np.arange(n1, dtype=np.int64)[:, None] * np.arange(n2, dtype=np.int64)[None, :]) / n
    tw = jnp.asarray(np.stack([np.cos(tw_ang), np.sin(tw_ang)], axis=1), F32)
    dft2 = jnp.asarray(np.stack(_dft_cos_sin(n2), axis=0), F32)
    cs = _channel_dft()
    norm = float(1.0 / np.sqrt(n * FFT_DG))
    out = pl.pallas_call(
        functools.partial(_fft_kernel, norm=norm),
        grid=(batch,),
        in_specs=[_const_spec(kron.shape), _const_spec(tw.shape), _const_spec(dft2.shape), _const_spec(cs.shape),
                  _layer_spec(wf_bd, layer),
                  pl.BlockSpec((None, n1, n2, W_FFT), lambda b: (b, 0, 0, 0), pipeline_mode=pl.Buffered(1))],
        out_specs=pl.BlockSpec((None, n2, n1, W_FFT), lambda b: (b, 0, 0, 0)),
        out_shape=jax.ShapeDtypeStruct((batch, n2, n1, W_FFT), F32),
        scratch_shapes=[pltpu.VMEM((2 * n1, n2, W_FFT), F32), pltpu.VMEM((halves, n2 * g, LANES), F32)],
        compiler_params=_params("arbitrary"),
        name="fft",
    )(kron, tw, dft2, cs, wf_bd, u.reshape(batch, n1, n2, W_FFT))
    return out.reshape(batch * n, W_FFT)


def _fft_direct_call(u, wf_bd, layer, *, batch):
    n = u.shape[0] // batch
    c, s = _dft_cos_sin(n)
    m = jnp.asarray(np.concatenate([c, -s], axis=0), F32)
    cs = _channel_dft()
    norm = float(1.0 / np.sqrt(n * FFT_DG))
    return pl.pallas_call(
        functools.partial(_fft_direct_kernel, norm=norm),
        grid=(batch,),
        in_specs=[_const_spec(m.shape), pl.BlockSpec((n, W_FFT), lambda b: (b, 0)),
                  _const_spec(cs.shape), _layer_spec(wf_bd, layer)],
        out_specs=pl.BlockSpec((n, W_FFT), lambda b: (b, 0)),
        out_shape=jax.ShapeDtypeStruct((batch * n, W_FFT), F32),
        compiler_params=_params("arbitrary"),
        name="fft_direct",
    )(m, u, cs, wf_bd)


def _load_cast(w_hbm, layer, dst_ref, stage_ref, sem_ref):
    nslots, step = stage_ref.shape[0], stage_ref.shape[1]
    nchunks = dst_ref.shape[0] // step

    def copy(k):
        return pltpu.make_async_copy(w_hbm.at[layer, pl.ds(k * step, step), :], stage_ref.at[k % nslots],
                                     sem_ref.at[k % nslots])

    for k in range(min(nslots - 1, nchunks)):
        copy(k).start()
    for k in range(nchunks):
        if k + nslots - 1 < nchunks:
            copy(k + nslots - 1).start()
        copy(k).wait()
        dst_ref[pl.ds(k * step, step), :] = stage_ref[k % nslots].astype(BF16)


def _out_kernel(x_ref, mod_ref, of_ref, ob_ref, gate_ref, fft_ref, ycp_ref, glag_ref, ones_ref, wout_hbm,
                n2g_ref, wup_hbm, cw_ref, cb_ref, wdn_hbm, fing_ref, o_ref,
                x1_ref, h2_ref, au_ref, h_ref, acc_ref, wout_ref, wup_ref, wdn_ref, stage_up_ref, stage_sq_ref,
                sem_up_ref, sem_sq_ref, *, row_len, final, layer):
    @pl.when(pl.program_id(0) == 0)
    def _():
        _load_cast(wout_hbm, layer, wout_ref, stage_sq_ref, sem_sq_ref)
        _load_cast(wup_hbm, layer, wup_ref, stage_up_ref, sem_up_ref)
        _load_cast(wdn_hbm, layer, wdn_ref, stage_sq_ref, sem_sq_ref)

    _out_head(x_ref, mod_ref, of_ref, ob_ref, gate_ref, fft_ref, ycp_ref, glag_ref, ones_ref, wout_ref,
              n2g_ref, x1_ref, h2_ref)
    _out_ffn(mod_ref, wup_ref, cw_ref, cb_ref, wdn_ref, fing_ref, o_ref, x1_ref, h2_ref, au_ref, h_ref,
             acc_ref, row_len, final)


def _out_head(x_ref, mod_ref, of_ref, ob_ref, gate_ref, fft_ref, ycp_ref, glag_ref, ones_ref, wout_ref, n2g_ref,
              x1_ref, h2_ref):
    for r0 in range(0, x_ref.shape[0], TOKEN_TILE):
        rows = slice(r0, r0 + TOKEN_TILE)
        o = of_ref[rows, :] + ob_ref[rows, :]
        osq = o * o
        hi = osq.astype(BF16)
        lo = (osq - hi.astype(F32)).astype(BF16)
        ms = (jnp.dot(hi, ones_ref[...], preferred_element_type=F32)
              + jnp.dot(lo, ones_ref[...], preferred_element_type=F32)) * (1.0 / GLA_DV)
        gl = (o * lax.rsqrt(ms + EPS)) * glag_ref[...] * _silu(gate_ref[rows, :])
        ymix = jnp.concatenate([gl.astype(BF16), fft_ref[rows, :].astype(BF16), ycp_ref[rows, :]], axis=1)
        x1 = x_ref[rows, :] + mod_ref[2:3, :] * jnp.dot(ymix, wout_ref[...], preferred_element_type=F32)
        x1_ref[rows, :] = x1
        h2_ref[rows, :] = _rms_mod(x1, n2g_ref[...], mod_ref[4:5, :], mod_ref[3:4, :]).astype(BF16)


def _out_ffn(mod_ref, wup_ref, cw_ref, cb_ref, wdn_ref, fing_ref, o_ref, x1_ref, h2_ref, au_ref, h_ref, acc_ref,
             row_len, final):
    t_rows = o_ref.shape[0]
    d_ff = wdn_ref.shape[0]
    nchunk = d_ff // FFN_CHUNK
    acc_ref[...] = jnp.zeros(acc_ref.shape, F32)

    def up(c, slot, row_step=None):
        row_step = row_step or t_rows
        for r0 in range(0, t_rows, row_step):
            rows = slice(r0, r0 + row_step)
            for half in range(2):
                cols = slice(half * d_ff + c * FFN_CHUNK, half * d_ff + (c + 1) * FFN_CHUNK)
                au_ref[slot, rows, half * FFN_CHUNK:(half + 1) * FFN_CHUNK] = jnp.dot(
                    h2_ref[rows, :], wup_ref[:, cols], preferred_element_type=F32)

    def elem(c, slot):
        cols = slice(c * FFN_CHUNK, (c + 1) * FFN_CHUNK)
        pos = lax.broadcasted_iota(jnp.int32, (t_rows, FFN_CHUNK), 0) & (row_len - 1)
        a = au_ref[slot, :, :FFN_CHUNK]
        cw = cw_ref[:, cols]
        a = (jnp.where(pos == 0, 0.0, pltpu.roll(a, 1, 0)) * cw[0:1] + a * cw[1:2]
             + jnp.where(pos == row_len - 1, 0.0, pltpu.roll(a, t_rows - 1, 0)) * cw[2:3] + cb_ref[:, cols])
        h_ref[slot] = (_silu(a) * au_ref[slot, :, FFN_CHUNK:]).astype(BF16)

    def down(c, slot):
        rows = slice(c * FFN_CHUNK, (c + 1) * FFN_CHUNK)
        acc_ref[...] += jnp.dot(h_ref[slot], wdn_ref[rows, :], preferred_element_type=F32)

    def stage(c):
        up(c + 1, 0)
        elem(c, 1)
        down(c - 1, 0)
        up(c + 2, 1)
        elem(c + 1, 0)
        down(c, 1)

    assert nchunk % 2 == 1 and nchunk >= 3
    up(0, 0, TOKEN_TILE)
    up(1, 1, TOKEN_TILE)
    elem(0, 0)

    for i in range((nchunk - 3) // 2):
        stage(2 * i + 1)
    c = nchunk - 2
    up(c + 1, 0)
    elem(c, 1)
    down(c - 1, 0)
    elem(c + 1, 0)
    down(c, 1)
    down(c + 1, 0)
    x2 = x1_ref[...] + mod_ref[5:6, :] * acc_ref[...]
    if final:
        ms2 = jnp.mean(x2 * x2, axis=-1, keepdims=True)
        x2 = (x2 * lax.rsqrt(ms2 + EPS)) * fing_ref[...]
    o_ref[...] = x2


def _out_call(xt, mod4, layer, mod_base, groups, o_f, o_b, gate, yfft, ycp, params, fing, *, row_len, final):
    n, d = xt.shape
    t = OUT_TILE
    tiles_per_group = n // groups // t
    assert tiles_per_group * t * groups == n
    hd = np.arange(W_GLA) // GLA_DV
    ones = jnp.asarray((hd[:, None] == hd[None, :]).astype(np.float32), BF16)
    row = lambda i: (i, 0)
    acts = [xt, o_f, o_b, gate, yfft, ycp]
    glag, wout, n2g, wup, cw, cb, wdn = params
    consts = [glag, ones, wout, n2g, wup, cw, cb, wdn, fing]

    def const_spec(a):
        if a is wout or a is wup or a is wdn:
            return pl.BlockSpec(memory_space=pl.ANY)
        return _const_spec(a.shape) if a is ones or a is fing else _layer_spec(a, layer)

    assert wout.shape[1] % WEIGHT_STAGE_ROWS == 0 and wdn.shape[1] % WEIGHT_STAGE_ROWS == 0
    assert wup.shape[1] % WEIGHT_STAGE_ROWS_WIDE == 0
    return pl.pallas_call(
        functools.partial(_out_kernel, row_len=row_len, final=final, layer=layer),
        grid=(n // t,),
        in_specs=[pl.BlockSpec((t, d), row), _mod_spec(mod4, layer, mod_base, tiles_per_group)]
                 + [pl.BlockSpec((t, a.shape[1]), row) for a in acts[1:]]
                 + [const_spec(a) for a in consts],
        out_specs=pl.BlockSpec((t, d), row),
        out_shape=jax.ShapeDtypeStruct((n, d), F32),
        scratch_shapes=[pltpu.VMEM((t, d), F32), pltpu.VMEM((t, d), BF16),
                        pltpu.VMEM((2, t, 2 * FFN_CHUNK), F32), pltpu.VMEM((2, t, FFN_CHUNK), BF16),
                        pltpu.VMEM((t, d), F32),
                        pltpu.VMEM(wout.shape[1:], BF16), pltpu.VMEM(wup.shape[1:], BF16),
                        pltpu.VMEM(wdn.shape[1:], BF16),
                        pltpu.VMEM((WEIGHT_STAGE_SLOTS, WEIGHT_STAGE_ROWS_WIDE, wup.shape[2]), F32),
                        pltpu.VMEM((WEIGHT_STAGE_SLOTS, WEIGHT_STAGE_ROWS, d), F32),
                        pltpu.SemaphoreType.DMA((WEIGHT_STAGE_SLOTS,)),
                        pltpu.SemaphoreType.DMA((WEIGHT_STAGE_SLOTS,))],
        compiler_params=_params("arbitrary"),
        name="out_ffn",
    )(xt, mod4, *acts[1:], *consts)


def _block_diag(w):
    dep, g, a, b = w.shape
    eye = jnp.eye(g, dtype=w.dtype)
    return (eye[None, :, None, :, None] * w[:, :, :, None, :]).reshape(dep, g * a, g * b)


def _w_in_layout_kernel(wt_ref, o_ref):
    a0, a1 = COL_STARTS[2], COL_STARTS[4]
    wt = wt_ref[...]
    pad = jnp.zeros((IN_COLS - wt.shape[0], wt.shape[1]), wt.dtype)
    o_ref[...] = jnp.concatenate([wt[:a0], wt[a1:], wt[a0:a1], pad], axis=0).T.astype(BF16)


def _w_in_layout_call(w_in):
    dep, d, cols = w_in.shape
    rows = TOKEN_TILE
    return pl.pallas_call(
        _w_in_layout_kernel,
        grid=(dep, d // rows),
        in_specs=[pl.BlockSpec((None, cols, rows), lambda i, j: (i, 0, j))],
        out_specs=pl.BlockSpec((None, rows, IN_COLS), lambda i, j: (i, j, 0)),
        out_shape=jax.ShapeDtypeStruct((dep, d, IN_COLS), BF16),
        compiler_params=_params("arbitrary", "arbitrary"),
        name="w_in_layout",
    )(jnp.swapaxes(w_in, 1, 2))


def _prep_params(norm1_g, norm2_g, w_in, gla_w_a2, gla_b_a2, gla_norm_g, fft_w, conv_w, conv_b, pool_w,
                 pool_scale, w_out, ffn_w_up, ffn_conv_w, ffn_conv_b, ffn_w_down):
    dep = w_in.shape[0]
    w_in_r = _w_in_layout_call(w_in)
    wa2 = jnp.zeros((dep, LANES, 2 * GLA_QK), F32)
    wa2 = wa2.at[:, 0:GLA_RANK, 0:GLA_QK].set(gla_w_a2[:, 0])
    wa2 = wa2.at[:, GLA_RANK:2 * GLA_RANK, GLA_QK:].set(gla_w_a2[:, 1])
    in_params = (norm1_g[:, None, :], w_in_r, wa2.astype(BF16), gla_b_a2.reshape(dep, 1, 2 * GLA_QK),
                 conv_w, conv_b[:, None, :], _block_diag(pool_w).astype(BF16), pool_scale[:, None, :])
    out_params = (jnp.tile(gla_norm_g, (1, GLA_HEADS))[:, None, :], w_out, norm2_g[:, None, :],
                  ffn_w_up, ffn_conv_w, ffn_conv_b[:, None, :], ffn_w_down)
    return in_params, out_params, _block_diag(fft_w).astype(BF16)


def kernel(x, c, ctx, c_ctx, norm1_g, norm2_g, w_mod, b_mod, w_in, gla_w_a2, gla_b_a2, gla_norm_g,
           fft_w, conv_w, conv_b, pool_w, pool_scale, w_out, ffn_w_up, ffn_conv_w, ffn_conv_b,
           ffn_w_down, final_norm_g):
    batch, seq, d = x.shape
    ctx_len = ctx.shape[1]
    depth = w_mod.shape[0]
    assert seq % GLA_BLOCK == 0 and seq % FFT_N2 == 0 and ctx_len == TOKEN_TILE

    cc = jnp.concatenate([c, c_ctx[None, :], jnp.zeros((MOD_ROWS - batch - 1, d), F32)], axis=0)
    mod4 = _mod_call(cc, w_mod, b_mod).reshape(depth, MOD_ROWS, 6, d)
    in_params, out_params, wf_bd = _prep_params(
        norm1_g, norm2_g, w_in, gla_w_a2, gla_b_a2, gla_norm_g, fft_w, conv_w, conv_b, pool_w, pool_scale,
        w_out, ffn_w_up, ffn_conv_w, ffn_conv_b, ffn_w_down)

    xt = x.reshape(batch * seq, d)
    xc = ctx.reshape(batch * ctx_len, d)
    zero_state = jnp.zeros((batch, W_GLA, GLA_QK), F32)
    fing = final_norm_g.reshape(1, d)
    for i in range(depth):
        last = i == depth - 1
        kq, v, la, gate, ufft, ycp = _in_call(xc, mod4, i, batch, 1, in_params, row_len=ctx_len)
        o_f, o_b, s_f, s_b = _gla_call(kq, v, la, zero_state, zero_state, batch=batch)
        if not last:
            yfft = _fft_direct_call(ufft, wf_bd, i, batch=batch)
            xc = _out_call(xc, mod4, i, batch, 1, o_f, o_b, gate, yfft, ycp, out_params, fing,
                           row_len=ctx_len, final=False)

        kq, v, la, gate, ufft, ycp = _in_call(xt, mod4, i, 0, batch, in_params, row_len=GRID_W)
        o_f, o_b, _, _ = _gla_call(kq, v, la, s_f, s_b, batch=batch)
        yfft = _fft_latent_call(ufft, wf_bd, i, batch=batch)
        xt = _out_call(xt, mod4, i, 0, batch, o_f, o_b, gate, yfft, ycp, out_params, fing,
                       row_len=GRID_W, final=last)
    return xt.reshape(batch, seq, d)
```

```python
import functools

import numpy as np
import jax
import jax.numpy as jnp
from jax import lax
from jax.experimental import pallas as pl
from jax.experimental.pallas import tpu as pltpu

F32 = jnp.float32
BF16 = jnp.bfloat16

GRID_W = 64
EPS = 1e-6
GLA_HEADS = 4
GLA_DK = 32
GLA_DV = 64
GLA_QK = GLA_HEADS * GLA_DK
W_GLA = GLA_HEADS * GLA_DV
GLA_RANK = 16
GLA_TAU = 16.0
FFT_GROUPS = 4
FFT_DG = 64
W_FFT = FFT_GROUPS * FFT_DG
W_CONV = 256
POOL_WINDOWS = (2, 4, 8, 16)
POOL_DG = 64
W_POOL = len(POOL_WINDOWS) * POOL_DG
COL_SIZES = (GLA_QK, W_GLA, GLA_RANK, GLA_RANK, GLA_QK, W_GLA, W_FFT, W_CONV, W_CONV, W_CONV, W_POOL)
COL_STARTS = tuple(int(s) for s in np.cumsum((0,) + COL_SIZES)[:-1])

LANES = 128
VMEM_LIMIT_BYTES = 56 * 1024 * 1024

TOKEN_TILE = 256
IN_TILE = 1024
OUT_TILE = 512
GLA_CHUNK = 64
GLA_SUB = 8
GLA_SUBBLOCK = 256
GLA_BLOCK = 1024
FFN_CHUNK = 256
WEIGHT_STAGE_ROWS = 128
WEIGHT_STAGE_ROWS_WIDE = 64
WEIGHT_STAGE_SLOTS = 4
FFT_N2 = 128
FFT_GROUP = 8
NEG_BIG = -1e30
LOG2_E = 1.4426950408889634

IN_K, IN_V, IN_Q, IN_G, IN_FFT, IN_H, IN_BG, IN_CG, IN_POOL, IN_A = (
    0, 128, 384, 512, 768, 1024, 1280, 1536, 1792, 2048)
IN_COLS = IN_A + LANES
MOD_COL_TILE = 1536
MOD_ROWS = 8


def _const_spec(shape):
    nd = len(shape)
    return pl.BlockSpec(shape, lambda *_: (0,) * nd, pipeline_mode=pl.Buffered(1))


def _layer_spec(arr, layer):
    nd = arr.ndim
    return pl.BlockSpec((None,) + arr.shape[1:], lambda *_: (layer,) + (0,) * (nd - 1),
                        pipeline_mode=pl.Buffered(1))


def _mod_spec(mod4, layer, base, tiles_per_group):
    return pl.BlockSpec((None, None) + mod4.shape[2:], lambda i: (layer, base + i // tiles_per_group, 0, 0))


def _params(*sem):
    return pltpu.CompilerParams(dimension_semantics=sem, vmem_limit_bytes=VMEM_LIMIT_BYTES)


def _rms_mod(x, g, scale, shift):
    ms = jnp.mean(x * x, axis=-1, keepdims=True)
    return (x * lax.rsqrt(ms + EPS)) * g * (1.0 + scale) + shift


def _silu(a):
    return a * jax.nn.sigmoid(a)


def _mod_kernel(c_ref, w_ref, b_ref, o_ref):
    s = _silu(c_ref[...]).astype(BF16)
    o_ref[0] = jnp.dot(s, w_ref[0].astype(BF16), preferred_element_type=F32) + b_ref[0]


def _mod_call(cc, w_mod, b_mod):
    depth, d, n = w_mod.shape
    tn = MOD_COL_TILE
    return pl.pallas_call(
        _mod_kernel,
        grid=(depth, n // tn),
        in_specs=[pl.BlockSpec(cc.shape, lambda i, j: (0, 0)),
                  pl.BlockSpec((1, d, tn), lambda i, j: (i, 0, j)),
                  pl.BlockSpec((1, 1, tn), lambda i, j: (i, 0, j))],
        out_specs=pl.BlockSpec((1, cc.shape[0], tn), lambda i, j: (i, 0, j)),
        out_shape=jax.ShapeDtypeStruct((depth, cc.shape[0], n), F32),
        compiler_params=_params("arbitrary", "arbitrary"),
        name="modulation",
    )(cc, w_mod, b_mod.reshape(depth, 1, n))


def _in_kernel(x_ref, mod_ref, g_ref, w_ref, wa2_ref, ba2_ref, cw_ref, cb_ref, cnt_ref, wpool_ref, pscale_ref,
               kq_ref, v_ref, la_ref, gate_ref, fft_ref, ycp_ref, *, row_len):
    nsub = x_ref.shape[0] // TOKEN_TILE

    def project(s):
        rows = slice(s * TOKEN_TILE, (s + 1) * TOKEN_TILE)
        h = _rms_mod(x_ref[rows, :], g_ref[...], mod_ref[1:2, :], mod_ref[0:1, :]).astype(BF16)
        return jnp.dot(h, w_ref[...], preferred_element_type=F32)

    p_next = project(0)
    for s in range(nsub):
        p, p_next = p_next, (project(s + 1) if s + 1 < nsub else None)
        _in_mixers(p, slice(s * TOKEN_TILE, (s + 1) * TOKEN_TILE), wa2_ref, ba2_ref, cw_ref, cb_ref, cnt_ref,
                   wpool_ref, pscale_ref, kq_ref, v_ref, la_ref, gate_ref, fft_ref, ycp_ref, row_len)


def _in_mixers(p, rows, wa2_ref, ba2_ref, cw_ref, cb_ref, cnt_ref, wpool_ref, pscale_ref,
               kq_ref, v_ref, la_ref, gate_ref, fft_ref, ycp_ref, row_len):
    t_rows = p.shape[0]
    kq_ref[rows, :GLA_QK] = p[:, IN_K:IN_K + GLA_QK]
    kq_ref[rows, GLA_QK:] = p[:, IN_Q:IN_Q + GLA_QK] * (GLA_DK ** -0.5)
    v_ref[rows, :] = p[:, IN_V:IN_V + W_GLA]
    gate_ref[rows, :] = p[:, IN_G:IN_G + W_GLA]
    fft_ref[rows, :] = p[:, IN_FFT:IN_FFT + W_FFT]
    z = jnp.dot(p[:, IN_A:IN_A + LANES].astype(BF16), wa2_ref[...], preferred_element_type=F32) + ba2_ref[...]
    la_ref[rows, :] = (jnp.minimum(z, 0.0) - jnp.log(1.0 + jnp.exp(-jnp.abs(z)))) * (1.0 / GLA_TAU)

    def pos(a):
        return lax.broadcasted_iota(jnp.int32, a.shape, 0) & (row_len - 1)

    def prev(a, s):
        return jnp.where(pos(a) >= s, pltpu.roll(a, s, 0), 0.0)

    def nxt(a, s):
        return jnp.where(pos(a) < row_len - s, pltpu.roll(a, t_rows - s, 0), 0.0)

    t = p[:, IN_CG:IN_CG + W_CONV] * p[:, IN_H:IN_H + W_CONV]
    cw = cw_ref[...]
    conv = prev(t, 1) * cw[0:1] + t * cw[1:2] + nxt(t, 1) * cw[2:3] + cb_ref[...]
    ycp_ref[rows, :W_CONV] = (p[:, IN_BG:IN_BG + W_CONV] * conv).astype(BF16)

    u = p[:, IN_POOL:IN_POOL + W_POOL]
    halves = []
    for side, steps in ((0, 1), (1, 3)):
        f = u[:, side * LANES:(side + 1) * LANES]
        g = prev(f, 1)
        sums = [g + f]
        for i in range(steps):
            f = f + nxt(f, 1 << i)
            g = g + prev(g, 1 << i)
            sums.append(g + f)
        lane = lax.broadcasted_iota(jnp.int32, f.shape, 1)
        halves.append(jnp.where(lane < POOL_DG, sums[-2], sums[-1]))
    tot = jnp.concatenate(halves, axis=1)
    pooled = tot / cnt_ref[...] - u
    yp = jnp.dot(pooled.astype(BF16), wpool_ref[...], preferred_element_type=F32) * pscale_ref[...]
    ycp_ref[rows, W_CONV:] = yp.astype(BF16)


def _pool_counts(t_rows, row_len):
    pos = np.arange(t_rows) % row_len
    cols = []
    for w in POOL_WINDOWS:
        lo = np.clip(pos - w // 2, 0, row_len - 1)
        hi = np.clip(pos + w // 2 - 1, 0, row_len - 1)
        cols.append(np.repeat((hi - lo + 1).astype(np.float32)[:, None], POOL_DG, axis=1))
    return np.concatenate(cols, axis=1)


def _in_call(xt, mod4, layer, mod_base, groups, params, *, row_len):
    n, d = xt.shape
    t = min(IN_TILE, n // groups)
    tiles_per_group = n // groups // t
    assert tiles_per_group * t * groups == n
    cnt = jnp.asarray(_pool_counts(TOKEN_TILE, row_len))
    row = lambda i: (i, 0)
    outs = [jax.ShapeDtypeStruct((n, 2 * GLA_QK), F32), jax.ShapeDtypeStruct((n, W_GLA), F32),
            jax.ShapeDtypeStruct((n, 2 * GLA_QK), F32), jax.ShapeDtypeStruct((n, W_GLA), F32),
            jax.ShapeDtypeStruct((n, W_FFT), F32), jax.ShapeDtypeStruct((n, W_CONV + W_POOL), BF16)]
    g, w, wa2, ba2, cw, cb, wpool, pscale = params
    return pl.pallas_call(
        functools.partial(_in_kernel, row_len=row_len),
        grid=(n // t,),
        in_specs=[pl.BlockSpec((t, d), row), _mod_spec(mod4, layer, mod_base, tiles_per_group),
                  _layer_spec(g, layer), _layer_spec(w, layer), _layer_spec(wa2, layer), _layer_spec(ba2, layer),
                  _layer_spec(cw, layer), _layer_spec(cb, layer), _const_spec(cnt.shape),
                  _layer_spec(wpool, layer), _layer_spec(pscale, layer)],
        out_specs=[pl.BlockSpec((t, o.shape[1]), row) for o in outs],
        out_shape=outs,
        compiler_params=_params("arbitrary"),
        name="in_proj",
    )(xt, mod4, g, w, wa2, ba2, cw, cb, cnt, wpool, pscale)


def _gla_block(kq_ref, v_ref, la_ref, s_ref, cum_ref, o_ref, consts, fwd):
    tri2, rexp, bdmask, hq, hv = consts
    k, q, v, la = kq_ref[:, :GLA_QK], kq_ref[:, GLA_QK:], v_ref[...], la_ref[...]
    t_rows = q.shape[0]
    c, sub = GLA_CHUNK, GLA_SUB
    nchunk = t_rows // c
    nt = (((1,), (1,)), ((), ()))
    tn = (((0,), (0,)), ((), ()))
    la_hi = la.astype(BF16)
    la_lo = (la - la_hi.astype(F32)).astype(BF16)
    cums = jnp.dot(tri2, jnp.concatenate([la_hi, la_lo], axis=1), preferred_element_type=F32)
    cum = (cums[:, :GLA_QK] + cums[:, GLA_QK:]) * LOG2_E
    cum_ref[...] = cum
    end = jnp.concatenate([jnp.broadcast_to(cum_ref[pl.ds(ci * c + (c - 1 if fwd else 0), 1), :], (c, GLA_QK))
                           for ci in range(nchunk)], axis=0)
    yield

    nblk = t_rows // sub

    def row_of_block(ref, width, j):
        cols = [jnp.concatenate([jnp.broadcast_to(ref[pl.ds(sub * blk + j, 1), c0:c0 + LANES], (sub, LANES))
                                 for blk in range(nblk)], axis=0) for c0 in range(0, width, LANES)]
        return cols[0] if len(cols) == 1 else jnp.concatenate(cols, axis=1)

    ii = lax.broadcasted_iota(jnp.int32, (t_rows, GLA_QK), 0) & (sub - 1)
    es = []
    for j in range(sub):
        keep = (ii >= j) if fwd else (ii <= j)
        w = jnp.exp2(jnp.where(keep, cum - row_of_block(cum_ref, GLA_QK, j), NEG_BIG))
        es.append((w * (q * row_of_block(kq_ref, GLA_QK, j))).astype(BF16))
    pr = jnp.dot(jnp.concatenate(es, axis=0), rexp, preferred_element_type=F32)
    yield

    qe = (q * jnp.exp2(cum)).astype(BF16)
    kd = (k * jnp.exp2(end - cum)).astype(BF16)
    v16 = v.astype(BF16)
    s = s_ref[0]
    o_parts = [None] * nchunk
    for ci in (range(nchunk) if fwd else reversed(range(nchunk))):
        rows = slice(ci * c, (ci + 1) * c)
        o_parts[ci] = lax.dot_general(qe[rows], s.astype(BF16), nt, preferred_element_type=F32)
        upd = lax.dot_general(v16[rows], kd[rows], tn, preferred_element_type=F32)
        s = s * jnp.exp2(end[ci * c:ci * c + 1]) + upd * bdmask
    s_ref[0] = s
    o = jnp.concatenate(o_parts, axis=0)

    half = t_rows // 2
    b = c // 2
    while b >= sub:
        npair = t_rows // (2 * b)
        refs = []
        for p in range(npair):
            r = 2 * b * p + (b - 1 if fwd else b)
            refs.append(jnp.broadcast_to(cum_ref[pl.ds(r, 1), :], (2 * b, GLA_QK)))
        e = jnp.exp2(-jnp.abs(cum - jnp.concatenate(refs, axis=0)))
        qfull, kfull = q * e, k * e
        first = [slice(2 * b * p, 2 * b * p + b) for p in range(npair)]
        second = [slice(2 * b * p + b, 2 * b * p + 2 * b) for p in range(npair)]
        qrows, krows = (second, first) if fwd else (first, second)
        qsel = jnp.concatenate([qfull[r] for r in qrows], axis=0)
        ksel = jnp.concatenate([kfull[r] for r in krows], axis=0).astype(BF16)
        vsel = jnp.concatenate([v[r] for r in krows], axis=0).astype(BF16)
        qst = jnp.concatenate([qsel * hq[h] for h in range(GLA_HEADS)], axis=0).astype(BF16)
        att = lax.dot_general(qst, ksel, nt, preferred_element_type=F32)
        ri = lax.broadcasted_iota(jnp.int32, att.shape, 0)
        ci = lax.broadcasted_iota(jnp.int32, att.shape, 1)
        att = jnp.where(((ri & (half - 1)) ^ ci) < b, att, 0.0).astype(BF16)
        att_k = jnp.concatenate([att[h * half:(h + 1) * half] for h in range(GLA_HEADS)], axis=1)
        v_k = jnp.concatenate([vsel * hv[h].astype(BF16) for h in range(GLA_HEADS)], axis=0)
        res = jnp.dot(att_k, v_k, preferred_element_type=F32)
        zero = jnp.zeros((b, W_GLA), F32)
        pieces = []
        for p in range(npair):
            piece = res[p * b:(p + 1) * b]
            pieces += [zero, piece] if fwd else [piece, zero]
        o = o + jnp.concatenate(pieces, axis=0)
        b //= 2
    yield

    for j in range(sub):
        o = o + pr[t_rows * j:t_rows * (j + 1)] * row_of_block(v_ref, W_GLA, j)
    o_ref[...] = o


def _gla_kernel(kqf_ref, vf_ref, laf_ref, kqb_ref, vb_ref, lab_ref, h0f_ref, h0b_ref,
                trif_ref, trib_ref, rexp_ref, bd_ref,
                of_ref, ob_ref, sf_ref, sb_ref, cum_ref):
    @pl.when(pl.program_id(1) == 0)
    def _():
        sf_ref[...] = h0f_ref[...]
        sb_ref[...] = h0b_ref[...]

    lq = lax.broadcasted_iota(jnp.int32, (1, GLA_QK), 1)
    lv = lax.broadcasted_iota(jnp.int32, (1, W_GLA), 1)
    hq = [jnp.where((lq >= h * GLA_DK) & (lq < (h + 1) * GLA_DK), 1.0, 0.0) for h in range(GLA_HEADS)]
    hv = [jnp.where((lv >= h * GLA_DV) & (lv < (h + 1) * GLA_DV), 1.0, 0.0) for h in range(GLA_HEADS)]
    rexp, bdmask = rexp_ref[...], bd_ref[...]
    nsub = kqf_ref.shape[0] // GLA_SUBBLOCK
    scans = []
    for r in range(nsub):
        lo, hi = pl.ds(r * GLA_SUBBLOCK, GLA_SUBBLOCK), pl.ds((nsub - 1 - r) * GLA_SUBBLOCK, GLA_SUBBLOCK)
        scans.append(_gla_block(kqf_ref.at[lo], vf_ref.at[lo], laf_ref.at[lo], sf_ref, cum_ref.at[2 * r],
                                of_ref.at[lo], (trif_ref[...], rexp, bdmask, hq, hv), True))
        scans.append(_gla_block(kqb_ref.at[hi], vb_ref.at[hi], lab_ref.at[hi], sb_ref, cum_ref.at[2 * r + 1],
                                ob_ref.at[hi], (trib_ref[...], rexp, bdmask, hq, hv), False))
    live = True
    while live:
        live = False
        for scan in scans:
            live = next(scan, "done") != "done" or live


def _gla_consts(t):
    c = GLA_CHUNK
    i = np.arange(t)
    same = (i[None, :] // c) == (i[:, None] // c)
    trif = (same & (i[None, :] <= i[:, None])).astype(np.float32)
    trib = (same & (i[None, :] >= i[:, None])).astype(np.float32)
    hk = np.arange(GLA_QK) // GLA_DK
    hd = np.arange(W_GLA) // GLA_DV
    rexp = (hk[:, None] == hd[None, :]).astype(np.float32)
    bd = (hd[:, None] == hk[None, :]).astype(np.float32)
    return (jnp.asarray(trif, BF16), jnp.asarray(trib, BF16), jnp.asarray(rexp, BF16), jnp.asarray(bd, F32))


def _gla_call(kq, v, la, h0f, h0b, *, batch):
    n = kq.shape[0]
    tg = min(GLA_BLOCK, n // batch)
    nb = n // batch // tg
    trif, trib, rexp, bd = _gla_consts(GLA_SUBBLOCK)
    fw = lambda b, i: (b * nb + i, 0)
    bw = lambda b, i: (b * nb + nb - 1 - i, 0)
    bw1 = lambda b, i: (b * nb + nb - 1 - i, 1)
    st = lambda b, i: (b, 0, 0)
    sshape = jax.ShapeDtypeStruct((batch, W_GLA, GLA_QK), F32)
    return pl.pallas_call(
        _gla_kernel,
        grid=(batch, nb),
        in_specs=[pl.BlockSpec((tg, 2 * GLA_QK), fw), pl.BlockSpec((tg, W_GLA), fw), pl.BlockSpec((tg, GLA_QK), fw),
                  pl.BlockSpec((tg, 2 * GLA_QK), bw), pl.BlockSpec((tg, W_GLA), bw), pl.BlockSpec((tg, GLA_QK), bw1),
                  pl.BlockSpec((1, W_GLA, GLA_QK), st), pl.BlockSpec((1, W_GLA, GLA_QK), st),
                  _const_spec(trif.shape), _const_spec(trib.shape), _const_spec(rexp.shape), _const_spec(bd.shape)],
        out_specs=[pl.BlockSpec((tg, W_GLA), fw), pl.BlockSpec((tg, W_GLA), bw),
                   pl.BlockSpec((1, W_GLA, GLA_QK), st), pl.BlockSpec((1, W_GLA, GLA_QK), st)],
        out_shape=[jax.ShapeDtypeStruct((n, W_GLA), F32), jax.ShapeDtypeStruct((n, W_GLA), F32), sshape, sshape],
        scratch_shapes=[pltpu.VMEM((2 * tg // GLA_SUBBLOCK, GLA_SUBBLOCK, GLA_QK), F32)],
        compiler_params=_params("arbitrary", "arbitrary"),
        name="gla",
    )(kq, v, la, kq, v, la, h0f, h0b, trif, trib, rexp, bd)


def _fft_tail(ab, cs_ref, wf_ref, norm):
    f = jnp.dot(ab, cs_ref[...].astype(BF16), preferred_element_type=F32) * norm
    return jnp.dot(f.astype(BF16), wf_ref[...], preferred_element_type=F32)


def _fft_kernel(kron_ref, tw_ref, dft_ref, cs_ref, wf_ref, u_ref, o_ref, y_ref, rows_ref, *, norm):
    n1, n2, width = u_ref.shape
    halves = rows_ref.shape[0]
    g = FFT_GROUP
    kron = kron_ref[...].astype(BF16)

    def stage1(blk, carry):
        cols = pl.ds(pl.multiple_of(blk * g, g), g)
        ub = u_ref[:, cols, :].reshape(n1 * g, width).astype(BF16)
        y_ref[:, cols, :] = jnp.dot(kron, ub, preferred_element_type=F32).reshape(2 * n1, g, width)
        return carry

    lax.fori_loop(0, n2 // g, stage1, 0)
    c2, s2 = dft_ref[0], dft_ref[1]

    def stage2(grp, carry):
        parts = []
        for j in range(g):
            k1 = grp * g + j
            tc, ts = tw_ref[k1, 0:1, :], tw_ref[k1, 1:2, :]
            mc = c2 * tc - s2 * ts
            ms = s2 * tc + c2 * ts
            m = jnp.concatenate([jnp.concatenate([mc, ms], axis=1),
                                 jnp.concatenate([-ms, mc], axis=1)], axis=0).astype(BF16)
            ys = jnp.concatenate([y_ref[k1], y_ref[n1 + k1]], axis=0).astype(BF16)
            z = jnp.dot(m, ys, preferred_element_type=F32)
            parts.append(jnp.concatenate([z[:n2], z[n2:]], axis=1).astype(BF16))
        out = _fft_tail(jnp.concatenate(parts, axis=0), cs_ref, wf_ref, norm)
        for j in range(g):
            for h in range(halves):
                rows_ref[h, pl.ds(j, n2, stride=g), :] = out[j * n2:(j + 1) * n2, h * LANES:(h + 1) * LANES]
        cols = pl.ds(pl.multiple_of(grp * g, g), g)
        for h in range(halves):
            o_ref[:, cols, h * LANES:(h + 1) * LANES] = rows_ref[h].reshape(n2, g, LANES)
        return carry

    lax.fori_loop(0, n1 // g, stage2, 0)


def _fft_direct_kernel(m_ref, u_ref, cs_ref, wf_ref, o_ref, *, norm):
    n = u_ref.shape[0]
    z = jnp.dot(m_ref[...].astype(BF16), u_ref[...].astype(BF16), preferred_element_type=F32)
    o_ref[...] = _fft_tail(jnp.concatenate([z[:n], z[n:]], axis=1).astype(BF16), cs_ref, wf_ref, norm)


def _dft_cos_sin(n):
    k = np.arange(n, dtype=np.int64)
    ang = 2.0 * np.pi * ((k[:, None] * k[None, :]) % n).astype(np.float64) / n
    return np.cos(ang), np.sin(ang)


def _channel_dft():
    c, s = _dft_cos_sin(FFT_DG)
    eye = np.eye(FFT_GROUPS)
    return jnp.asarray(np.concatenate([np.kron(eye, c), np.kron(eye, s)], axis=0), F32)


def _fft_latent_call(u, wf_bd, layer, *, batch):
    n = u.shape[0] // batch
    n1, n2 = n // FFT_N2, FFT_N2
    halves = W_FFT // LANES
    g = FFT_GROUP
    assert n1 % g == 0 and n2 % g == 0
    c1, s1 = _dft_cos_sin(n1)
    kron =jnp.asarray(np.kron(np.concatenate([c1, -s1], axis=0), np.eye(g)), F32)
    tw_ang = 2.0 * np.pi * (np.arange(n1, dtype=np.int64)[:, None] * np.arange(n2, dtype=np.int64)[None, :]) / n
    tw = jnp.asarray(np.stack([np.cos(tw_ang), np.sin(tw_ang)], axis=1), F32)
    dft2 = jnp.asarray(np.stack(_dft_cos_sin(n2), axis=0), F32)
    cs = _channel_dft()
    norm = float(1.0 / np.sqrt(n * FFT_DG))
    out = pl.pallas_call(
        functools.partial(_fft_kernel, norm=norm),
        grid=(batch,),
        in_specs=[_const_spec(kron.shape), _const_spec(tw.shape), _const_spec(dft2.shape), _const_spec(cs.shape),
                  _layer_spec(wf_bd, layer),
                  pl.BlockSpec((None, n1, n2, W_FFT), lambda b: (b, 0, 0, 0))],
        out_specs=pl.BlockSpec((None, n2, n1, W_FFT), lambda b: (b, 0, 0, 0)),
        out_shape=jax.ShapeDtypeStruct((batch, n2, n1, W_FFT), F32),
        scratch_shapes=[pltpu.VMEM((2 * n1, n2, W_FFT), F32), pltpu.VMEM((halves, n2 * g, LANES), F32)],
        compiler_params=_params("arbitrary"),
        name="fft",
    )(kron, tw, dft2, cs, wf_bd, u.reshape(batch, n1, n2, W_FFT))
    return out.reshape(batch * n, W_FFT)


def _fft_direct_call(u, wf_bd, layer, *, batch):
    n = u.shape[0] // batch
    c, s = _dft_cos_sin(n)
    m = jnp.asarray(np.concatenate([c, -s], axis=0), F32)
    cs = _channel_dft()
    norm = float(1.0 / np.sqrt(n * FFT_DG))
    return pl.pallas_call(
        functools.partial(_fft_direct_kernel, norm=norm),
        grid=(batch,),
        in_specs=[_const_spec(m.shape), pl.BlockSpec((n, W_FFT), lambda b: (b, 0)),
                  _const_spec(cs.shape), _layer_spec(wf_bd, layer)],
        out_specs=pl.BlockSpec((n, W_FFT), lambda b: (b, 0)),
        out_shape=jax.ShapeDtypeStruct((batch * n, W_FFT), F32),
        compiler_params=_params("arbitrary"),
        name="fft_direct",
    )(m, u, cs, wf_bd)


def _load_cast(w_hbm, layer, dst_ref, stage_ref, sem_ref):
    nslots, step = stage_ref.shape[0], stage_ref.shape[1]
    nchunks = dst_ref.shape[0] // step

    def copy(k):
        return pltpu.make_async_copy(w_hbm.at[layer, pl.ds(k * step, step), :], stage_ref.at[k % nslots],
                                     sem_ref.at[k % nslots])

    for k in range(min(nslots - 1, nchunks)):
        copy(k).start()
    for k in range(nchunks):
        if k + nslots - 1 < nchunks:
            copy(k + nslots - 1).start()
        copy(k).wait()
        dst_ref[pl.ds(k * step, step), :] = stage_ref[k % nslots].astype(BF16)


def _out_kernel(x_ref, mod_ref, of_ref, ob_ref, gate_ref, fft_ref, ycp_ref, glag_ref, ones_ref, wout_hbm,
                n2g_ref, wup_hbm, cw_ref, cb_ref, wdn_hbm, fing_ref, o_ref,
                x1_ref, h2_ref, au_ref, h_ref, acc_ref, wout_ref, wup_ref, wdn_ref, stage_up_ref, stage_sq_ref,
                sem_up_ref, sem_sq_ref, *, row_len, final, layer):
    @pl.when(pl.program_id(0) == 0)
    def _():
        _load_cast(wout_hbm, layer, wout_ref, stage_sq_ref, sem_sq_ref)
        _load_cast(wup_hbm, layer, wup_ref, stage_up_ref, sem_up_ref)
        _load_cast(wdn_hbm, layer, wdn_ref, stage_sq_ref, sem_sq_ref)

    _out_head(x_ref, mod_ref, of_ref, ob_ref, gate_ref, fft_ref, ycp_ref, glag_ref, ones_ref, wout_ref,
              n2g_ref, x1_ref, h2_ref)
    _out_ffn(mod_ref, wup_ref, cw_ref, cb_ref, wdn_ref, fing_ref, o_ref, x1_ref, h2_ref, au_ref, h_ref,
             acc_ref, row_len, final)


def _out_head(x_ref, mod_ref, of_ref, ob_ref, gate_ref, fft_ref, ycp_ref, glag_ref, ones_ref, wout_ref, n2g_ref,
              x1_ref, h2_ref):
    for r0 in range(0, x_ref.shape[0], TOKEN_TILE):
        rows = slice(r0, r0 + TOKEN_TILE)
        o = of_ref[rows, :] + ob_ref[rows, :]
        osq = o * o
        hi = osq.astype(BF16)
        lo = (osq - hi.astype(F32)).astype(BF16)
        ms = (jnp.dot(hi, ones_ref[...], preferred_element_type=F32)
              + jnp.dot(lo, ones_ref[...], preferred_element_type=F32)) * (1.0 / GLA_DV)
        gl = (o * lax.rsqrt(ms + EPS)) * glag_ref[...] * _silu(gate_ref[rows, :])
        ymix = jnp.concatenate([gl.astype(BF16), fft_ref[rows, :].astype(BF16), ycp_ref[rows, :]], axis=1)
        x1 = x_ref[rows, :] + mod_ref[2:3, :] * jnp.dot(ymix, wout_ref[...], preferred_element_type=F32)
        x1_ref[rows, :] = x1
        h2_ref[rows, :] = _rms_mod(x1, n2g_ref[...], mod_ref[4:5, :], mod_ref[3:4, :]).astype(BF16)


def _out_ffn(mod_ref, wup_ref, cw_ref, cb_ref, wdn_ref, fing_ref, o_ref, x1_ref, h2_ref, au_ref, h_ref, acc_ref,
             row_len, final):
    t_rows = o_ref.shape[0]
    d_ff = wdn_ref.shape[0]
    nchunk = d_ff // FFN_CHUNK
    acc_ref[...] = jnp.zeros(acc_ref.shape, F32)

    def up(c, slot, row_step=None):
        row_step = row_step or t_rows
        for r0 in range(0, t_rows, row_step):
            rows = slice(r0, r0 + row_step)
            for half in range(2):
                cols = slice(half * d_ff + c * FFN_CHUNK, half * d_ff + (c + 1) * FFN_CHUNK)
                au_ref[slot, rows, half * FFN_CHUNK:(half + 1) * FFN_CHUNK] = jnp.dot(
                    h2_ref[rows, :], wup_ref[:, cols], preferred_element_type=F32)

    def elem(c, slot):
        cols = slice(c * FFN_CHUNK, (c + 1) * FFN_CHUNK)
        pos = lax.broadcasted_iota(jnp.int32, (t_rows, FFN_CHUNK), 0) & (row_len - 1)
        a = au_ref[slot, :, :FFN_CHUNK]
        cw = cw_ref[:, cols]
        a = (jnp.where(pos == 0, 0.0, pltpu.roll(a, 1, 0)) * cw[0:1] + a * cw[1:2]
             + jnp.where(pos == row_len - 1, 0.0, pltpu.roll(a, t_rows - 1, 0)) * cw[2:3] + cb_ref[:, cols])
        h_ref[slot] = (_silu(a) * au_ref[slot, :, FFN_CHUNK:]).astype(BF16)

    def down(c, slot):
        rows = slice(c * FFN_CHUNK, (c + 1) * FFN_CHUNK)
        acc_ref[...] += jnp.dot(h_ref[slot], wdn_ref[rows, :], preferred_element_type=F32)

    def stage(c):
        up(c + 1, 0)
        elem(c, 1)
        down(c - 1, 0)
        up(c + 2, 1)
        elem(c + 1, 0)
        down(c, 1)

    assert nchunk % 2 == 1 and nchunk >= 3
    up(0, 0, TOKEN_TILE)
    up(1, 1, TOKEN_TILE)
    elem(0, 0)

    for i in range((nchunk - 3) // 2):
        stage(2 * i + 1)
    c = nchunk - 2
    up(c + 1, 0)
    elem(c, 1)
    down(c - 1, 0)
    elem(c + 1, 0)
    down(c, 1)
    down(c + 1, 0)
    x2 = x1_ref[...] + mod_ref[5:6, :] * acc_ref[...]
    if final:
        ms2 = jnp.mean(x2 * x2, axis=-1, keepdims=True)
        x2 = (x2 * lax.rsqrt(ms2 + EPS)) * fing_ref[...]
    o_ref[...] = x2


def _out_call(xt, mod4, layer, mod_base, groups, o_f, o_b, gate, yfft, ycp, params, fing, *, row_len, final):
    n, d = xt.shape
    t = OUT_TILE
    tiles_per_group = n // groups // t
    assert tiles_per_group * t * groups == n
    hd = np.arange(W_GLA) // GLA_DV
    ones = jnp.asarray((hd[:, None] == hd[None, :]).astype(np.float32), BF16)
    row = lambda i: (i, 0)
    acts = [xt, o_f, o_b, gate, yfft, ycp]
    glag, wout, n2g, wup, cw, cb, wdn = params
    consts = [glag, ones, wout, n2g, wup, cw, cb, wdn, fing]

    def const_spec(a):
        if a is wout or a is wup or a is wdn:
            return pl.BlockSpec(memory_space=pl.ANY)
        return _const_spec(a.shape) if a is ones or a is fing else _layer_spec(a, layer)

    assert wout.shape[1] % WEIGHT_STAGE_ROWS == 0 and wdn.shape[1] % WEIGHT_STAGE_ROWS == 0
    assert wup.shape[1] % WEIGHT_STAGE_ROWS_WIDE == 0
    return pl.pallas_call(
        functools.partial(_out_kernel, row_len=row_len, final=final, layer=layer),
        grid=(n // t,),
        in_specs=[pl.BlockSpec((t, d), row), _mod_spec(mod4, layer, mod_base, tiles_per_group)]
                 + [pl.BlockSpec((t, a.shape[1]), row) for a in acts[1:]]
                 + [const_spec(a) for a in consts],
        out_specs=pl.BlockSpec((t, d), row),
        out_shape=jax.ShapeDtypeStruct((n, d), F32),
        scratch_shapes=[pltpu.VMEM((t, d), F32), pltpu.VMEM((t, d), BF16),
                        pltpu.VMEM((2, t, 2 * FFN_CHUNK), F32), pltpu.VMEM((2, t, FFN_CHUNK), BF16),
                        pltpu.VMEM((t, d), F32),
                        pltpu.VMEM(wout.shape[1:], BF16), pltpu.VMEM(wup.shape[1:], BF16),
                        pltpu.VMEM(wdn.shape[1:], BF16),
                        pltpu.VMEM((WEIGHT_STAGE_SLOTS, WEIGHT_STAGE_ROWS_WIDE, wup.shape[2]), F32),
                        pltpu.VMEM((WEIGHT_STAGE_SLOTS, WEIGHT_STAGE_ROWS, d), F32),
                        pltpu.SemaphoreType.DMA((WEIGHT_STAGE_SLOTS,)),
                        pltpu.SemaphoreType.DMA((WEIGHT_STAGE_SLOTS,))],
        compiler_params=_params("arbitrary"),
        name="out_ffn",
    )(xt, mod4, *acts[1:], *consts)


def _block_diag(w):
    dep, g, a, b = w.shape
    eye = jnp.eye(g, dtype=w.dtype)
    return (eye[None, :, None, :, None] * w[:, :, :, None, :]).reshape(dep, g * a, g * b)


def _w_in_layout_kernel(wt_ref, o_ref):
    a0, a1 = COL_STARTS[2], COL_STARTS[4]
    wt = wt_ref[...]
    pad = jnp.zeros((IN_COLS - wt.shape[0], wt.shape[1]), wt.dtype)
    o_ref[...] = jnp.concatenate([wt[:a0], wt[a1:], wt[a0:a1], pad], axis=0).T.astype(BF16)


def _w_in_layout_call(w_in):
    dep, d, cols = w_in.shape
    rows = TOKEN_TILE
    return pl.pallas_call(
        _w_in_layout_kernel,
        grid=(dep, d // rows),
        in_specs=[pl.BlockSpec((None, cols, rows), lambda i, j: (i, 0, j))],
        out_specs=pl.BlockSpec((None, rows, IN_COLS), lambda i, j: (i, j, 0)),
        out_shape=jax.ShapeDtypeStruct((dep, d, IN_COLS), BF16),
        compiler_params=_params("arbitrary", "arbitrary"),
        name="w_in_layout",
    )(jnp.swapaxes(w_in, 1, 2))


def _prep_params(norm1_g, norm2_g, w_in, gla_w_a2, gla_b_a2, gla_norm_g, fft_w, conv_w, conv_b, pool_w,
                 pool_scale, w_out, ffn_w_up, ffn_conv_w, ffn_conv_b, ffn_w_down):
    dep = w_in.shape[0]
    w_in_r = _w_in_layout_call(w_in)
    wa2 = jnp.zeros((dep, LANES, 2 * GLA_QK), F32)
    wa2 = wa2.at[:, 0:GLA_RANK, 0:GLA_QK].set(gla_w_a2[:, 0])
    wa2 = wa2.at[:, GLA_RANK:2 * GLA_RANK, GLA_QK:].set(gla_w_a2[:, 1])
    in_params = (norm1_g[:, None, :], w_in_r, wa2.astype(BF16), gla_b_a2.reshape(dep, 1, 2 * GLA_QK),
                 conv_w, conv_b[:, None, :], _block_diag(pool_w).astype(BF16), pool_scale[:, None, :])
    out_params = (jnp.tile(gla_norm_g, (1, GLA_HEADS))[:, None, :], w_out, norm2_g[:, None, :],
                  ffn_w_up, ffn_conv_w, ffn_conv_b[:, None, :], ffn_w_down)
    return in_params, out_params, _block_diag(fft_w).astype(BF16)


def kernel(x, c, ctx, c_ctx, norm1_g, norm2_g, w_mod, b_mod, w_in, gla_w_a2, gla_b_a2, gla_norm_g,
           fft_w, conv_w, conv_b, pool_w, pool_scale, w_out, ffn_w_up, ffn_conv_w, ffn_conv_b,
           ffn_w_down, final_norm_g):
    batch, seq, d = x.shape
    ctx_len = ctx.shape[1]
    depth = w_mod.shape[0]
    assert seq % GLA_BLOCK == 0 and seq % FFT_N2 == 0 and ctx_len == TOKEN_TILE

    cc = jnp.concatenate([c, c_ctx[None, :], jnp.zeros((MOD_ROWS - batch - 1, d), F32)], axis=0)
    mod4 = _mod_call(cc, w_mod, b_mod).reshape(depth, MOD_ROWS, 6, d)
    in_params, out_params, wf_bd = _prep_params(
        norm1_g, norm2_g, w_in, gla_w_a2, gla_b_a2, gla_norm_g, fft_w, conv_w, conv_b, pool_w, pool_scale,
        w_out, ffn_w_up, ffn_conv_w, ffn_conv_b, ffn_w_down)

    xt = x.reshape(batch * seq, d)
    xc = ctx.reshape(batch * ctx_len, d)
    zero_state = jnp.zeros((batch, W_GLA, GLA_QK), F32)
    fing = final_norm_g.reshape(1, d)
    for i in range(depth):
        last = i == depth - 1
        kq, v, la, gate, ufft, ycp = _in_call(xc, mod4, i, batch, 1, in_params, row_len=ctx_len)
        o_f, o_b, s_f, s_b = _gla_call(kq, v, la, zero_state, zero_state, batch=batch)
        if not last:
            yfft = _fft_direct_call(ufft, wf_bd, i, batch=batch)
            xc = _out_call(xc, mod4, i, batch, 1, o_f, o_b, gate, yfft, ycp, out_params, fing,
                           row_len=ctx_len, final=False)

        kq, v, la, gate, ufft, ycp = _in_call(xt, mod4, i, 0, batch, in_params, row_len=GRID_W)
        o_f, o_b, _, _ = _gla_call(kq, v, la, s_f, s_b, batch=batch)
        yfft = _fft_latent_call(ufft, wf_bd, i, batch=batch)
        xt = _out_call(xt, mod4, i, 0, batch, o_f, o_b, gate, yfft, ycp, out_params, fing,
                       row_len=GRID_W, final=last)
    return xt.reshape(batch, seq, d)
```
